```python
import jax, jax.numpy as jnp
from jax import lax
import numpy as np

D_MODEL = 1024
BATCH = 1
SEQ = 16384
DEPTH = 1

EPS = 1e-6
ROPE_THETA = 500000.0
ROT_FRAC = 4
GLA_HEADS = 4
GLA_DK = 64
GLA_DV = 128
GLA_GATE_RANK = 16
GLA_TAU = 16.0
GLA_CHUNK = 64
DSA_HEADS = 8
DSA_HD = 64
IDX_HEADS = 8
IDX_DIM = 64
DSA_TOPK_MAX = 256
Q_BLOCK = 128
N_GROUPS = 4
EXPERTS_PER_GROUP = 8
N_EXPERTS = N_GROUPS * EXPERTS_PER_GROUP
EXPERT_TOPK = 2
D_EXPERT = D_MODEL // 4
N_BRANCHES = 2
N_MOD = 6

GLA_QK = GLA_HEADS * GLA_DK
GLA_V = GLA_HEADS * GLA_DV
DSA_W = DSA_HEADS * DSA_HD
IDX_Q = IDX_HEADS * IDX_DIM
SPLIT_SIZES = (GLA_QK, GLA_QK, GLA_V, GLA_V, GLA_GATE_RANK,
               DSA_W, DSA_W, DSA_W, IDX_Q, IDX_DIM, IDX_HEADS,
               N_BRANCHES * D_MODEL)
D_IN = sum(SPLIT_SIZES)

kernel_name = "hybrid_gla_dsa_hmoe_block"


def rmsnorm(x, g):
    x32 = x.astype(jnp.float32)
    y = x32 * lax.rsqrt(jnp.mean(x32 * x32, axis=-1, keepdims=True) + EPS)
    return (y * g.astype(jnp.float32)).astype(x.dtype)


def layernorm(x, g):
    x32 = x.astype(jnp.float32)
    mu = jnp.mean(x32, axis=-1, keepdims=True)
    xc = x32 - mu
    y = xc * lax.rsqrt(jnp.mean(xc * xc, axis=-1, keepdims=True) + EPS)
    return (y * g.astype(jnp.float32)).astype(x.dtype)


def rope_partial(x, pos):
    hd = x.shape[-1]
    rot = hd // ROT_FRAC
    half = rot // 2
    freqs = jnp.power(jnp.float32(ROPE_THETA), -jnp.arange(half, dtype=jnp.float32) * 2.0 / rot)
    ang = pos.astype(jnp.float32)[:, :, None, None] * freqs
    cos, sin = jnp.cos(ang), jnp.sin(ang)
    x32 = x.astype(jnp.float32)
    x1, x2 = x32[..., :half], x32[..., half:rot]
    out = jnp.concatenate([x1 * cos - x2 * sin, x1 * sin + x2 * cos, x32[..., rot:]], axis=-1)
    return out.astype(x.dtype)


def gla_chunked(q, k, v, la):
    B, S, H, dk = q.shape
    dv = v.shape[-1]
    C = GLA_CHUNK
    N = S // C

    def to_chunks(t):
        return t.astype(jnp.float32).reshape(B, N, C, H, t.shape[-1]).transpose(1, 0, 3, 2, 4)

    causal = jnp.tril(jnp.ones((C, C), dtype=bool))

    def step(state, inp):
        qc, kc, vc, lac = inp
        b = jnp.cumsum(lac, axis=2)
        o_inter = jnp.einsum('bhtd,bhdv->bhtv', qc * jnp.exp(b), state)
        diff = b[:, :, :, None, :] - b[:, :, None, :, :]
        decay = jnp.exp(jnp.where(causal[:, :, None], diff, -jnp.inf))
        scores = jnp.einsum('bhtd,bhsd,bhtsd->bhts', qc, kc, decay)
        o = o_inter + jnp.einsum('bhts,bhsv->bhtv', scores, vc)
        b_last = b[:, :, -1:, :]
        state = (jnp.exp(b_last[:, :, 0, :])[..., None] * state
                 + jnp.einsum('bhsd,bhsv->bhdv', kc * jnp.exp(b_last - b), vc))
        return state, o

    s0 = jnp.zeros((B, H, dk, dv), jnp.float32)
    _, o = lax.scan(step, s0, (to_chunks(q), to_chunks(k), to_chunks(v), to_chunks(la)))
    return o.transpose(1, 0, 3, 2, 4).reshape(B, S, H, dv).astype(q.dtype)


def dsa_attention(q, k, v, qi, ki, wi):
    B, S, H, hd = q.shape
    topk = min(DSA_TOPK_MAX, S // 4)
    nb = S // Q_BLOCK

    def blocks(t):
        return t.reshape((B, nb, Q_BLOCK) + t.shape[2:]).swapaxes(0, 1)

    qpos = jnp.arange(S, dtype=jnp.int32).reshape(nb, Q_BLOCK)
    kpos = jnp.arange(S, dtype=jnp.int32)
    ki32 = ki.astype(jnp.float32)

    def one_block(inp):
        qb, qib, wib, tb = inp
        logits = jnp.einsum('bqhd,bsd->bqhs', qib.astype(jnp.float32), ki32) * (IDX_DIM ** -0.5)
        idx_score = jnp.einsum('bqhs,bqh->bqs', jax.nn.relu(logits), wib.astype(jnp.float32))
        visible = kpos[None, :] <= tb[:, None]
        idx_score = jnp.where(visible[None], idx_score, -jnp.inf)
        _, sel = lax.top_k(idx_score, topk)
        k_sel = jax.vmap(lambda kk, ii: kk[ii])(k, sel)
        v_sel = jax.vmap(lambda vv, ii: vv[ii])(v, sel)
        valid = sel <= tb[None, :, None]
        s = jnp.einsum('bqhd,bqkhd->bhqk', qb.astype(jnp.float32), k_sel.astype(jnp.float32)) * (hd ** -0.5)
        s = jnp.where(valid[:, None], s, -jnp.inf)
        p = jax.nn.softmax(s, axis=-1)
        return jnp.einsum('bhqk,bqkhd->bqhd', p, v_sel.astype(jnp.float32)).astype(q.dtype)

    out = lax.map(one_block, (blocks(q), blocks(qi), blocks(wi), qpos))
    return out.swapaxes(0, 1).reshape(B, S, H, hd)


def hier_moe(h, w_rg, b_rg, w_re, b_re, w_g, w_u, w_d):
    B, S, D = h.shape
    T = B * S
    ht = h.reshape(T, D)
    g_prob = jax.nn.softmax((ht @ w_rg + b_rg).astype(jnp.float32), axis=-1)
    g_sel = jnp.argmax(g_prob, axis=-1)
    p_g = jnp.take_along_axis(g_prob, g_sel[:, None], axis=1)[:, 0]
    e_logits = (ht @ w_re + b_re).astype(jnp.float32).reshape(T, N_GROUPS, EXPERTS_PER_GROUP)
    e_in = jnp.take_along_axis(e_logits, g_sel[:, None, None], axis=1)[:, 0]
    e_prob = jax.nn.softmax(e_in, axis=-1)
    top_p, top_i = lax.top_k(e_prob, EXPERT_TOPK)
    top_p = top_p / jnp.sum(top_p, axis=-1, keepdims=True)
    eid = g_sel[:, None] * EXPERTS_PER_GROUP + top_i
    combine = jnp.sum(jax.nn.one_hot(eid, N_EXPERTS, dtype=jnp.float32)
                      * (p_g[:, None] * top_p)[..., None], axis=1)
    hg = jnp.einsum('td,edf->tef', ht, w_g)
    hu = jnp.einsum('td,edf->tef', ht, w_u)
    act = jax.nn.silu(hg) * hu * combine[..., None].astype(hg.dtype)
    y = jnp.einsum('tef,efd->td', act, w_d)
    return y.reshape(B, S, D)


def setup_inputs(seed: int = 0) -> dict:
    key = jax.random.key(seed)
    ks = jax.random.split(key, 32)
    D = D_MODEL

    def nrm(k, shape, scale):
        return jax.random.normal(k, shape, jnp.float32) * scale

    def gain(k, n):
        return 1.0 + 0.02 * jax.random.normal(k, (DEPTH, n), jnp.float32)

    return {
        "x": nrm(ks[0], (BATCH, SEQ, D), 1.0),
        "c": nrm(ks[1], (BATCH, D), 1.0),
        "positions": jnp.broadcast_to(jnp.arange(SEQ, dtype=jnp.int32), (BATCH, SEQ)),
        "w_ada": nrm(ks[2], (DEPTH, D, N_MOD * D), 0.2 * D ** -0.5),
        "b_ada": nrm(ks[3], (DEPTH, N_MOD * D), 0.02),
        "g_pre_mix": gain(ks[4], D),
        "g_post_mix": gain(ks[5], D),
        "g_pre_ffn": gain(ks[6], D),
        "g_post_ffn": gain(ks[7], D),
        "w_in": nrm(ks[8], (DEPTH, D, D_IN), D ** -0.5),
        "w_gla_a2": nrm(ks[9], (DEPTH, GLA_GATE_RANK, GLA_QK), GLA_GATE_RANK ** -0.5),
        "b_gla_a": nrm(ks[10], (DEPTH, GLA_QK), 0.1),
        "g_gla_out": gain(ks[11], GLA_DV),
        "g_idx_k": gain(ks[12], IDX_DIM),
        "w_proj_gla": nrm(ks[13], (DEPTH, GLA_V, D), GLA_V ** -0.5),
        "w_proj_dsa": nrm(ks[14], (DEPTH, DSA_W, D), DSA_W ** -0.5),
        "w_out": nrm(ks[15], (DEPTH, D, D), D ** -0.5),
        "w_router_g": nrm(ks[16], (DEPTH, D, N_GROUPS), D ** -0.5),
        "b_router_g": nrm(ks[17], (DEPTH, N_GROUPS), 0.01),
        "w_router_e": nrm(ks[18], (DEPTH, D, N_EXPERTS), D ** -0.5),
        "b_router_e": nrm(ks[19], (DEPTH, N_EXPERTS), 0.01),
        "w_e_gate": nrm(ks[20], (DEPTH, N_EXPERTS, D, D_EXPERT), D ** -0.5),
        "w_e_up": nrm(ks[21], (DEPTH, N_EXPERTS, D, D_EXPERT), D ** -0.5),
        "w_e_down": nrm(ks[22], (DEPTH, N_EXPERTS, D_EXPERT, D), D_EXPERT ** -0.5),
    }


def reference(x, c, positions, w_ada, b_ada, g_pre_mix, g_post_mix, g_pre_ffn, g_post_ffn,
              w_in, w_gla_a2, b_gla_a, g_gla_out, g_idx_k, w_proj_gla, w_proj_dsa, w_out,
              w_router_g, b_router_g, w_router_e, b_router_e, w_e_gate, w_e_up, w_e_down):
    B, S, D = x.shape
    offsets = [int(o) for o in np.cumsum(SPLIT_SIZES)[:-1]]
    for l in range(DEPTH):
        mod = (jax.nn.silu(c) @ w_ada[l] + b_ada[l]).reshape(B, N_MOD, D)
        shift_m, scale_m, gate_m, shift_f, scale_f, gate_f = [mod[:, i][:, None, :] for i in range(N_MOD)]

        h = rmsnorm(x, g_pre_mix[l]) * (1.0 + scale_m) + shift_m
        proj = h @ w_in[l]
        (gq, gk, gv, gg, ga_low, dq, dk, dv, iq, ik, iw, bg) = jnp.split(proj, offsets, axis=-1)

        q_a = gq.reshape(B, S, GLA_HEADS, GLA_DK) * (GLA_DK ** -0.5)
        k_a = gk.reshape(B, S, GLA_HEADS, GLA_DK)
        v_a = gv.reshape(B, S, GLA_HEADS, GLA_DV)
        la = jax.nn.log_sigmoid((ga_low @ w_gla_a2[l] + b_gla_a[l]).astype(jnp.float32)) / GLA_TAU
        o_a = gla_chunked(q_a, k_a, v_a, la.reshape(B, S, GLA_HEADS, GLA_DK))
        o_a = rmsnorm(o_a, g_gla_out[l]) * jax.nn.silu(gg.reshape(B, S, GLA_HEADS, GLA_DV))
        y_gla = o_a.reshape(B, S, GLA_V) @ w_proj_gla[l]

        q_b = rope_partial(dq.reshape(B, S, DSA_HEADS, DSA_HD), positions)
        k_b = rope_partial(dk.reshape(B, S, DSA_HEADS, DSA_HD), positions)
        v_b = dv.reshape(B, S, DSA_HEADS, DSA_HD)
        qi = rope_partial(iq.reshape(B, S, IDX_HEADS, IDX_DIM), positions)
        ki = rope_partial(layernorm(ik, g_idx_k[l])[:, :, None, :], positions)[:, :, 0, :]
        wi = iw * (IDX_HEADS ** -0.5)
        o_b = dsa_attention(q_b, k_b, v_b, qi, ki, wi)
        y_dsa = o_b.reshape(B, S, DSA_W) @ w_proj_dsa[l]

        gates = jax.nn.sigmoid(bg.reshape(B, S, N_BRANCHES, D))
        mix = gates[:, :, 0] * y_gla + gates[:, :, 1] * y_dsa
        out = mix @ w_out[l]
        x = x + gate_m * rmsnorm(out, g_post_mix[l])

        h = rmsnorm(x, g_pre_ffn[l]) * (1.0 + scale_f) + shift_f
        y = hier_moe(h, w_router_g[l], b_router_g[l], w_router_e[l], b_router_e[l],
                     w_e_gate[l], w_e_up[l], w_e_down[l])
        x = x + gate_f * rmsnorm(y, g_post_ffn[l])
    return x
```

```python
import functools

import numpy as np
import jax
import jax.numpy as jnp
from jax import lax
from jax.experimental import pallas as pl
from jax.experimental.pallas import tpu as pltpu

F32 = jnp.float32
BF16 = jnp.bfloat16
I32 = jnp.int32

D_MODEL = 1024
EPS = 1e-6
ROPE_THETA = 500000.0
ROT_FRAC = 4
GLA_HEADS = 4
GLA_DK = 64
GLA_DV = 128
GLA_GATE_RANK = 16
GLA_TAU = 16.0
GLA_CHUNK = 64
DSA_HEADS = 8
DSA_HD = 64
IDX_HEADS = 8
IDX_DIM = 64
DSA_TOPK_MAX = 256
N_GROUPS = 4
EXPERTS_PER_GROUP = 8
N_EXPERTS = N_GROUPS * EXPERTS_PER_GROUP
D_EXPERT = D_MODEL // 4
N_MOD = 6

GLA_QK = GLA_HEADS * GLA_DK
GLA_V = GLA_HEADS * GLA_DV
DSA_W = DSA_HEADS * DSA_HD
IDX_Q = IDX_HEADS * IDX_DIM

LANES = 128
VMEM_LIMIT_BYTES = 56 * 1024 * 1024

PROJ_ROWS = 512
GLA_ROWS = 512
DSA_TQ = 128
DSA_CH = 512
MOE_ROWS = 512
MOE_EB = 4
MASKED = -1e30
GLA_LEVELS = 7


def _dot(a, b):
    return jnp.dot(a, b, preferred_element_type=F32)


def _dot_nt(a, b):
    return lax.dot_general(a, b, (((1,), (1,)), ((), ())), preferred_element_type=F32)


def _dot_tn(a, b):
    return lax.dot_general(a, b, (((0,), (0,)), ((), ())), preferred_element_type=F32)


def _sigmoid(x):
    return 1.0 / (1.0 + jnp.exp(-x))


def _rms(x, g):
    ms = jnp.mean(x * x, axis=-1, keepdims=True)
    return x * lax.rsqrt(ms + EPS) * g


def _modulated(x, g, mod_ref, shift_row, scale_row):
    return (_rms(x, g) * (1.0 + mod_ref[scale_row:scale_row + 1, :])
            + mod_ref[shift_row:shift_row + 1, :])


def _params(sem):
    return pltpu.CompilerParams(dimension_semantics=sem, vmem_limit_bytes=VMEM_LIMIT_BYTES)


def _full(shape):
    return pl.BlockSpec(shape, lambda *_: (0,) * len(shape))


def _resident(shape):
    return pl.BlockSpec(shape, lambda *_: (0,) * len(shape), pipeline_mode=pl.Buffered(1))


def _ada_kernel(c_ref, w_ref, b_ref, o_ref):
    c = c_ref[...]
    a = c * _sigmoid(c)
    o_ref[...] = _dot(a.astype(BF16), w_ref[...].astype(BF16)) + b_ref[...]


def _ada(c8, w, b):
    n = w.shape[1]
    bn = 1536
    return pl.pallas_call(
        _ada_kernel,
        grid=(n // bn,),
        in_specs=[_full(c8.shape),
                  pl.BlockSpec((D_MODEL, bn), lambda j: (0, j)),
                  pl.BlockSpec((1, bn), lambda j: (0, j))],
        out_specs=pl.BlockSpec((8, bn), lambda j: (0, j)),
        out_shape=jax.ShapeDtypeStruct((8, n), F32),
        compiler_params=_params(("arbitrary",)),
        name="ada",
    )(c8, w, b)


def _gla_proj_kernel(x_ref, mod_ref, g_ref, w_ref, wga_ref, wa2_ref, ba_ref,
                     q_ref, k_ref, v_ref, gg_ref, la_ref):
    h = _modulated(x_ref[...], g_ref[...], mod_ref, 0, 1)
    hb = h.astype(BF16)
    p = _dot(hb, w_ref[...])
    q_ref[...] = p[:, 0:GLA_QK] * (GLA_DK ** -0.5)
    k_ref[...] = p[:, GLA_QK:2 * GLA_QK]
    v_ref[...] = p[:, 2 * GLA_QK:2 * GLA_QK + GLA_V]
    gg_ref[...] = p[:, 2 * GLA_QK + GLA_V:]
    ga = _dot(hb, wga_ref[...])
    z = _dot(ga.astype(BF16), wa2_ref[...]) + ba_ref[...]
    log_sig = jnp.minimum(z, 0.0) - jnp.log1p(jnp.exp(-jnp.abs(z)))
    la_ref[...] = log_sig * (1.0 / GLA_TAU)


def _gla_proj(x, mod, g, w, wga, wa2, ba):
    s = x.shape[0]
    tm = PROJ_ROWS
    row = lambda n: pl.BlockSpec((tm, n), lambda i: (i, 0))
    return pl.pallas_call(
        _gla_proj_kernel,
        grid=(s // tm,),
        in_specs=[row(D_MODEL), _full(mod.shape), _full(g.shape), _full(w.shape),
                  _full(wga.shape), _full(wa2.shape), _full(ba.shape)],
        out_specs=[row(GLA_QK), row(GLA_QK), row(GLA_V), row(GLA_V), row(GLA_QK)],
        out_shape=[jax.ShapeDtypeStruct((s, GLA_QK), F32), jax.ShapeDtypeStruct((s, GLA_QK), F32),
                   jax.ShapeDtypeStruct((s, GLA_V), F32), jax.ShapeDtypeStruct((s, GLA_V), F32),
                   jax.ShapeDtypeStruct((s, GLA_QK), F32)],
        compiler_params=_params(("arbitrary",)),
        name="gla_proj",
    )(x, mod, g, w, wga, wa2, ba)


def _dsa_proj_kernel(x_ref, mod_ref, g_ref, pos_ref, fv_ref, sg_ref, w_ref, wsm_ref, gik_ref,
                     qb_ref, kb_ref, vb_ref, qi_ref, ki_ref, wi_ref):
    tm = x_ref.shape[0]
    h = _modulated(x_ref[...], g_ref[...], mod_ref, 0, 1)
    hb = h.astype(BF16)
    ang = pos_ref[...] * fv_ref[...]
    cs = jnp.cos(ang)
    sn = jnp.sin(ang) * sg_ref[...]
    lane = lax.broadcasted_iota(I32, (tm, LANES), 1)
    first = (lane & (DSA_HD - 1)) < (DSA_HD // ROT_FRAC // 2)

    def rope(t):
        width = t.shape[1]
        rep = width // LANES
        tile = (lambda a: jnp.concatenate([a] * rep, axis=1)) if rep > 1 else (lambda a: a)
        half = DSA_HD // ROT_FRAC // 2
        fwd = pltpu.roll(t, half, 1)
        bwd = pltpu.roll(t, width - half, 1)
        partner = jnp.where(tile(first), bwd, fwd)
        return t * tile(cs) + partner * tile(sn)

    p = _dot(hb, w_ref[...])
    qb_ref[...] = (rope(p[:, 0:DSA_W]) * (DSA_HD ** -0.5)).astype(BF16)
    kb_ref[...] = rope(p[:, DSA_W:2 * DSA_W]).astype(BF16)
    vb_ref[...] = p[:, 2 * DSA_W:3 * DSA_W].astype(BF16)
    qi_ref[...] = (rope(p[:, 3 * DSA_W:]) * (IDX_DIM ** -0.5)).astype(BF16)

    sm = _dot(hb, wsm_ref[...])
    is_ik = lane < IDX_DIM
    mu = jnp.sum(jnp.where(is_ik, sm, 0.0), axis=-1, keepdims=True) * (1.0 / IDX_DIM)
    xc = jnp.where(is_ik, sm - mu, 0.0)
    var = jnp.sum(xc * xc, axis=-1, keepdims=True) * (1.0 / IDX_DIM)
    y = xc * lax.rsqrt(var + EPS) * gik_ref[...]
    ki_ref[...] = rope(y)[:, 0:IDX_DIM].astype(BF16)
    wi_ref[...] = sm * (IDX_HEADS ** -0.5)


def _dsa_proj(x, mod, g, pos, fv, sg, w, wsm, gik):
    s = x.shape[0]
    tm = PROJ_ROWS
    row = lambda n: pl.BlockSpec((tm, n), lambda i: (i, 0))
    return pl.pallas_call(
        _dsa_proj_kernel,
        grid=(s // tm,),
        in_specs=[row(D_MODEL), _full(mod.shape), _full(g.shape), row(1), _full(fv.shape),
                  _full(sg.shape), _full(w.shape), _full(wsm.shape), _full(gik.shape)],
        out_specs=[row(DSA_W), row(DSA_W), row(DSA_W), row(IDX_Q), row(IDX_DIM), row(LANES)],
        out_shape=[jax.ShapeDtypeStruct((s, DSA_W), BF16), jax.ShapeDtypeStruct((s, DSA_W), BF16),
                   jax.ShapeDtypeStruct((s, DSA_W), BF16), jax.ShapeDtypeStruct((s, IDX_Q), BF16),
                   jax.ShapeDtypeStruct((s, IDX_DIM), BF16), jax.ShapeDtypeStruct((s, LANES), F32)],
        compiler_params=_params(("arbitrary",)),
        name="dsa_proj",
    )(x, mod, g, pos, fv, sg, w, wsm, gik)


def _gla_constants():
    c = GLA_CHUNK
    tril = np.tril(np.ones((c, c), np.float32))
    t = np.arange(c)
    mats = [tril]
    masks = [(t[:, None] == t[None, :])]
    for hs in (32, 16, 8, 4, 2, 1):
        blk = 2 * hs
        r = (t // blk) * blk + hs - 1
        mats.append(tril - tril[r, :])
        same = (t[:, None] // blk) == (t[None, :] // blk)
        masks.append(same & ((t[:, None] % blk) >= hs) & ((t[None, :] % blk) < hs))
    m_all = np.concatenate(mats, axis=0)
    lvl = np.stack([np.tile(m.astype(np.float32), (1, GLA_HEADS)) for m in masks])
    hrow = np.arange(GLA_HEADS * c) // c
    wmask = (hrow[:, None] == (np.arange(GLA_QK) // GLA_DK)[None, :]).astype(np.float32)
    vmask = (hrow[:, None] == (np.arange(GLA_V) // GLA_DV)[None, :]).astype(np.float32)
    smask = ((np.arange(GLA_V) // GLA_DV)[:, None] == (np.arange(GLA_QK) // GLA_DK)[None, :])
    return (jnp.asarray(m_all, BF16), jnp.asarray(lvl), jnp.asarray(wmask), jnp.asarray(vmask),
            jnp.asarray(smask.astype(np.float32)))


def _gla_kernel(q_ref, k_ref, v_ref, gg_ref, la_ref, gout_ref, mall_ref, lvl_ref, wmask_ref,
                vmask_ref, smask_ref, o_ref, st_ref):
    c = GLA_CHUNK

    @pl.when(pl.program_id(0) == 0)
    def _():
        st_ref[...] = jnp.zeros_like(st_ref)

    m_all = mall_ref[...]
    wmask = wmask_ref[...]
    vmask = vmask_ref[...]
    smask = smask_ref[...]
    gout = gout_ref[...]

    def chunk(ci, carry):
        r0 = pl.multiple_of(ci * c, c)
        rows = pl.ds(r0, c)
        q = q_ref[rows, :]
        k = k_ref[rows, :]
        v = v_ref[rows, :]
        la = la_ref[rows, :]
        hi = la.astype(BF16)
        r1 = la - hi.astype(F32)
        mid = r1.astype(BF16)
        lo = (r1 - mid.astype(F32)).astype(BF16)
        dall = _dot(m_all, hi) + _dot(m_all, mid) + _dot(m_all, lo)
        b = dall[0:c]
        b_last = b[c - 1:c, :]
        qhat = q * jnp.exp(b)
        khat = k * jnp.exp(b_last - b)

        a = jnp.zeros((c, GLA_HEADS * c), F32)
        for lv in range(GLA_LEVELS):
            if lv == 0:
                qt, kt = q, k
            else:
                d = dall[lv * c:(lv + 1) * c]
                qt = q * jnp.exp(jnp.minimum(d, 0.0))
                kt = k * jnp.exp(jnp.minimum(-d, 0.0))
            w = (jnp.concatenate([kt] * GLA_HEADS, axis=0) * wmask).astype(BF16)
            a = a + lvl_ref[lv] * _dot_nt(qt.astype(BF16), w)

        st = st_ref[...]
        vbd = (jnp.concatenate([v] * GLA_HEADS, axis=0) * vmask).astype(BF16)
        o = _dot(a.astype(BF16), vbd) + _dot_nt(qhat.astype(BF16), st.astype(BF16))
        st_ref[...] = st * jnp.exp(b_last) + smask * _dot_tn(v.astype(BF16), khat.astype(BF16))

        parts = []
        for hh in range(GLA_HEADS):
            oh = o[:, hh * GLA_DV:(hh + 1) * GLA_DV]
            ms = jnp.mean(oh * oh, axis=-1, keepdims=True)
            parts.append(oh * lax.rsqrt(ms + EPS))
        gg = gg_ref[rows, :]
        o_ref[rows, :] = jnp.concatenate(parts, axis=1) * gout * (gg * _sigmoid(gg))
        return carry

    lax.fori_loop(0, q_ref.shape[0] // c, chunk, 0)


def _gla(q, k, v, gg, la, gout):
    s = q.shape[0]
    tb = GLA_ROWS
    consts = _gla_constants()
    row = lambda n: pl.BlockSpec((tb, n), lambda i: (i, 0))
    return pl.pallas_call(
        _gla_kernel,
        grid=(s // tb,),
        in_specs=[row(GLA_QK), row(GLA_QK), row(GLA_V), row(GLA_V), row(GLA_QK), _full(gout.shape)]
                 + [_full(a.shape) for a in consts],
        out_specs=row(GLA_V),
        out_shape=jax.ShapeDtypeStruct((s, GLA_V), F32),
        scratch_shapes=[pltpu.VMEM((GLA_V, GLA_QK), F32)],
        compiler_params=_params(("arbitrary",)),
        name="gla",
    )(q, k, v, gg, la, gout, *consts)


def _key_to_float(u):
    key = u ^ jnp.int32(-2147483648)
    bits = key ^ ((key >> 31) & jnp.int32(0x7FFFFFFF))
    return lax.bitcast_convert_type(bits, F32)


def _dsa_kernel(qi_ref, kit_ref, wi_ref, qb_ref, kt_ref, v_ref, o_ref,
                score_ref, wb_ref, qbd_ref, m_ref, l_ref, acc_ref, *, seq, topk):
    tq = DSA_TQ
    ch = DSA_CH
    hg = 4
    gw = hg * DSA_HD
    i = pl.program_id(0)
    n_ch = ((i + 1) * tq + ch - 1) // ch
    qpos = lax.broadcasted_iota(I32, (tq, ch), 0) + i * tq
    lane_c = lax.broadcasted_iota(I32, (tq, ch), 1)

    def tile_l(a, width):
        return jnp.concatenate([a] * (width // a.shape[1]), axis=1)

    wv = wi_ref[...]
    for h in range(IDX_HEADS):
        wb_ref[h] = jnp.broadcast_to(wv[:, IDX_DIM + h:IDX_DIM + h + 1], (tq, LANES))
    qi = qi_ref[0]

    def score_body(c, carry):
        c0 = pl.multiple_of(c * ch, ch)
        lg = _dot(qi, kit_ref[:, pl.ds(c0, ch)])
        sc = jnp.maximum(lg[0:tq], 0.0) * tile_l(wb_ref[0], ch)
        for h in range(1, IDX_HEADS):
            sc = sc + jnp.maximum(lg[h * tq:(h + 1) * tq], 0.0) * tile_l(wb_ref[h], ch)
        score_ref[:, pl.ds(c0, ch)] = jnp.where(lane_c + c0 <= qpos, sc, -jnp.inf)
        return carry

    lax.fori_loop(0, n_ch, score_body, 0)

    def count(pred):
        def body(c, acc):
            c0 = pl.multiple_of(c * ch, ch)
            one = jnp.where(pred(score_ref[:, pl.ds(c0, ch)], c0), 1.0, 0.0)
            for j in range(ch // LANES):
                acc = acc + one[:, j * LANES:(j + 1) * LANES]
            return acc
        acc = lax.fori_loop(0, n_ch, body, jnp.zeros((tq, LANES), F32))
        return jnp.sum(acc, axis=1, keepdims=True)

    def bit_body(p, t_u):
        cand = t_u | jnp.left_shift(jnp.int32(1), 31 - p)
        tb = jnp.broadcast_to(_key_to_float(cand), (tq, ch))
        cnt = count(lambda blk, c0: blk >= tb)
        return jnp.where(cnt >= float(topk), cand, t_u)

    t_u = lax.fori_loop(0, 32, bit_body, jnp.zeros((tq, 1), I32))
    below_all = (t_u >= 0) & (t_u < jnp.int32(0x00800000))
    thr = jnp.where(below_all, -jnp.inf, _key_to_float(t_u))
    thr_b = jnp.broadcast_to(thr, (tq, ch))
    cnt_gt = count(lambda blk, c0: blk > thr_b)
    cnt_ge = count(lambda blk, c0: blk >= thr_b)
    need = float(topk) - cnt_gt
    tie_row = ((cnt_ge - cnt_gt) > need) & jnp.logical_not(below_all)
    row_pos = lax.broadcasted_iota(I32, (tq, 1), 0) + i * tq
    j_default = jnp.where(below_all, row_pos, jnp.int32(seq))

    def tie_break(_):
        nbits = max(1, int(np.ceil(np.log2(seq))))

        def jbit(p, m):
            cand = m | jnp.left_shift(jnp.int32(1), nbits - 1 - p)
            cand_b = jnp.broadcast_to(cand, (tq, ch))
            cnt = count(lambda blk, c0: (blk == thr_b) & (lane_c + c0 < cand_b))
            return jnp.where(cnt < need, cand, m)

        m = lax.fori_loop(0, nbits, jbit, jnp.zeros((tq, 1), I32))
        return jnp.where(tie_row, m, j_default)

    any_tie = jnp.max(jnp.where(tie_row, 1.0, 0.0)) > 0.0
    j_last = lax.cond(any_tie, tie_break, lambda _: j_default, 0)
    j_b = jnp.broadcast_to(j_last, (tq, ch))

    qt = qb_ref[...]
    head_of_lane = lax.broadcasted_iota(I32, (tq, gw), 1) >> 6
    for g in range(2):
        qg = qt[:, g * gw:(g + 1) * gw]
        for h in range(hg):
            qbd_ref[g, h * tq:(h + 1) * tq, :] = jnp.where(head_of_lane == h, qg, jnp.zeros_like(qg))
    m_ref[...] = jnp.full(m_ref.shape, MASKED, F32)
    l_ref[...] = jnp.zeros(l_ref.shape, F32)
    acc_ref[...] = jnp.zeros(acc_ref.shape, F32)

    def attn_body(c, carry):
        c0 = pl.multiple_of(c * ch, ch)
        blk = score_ref[:, pl.ds(c0, ch)]
        sel = (blk > thr_b) | ((blk == thr_b) & (lane_c + c0 <= j_b))
        sel4 = jnp.concatenate([sel] * hg, axis=0)
        for g in range(2):
            s = _dot(qbd_ref[g], kt_ref[g * gw:(g + 1) * gw, pl.ds(c0, ch)])
            s = jnp.where(sel4, s, MASKED)
            m_prev = m_ref[g]
            m_new = jnp.maximum(m_prev, jnp.max(s, axis=1, keepdims=True))
            alpha = jnp.exp(m_prev - m_new)
            p = jnp.exp(s - tile_l(m_new, ch))
            l_ref[g] = alpha * l_ref[g] + jnp.sum(p, axis=1, keepdims=True)
            acc_ref[g] = (acc_ref[g] * tile_l(alpha, gw)
                          + _dot(p.astype(BF16), v_ref[pl.ds(c0, ch), g * gw:(g + 1) * gw]))
            m_ref[g] = m_new
        return carry

    lax.fori_loop(0, n_ch, attn_body, 0)

    outs = []
    for g in range(2):
        a = acc_ref[g] * tile_l(1.0 / l_ref[g], gw)
        og = jnp.zeros((tq, gw), F32)
        for h in range(hg):
            og = og + jnp.where(head_of_lane == h, a[h * tq:(h + 1) * tq], 0.0)
        outs.append(og)
    o_ref[...] = jnp.concatenate(outs, axis=1)


def _dsa(qi_r, kit, wi, qb, kt, v, topk):
    s = qb.shape[0]
    tq = DSA_TQ
    return pl.pallas_call(
        functools.partial(_dsa_kernel, seq=s, topk=topk),
        grid=(s // tq,),
        in_specs=[pl.BlockSpec((1, IDX_HEADS * tq, IDX_DIM), lambda i: (i, 0, 0)),
                  _resident(kit.shape),
                  pl.BlockSpec((tq, LANES), lambda i: (i, 0)),
                  pl.BlockSpec((tq, DSA_W), lambda i: (i, 0)),
                  _resident(kt.shape),
                  _resident(v.shape)],
        out_specs=pl.BlockSpec((tq, DSA_W), lambda i: (i, 0)),
        out_shape=jax.ShapeDtypeStruct((s, DSA_W), F32),
        scratch_shapes=[pltpu.VMEM((tq, s), F32),
                        pltpu.VMEM((IDX_HEADS, tq, LANES), F32),
                        pltpu.VMEM((2, 4 * tq, 4 * DSA_HD), BF16),
                        pltpu.VMEM((2, 4 * tq, LANES), F32),
                        pltpu.VMEM((2, 4 * tq, LANES), F32),
                        pltpu.VMEM((2, 4 * tq, 4 * DSA_HD), F32)],
        compiler_params=_params(("arbitrary",)),
        name="dsa",
    )(qi_r, kit, wi, qb, kt, v)


def _merge_kernel(x_ref, mod_ref, gpre_ref, gpost_ref, oa_ref, ob_ref, wbg_ref, wpg_ref, wpd_ref,
                  wout_ref, o_ref):
    x = x_ref[...]
    h = _modulated(x, gpre_ref[...], mod_ref, 0, 1)
    gates = _sigmoid(_dot(h.astype(BF16), wbg_ref[...]))
    yg = _dot(oa_ref[...].astype(BF16), wpg_ref[...])
    yd = _dot(ob_ref[...].astype(BF16), wpd_ref[...])
    mix = gates[:, 0:D_MODEL] * yg + gates[:, D_MODEL:] * yd
    out = _dot(mix.astype(BF16), wout_ref[...])
    o_ref[...] = x + mod_ref[2:3, :] * _rms(out, gpost_ref[...])


def _merge(x, mod, gpre, gpost, oa, ob, wbg, wpg, wpd, wout):
    s = x.shape[0]
    tm = PROJ_ROWS
    row = lambda n: pl.BlockSpec((tm, n), lambda i: (i, 0))
    return pl.pallas_call(
        _merge_kernel,
        grid=(s // tm,),
        in_specs=[row(D_MODEL), _full(mod.shape), _full(gpre.shape), _full(gpost.shape),
                  row(GLA_V), row(DSA_W), _full(wbg.shape), _full(wpg.shape), _full(wpd.shape),
                  _full(wout.shape)],
        out_specs=row(D_MODEL),
        out_shape=jax.ShapeDtypeStruct((s, D_MODEL), F32),
        compiler_params=_params(("arbitrary",)),
        name="merge",
    )(x, mod, gpre, gpost, oa, ob, wbg, wpg, wpd, wout)


def _router(lg):
    t = lg.shape[0]
    lane = lax.broadcasted_iota(I32, (t, LANES), 1)
    lanef = lane.astype(F32)
    big = 1e9
    gm = lane < N_GROUPS
    gmax = jnp.max(jnp.where(gm, lg, -jnp.inf), axis=1, keepdims=True)
    gsum = jnp.sum(jnp.where(gm, jnp.exp(lg - gmax), 0.0), axis=1, keepdims=True)
    p_g = 1.0 / gsum
    g_sel = jnp.min(jnp.where(gm & (lg == gmax), lanef, big), axis=1, keepdims=True)
    lo = N_GROUPS + EXPERTS_PER_GROUP * g_sel
    em = (lanef >= lo) & (lanef < lo + EXPERTS_PER_GROUP)
    m1 = jnp.max(jnp.where(em, lg, -jnp.inf), axis=1, keepdims=True)
    i1 = jnp.min(jnp.where(em & (lg == m1), lanef, big), axis=1, keepdims=True)
    em2 = em & (lanef != i1)
    m2 = jnp.max(jnp.where(em2, lg, -jnp.inf), axis=1, keepdims=True)
    i2 = jnp.min(jnp.where(em2 & (lg == m2), lanef, big), axis=1, keepdims=True)
    e2 = jnp.exp(m2 - m1)
    inv = 1.0 / (1.0 + e2)
    return (jnp.where(lanef == i1, p_g * inv, 0.0) + jnp.where(lanef == i2, p_g * (e2 * inv), 0.0))


def _moe_kernel(x_ref, mod_ref, gpre_ref, gpost_ref, wr_ref, br_ref, wg_ref, wu_ref, wd_ref,
                o_ref, hb_ref, comb_ref, acc_ref):
    j = pl.program_id(1)
    tm = x_ref.shape[0]

    @pl.when(j == 0)
    def _():
        h = _modulated(x_ref[...], gpre_ref[...], mod_ref, 3, 4)
        hb = h.astype(BF16)
        hb_ref[...] = hb
        comb_ref[...] = _router(_dot(hb, wr_ref[...]) + br_ref[...])
        acc_ref[...] = jnp.zeros_like(acc_ref)

    hb = hb_ref[...]
    hgate = _dot(hb, wg_ref[...])
    hup = _dot(hb, wu_ref[...])
    act = hgate * _sigmoid(hgate) * hup
    comb = comb_ref[...]
    lane = lax.broadcasted_iota(I32, (tm, LANES), 1)
    parts = []
    for e in range(MOE_EB):
        sel = lane == (N_GROUPS + j * MOE_EB + e)
        cw = jnp.sum(jnp.where(sel, comb, 0.0), axis=1, keepdims=True)
        parts.append((act[:, e * D_EXPERT:(e + 1) * D_EXPERT] * cw).astype(BF16))
    acc_ref[...] += _dot(jnp.concatenate(parts, axis=1), wd_ref[...])

    @pl.when(j == pl.num_programs(1) - 1)
    def _():
        o_ref[...] = x_ref[...] + mod_ref[5:6, :] * _rms(acc_ref[...], gpost_ref[...])


def _moe(x, mod, gpre, gpost, wr, br, wg, wu, wd):
    s = x.shape[0]
    tm = MOE_ROWS
    bw = MOE_EB * D_EXPERT
    return pl.pallas_call(
        _moe_kernel,
        grid=(s // tm, N_EXPERTS // MOE_EB),
        in_specs=[pl.BlockSpec((tm, D_MODEL), lambda i, j: (i, 0)),
                  _full(mod.shape), _full(gpre.shape), _full(gpost.shape), _full(wr.shape),
                  _full(br.shape),
                  pl.BlockSpec((D_MODEL, bw), lambda i, j: (0, j)),
                  pl.BlockSpec((D_MODEL, bw), lambda i, j: (0, j)),
                  pl.BlockSpec((bw, D_MODEL), lambda i, j: (j, 0))],
        out_specs=pl.BlockSpec((tm, D_MODEL), lambda i, j: (i, 0)),
        out_shape=jax.ShapeDtypeStruct((s, D_MODEL), F32),
        scratch_shapes=[pltpu.VMEM((tm, D_MODEL), BF16), pltpu.VMEM((tm, LANES), F32),
                        pltpu.VMEM((tm, D_MODEL), F32)],
        compiler_params=_params(("arbitrary", "arbitrary")),
        name="moe",
    )(x, mod, gpre, gpost, wr, br, wg, wu, wd)


def _rope_lane_constants():
    rot = DSA_HD // ROT_FRAC
    half = rot // 2
    freqs = np.power(np.float32(ROPE_THETA), -np.arange(half, dtype=np.float32) * np.float32(2.0) / rot)
    j = np.arange(LANES) % DSA_HD
    fv = np.where(j < rot, freqs[j % half], 0.0).astype(np.float32)
    sg = np.where(j < half, -1.0, np.where(j < rot, 1.0, 0.0)).astype(np.float32)
    return jnp.asarray(fv)[None, :], jnp.asarray(sg)[None, :]


def _pad_cols(w, n):
    return jnp.pad(w, ((0, 0), (0, n - w.shape[1])))


def _layer(x, c, pos, w_ada, b_ada, g_pre_mix, g_post_mix, g_pre_ffn, g_post_ffn, w_in, w_gla_a2,
           b_gla_a, g_gla_out, g_idx_k, w_proj_gla, w_proj_dsa, w_out, w_router_g, b_router_g,
           w_router_e, b_router_e, w_e_gate, w_e_up, w_e_down):
    s = x.shape[0]
    mod = _ada(jnp.broadcast_to(c, (8, D_MODEL)), w_ada, b_ada[None, :])[0].reshape(N_MOD, D_MODEL)

    o = np.cumsum((GLA_QK, GLA_QK, GLA_V, GLA_V, GLA_GATE_RANK, DSA_W, DSA_W, DSA_W, IDX_Q, IDX_DIM,
                   IDX_HEADS, 2 * D_MODEL))
    wb = w_in.astype(BF16)
    w_gla = wb[:, 0:o[3]]
    w_ga = _pad_cols(wb[:, o[3]:o[4]], LANES)
    w_dsa = wb[:, o[4]:o[8]]
    w_sm = _pad_cols(jnp.concatenate([wb[:, o[8]:o[9]], wb[:, o[9]:o[10]]], axis=1), LANES)
    w_bg = wb[:, o[10]:o[11]]
    w_a2 = jnp.pad(w_gla_a2.astype(BF16), ((0, LANES - GLA_GATE_RANK), (0, 0)))

    q_a, k_a, v_a, gg, la = _gla_proj(x, mod, g_pre_mix[None, :], w_gla, w_ga, w_a2, b_gla_a[None, :])
    o_a = _gla(q_a, k_a, v_a, gg, la, jnp.tile(g_gla_out, GLA_HEADS)[None, :])

    fv, sg = _rope_lane_constants()
    gik = jnp.pad(g_idx_k, (0, LANES - IDX_DIM))[None, :]
    q_b, k_b, v_b, qi, ki, wi = _dsa_proj(x, mod, g_pre_mix[None, :], pos.astype(F32)[:, None], fv, sg,
                                          w_dsa, w_sm, gik)
    nqb = s // DSA_TQ
    qi_r = qi.reshape(nqb, DSA_TQ, IDX_HEADS, IDX_DIM).transpose(0, 2, 1, 3).reshape(
        nqb, IDX_HEADS * DSA_TQ, IDX_DIM)
    o_b = _dsa(qi_r, ki.T, wi, q_b, k_b.T, v_b, min(DSA_TOPK_MAX, s // 4))

    x1 = _merge(x, mod, g_pre_mix[None, :], g_post_mix[None, :], o_a, o_b, w_bg,
                w_proj_gla.astype(BF16), w_proj_dsa.astype(BF16), w_out.astype(BF16))

    wr = _pad_cols(jnp.concatenate([w_router_g, w_router_e], axis=1).astype(BF16), LANES)
    br = jnp.pad(jnp.concatenate([b_router_g, b_router_e]), (0, LANES - N_GROUPS - N_EXPERTS))[None, :]
    wg = w_e_gate.astype(BF16).transpose(1, 0, 2).reshape(D_MODEL, N_EXPERTS * D_EXPERT)
    wu = w_e_up.astype(BF16).transpose(1, 0, 2).reshape(D_MODEL, N_EXPERTS * D_EXPERT)
    wd = w_e_down.astype(BF16).reshape(N_EXPERTS * D_EXPERT, D_MODEL)
    return _moe(x1, mod, g_pre_ffn[None, :], g_post_ffn[None, :], wr, br, wg, wu, wd)


def kernel(x, c, positions, w_ada, b_ada, g_pre_mix, g_post_mix, g_pre_ffn, g_post_ffn, w_in, w_gla_a2,
           b_gla_a, g_gla_out, g_idx_k, w_proj_gla, w_proj_dsa, w_out, w_router_g, b_router_g,
           w_router_e, b_router_e, w_e_gate, w_e_up, w_e_down):
    batch, depth = x.shape[0], w_ada.shape[0]
    outs = []
    for bi in range(batch):
        xb = x[bi]
        for l in range(depth):
            xb = _layer(xb, c[bi:bi + 1], positions[bi], w_ada[l], b_ada[l], g_pre_mix[l], g_post_mix[l],
                        g_pre_ffn[l], g_post_ffn[l], w_in[l], w_gla_a2[l], b_gla_a[l], g_gla_out[l],
                        g_idx_k[l], w_proj_gla[l], w_proj_dsa[l], w_out[l], w_router_g[l],
                        b_router_g[l], w_router_e[l], b_router_e[l], w_e_gate[l], w_e_up[l],
                        w_e_down[l])
        outs.append(xb)
    return jnp.stack(outs, axis=0)
```

```python
import functools

import numpy as np
import jax
import jax.numpy as jnp
from jax import lax
from jax.experimental import pallas as pl
from jax.experimental.pallas import tpu as pltpu

F32 = jnp.float32
BF16 = jnp.bfloat16
I32 = jnp.int32

D_MODEL = 1024
EPS = 1e-6
ROPE_THETA = 500000.0
ROT_FRAC = 4
GLA_HEADS = 4
GLA_DK = 64
GLA_DV = 128
GLA_GATE_RANK = 16
GLA_TAU = 16.0
GLA_CHUNK = 64
DSA_HEADS = 8
DSA_HD = 64
IDX_HEADS = 8
IDX_DIM = 64
DSA_TOPK_MAX = 256
N_GROUPS = 4
EXPERTS_PER_GROUP = 8
N_EXPERTS = N_GROUPS * EXPERTS_PER_GROUP
D_EXPERT = D_MODEL // 4
N_MOD = 6

GLA_QK = GLA_HEADS * GLA_DK
GLA_V = GLA_HEADS * GLA_DV
DSA_W = DSA_HEADS * DSA_HD
IDX_Q = IDX_HEADS * IDX_DIM

LANES = 128
VMEM_LIMIT_BYTES = 56 * 1024 * 1024

PROJ_ROWS = 512
GLA_ROWS = 512
DSA_TQ = 128
DSA_CH = 512
MOE_ROWS = 512
MOE_EB = 4
MASKED = -1e30
LOG2E = float(np.log2(np.e))
GLA_LEVELS = 7


def _dot(a, b):
    return jnp.dot(a, b, preferred_element_type=F32)


def _dot_nt(a, b):
    return lax.dot_general(a, b, (((1,), (1,)), ((), ())), preferred_element_type=F32)


def _dot_tn(a, b):
    return lax.dot_general(a, b, (((0,), (0,)), ((), ())), preferred_element_type=F32)


def _sigmoid(x):
    return 1.0 / (1.0 + jnp.exp(-x))


def _rms(x, g):
    ms = jnp.mean(x * x, axis=-1, keepdims=True)
    return x * lax.rsqrt(ms + EPS) * g


def _modulated(x, g, mod_ref, shift_row, scale_row):
    return (_rms(x, g) * (1.0 + mod_ref[scale_row:scale_row + 1, :])
            + mod_ref[shift_row:shift_row + 1, :])


def _params(sem):
    return pltpu.CompilerParams(dimension_semantics=sem, vmem_limit_bytes=VMEM_LIMIT_BYTES)


def _full(shape):
    return pl.BlockSpec(shape, lambda *_: (0,) * len(shape))


def _resident(shape):
    return pl.BlockSpec(shape, lambda *_: (0,) * len(shape), pipeline_mode=pl.Buffered(1))


def _ada_kernel(c_ref, w_ref, b_ref, o_ref):
    c = c_ref[...]
    a = c * _sigmoid(c)
    o_ref[...] = _dot(a.astype(BF16), w_ref[...].astype(BF16)) + b_ref[...]


def _ada(c8, w, b):
    n = w.shape[1]
    bn = 1536
    return pl.pallas_call(
        _ada_kernel,
        grid=(n // bn,),
        in_specs=[_full(c8.shape),
                  pl.BlockSpec((D_MODEL, bn), lambda j: (0, j)),
                  pl.BlockSpec((1, bn), lambda j: (0, j))],
        out_specs=pl.BlockSpec((8, bn), lambda j: (0, j)),
        out_shape=jax.ShapeDtypeStruct((8, n), F32),
        compiler_params=_params(("arbitrary",)),
        name="ada",
    )(c8, w, b)


def _gla_proj_kernel(x_ref, mod_ref, g_ref, w_ref, wga_ref, wa2_ref, ba_ref,
                     q_ref, k_ref, v_ref, gg_ref, la_ref):
    h = _modulated(x_ref[...], g_ref[...], mod_ref, 0, 1)
    hb = h.astype(BF16)
    p = _dot(hb, w_ref[...])
    q_ref[...] = p[:, 0:GLA_QK] * (GLA_DK ** -0.5)
    k_ref[...] = p[:, GLA_QK:2 * GLA_QK]
    v_ref[...] = p[:, 2 * GLA_QK:2 * GLA_QK + GLA_V]
    gg_ref[...] = p[:, 2 * GLA_QK + GLA_V:]
    ga = _dot(hb, wga_ref[...])
    z = _dot(ga.astype(BF16), wa2_ref[...]) + ba_ref[...]
    log_sig = jnp.minimum(z, 0.0) - jnp.log1p(jnp.exp(-jnp.abs(z)))
    la_ref[...] = log_sig * (1.0 / GLA_TAU)


def _gla_proj(x, mod, g, w, wga, wa2, ba):
    s = x.shape[0]
    tm = PROJ_ROWS
    row = lambda n: pl.BlockSpec((tm, n), lambda i: (i, 0))
    return pl.pallas_call(
        _gla_proj_kernel,
        grid=(s // tm,),
        in_specs=[row(D_MODEL), _full(mod.shape), _full(g.shape), _full(w.shape),
                  _full(wga.shape), _full(wa2.shape), _full(ba.shape)],
        out_specs=[row(GLA_QK), row(GLA_QK), row(GLA_V), row(GLA_V), row(GLA_QK)],
        out_shape=[jax.ShapeDtypeStruct((s, GLA_QK), F32), jax.ShapeDtypeStruct((s, GLA_QK), F32),
                   jax.ShapeDtypeStruct((s, GLA_V), F32), jax.ShapeDtypeStruct((s, GLA_V), F32),
                   jax.ShapeDtypeStruct((s, GLA_QK), F32)],
        compiler_params=_params(("arbitrary",)),
        name="gla_proj",
    )(x, mod, g, w, wga, wa2, ba)


def _dsa_proj_kernel(x_ref, mod_ref, g_ref, pos_ref, fv_ref, sg_ref, w_ref, wsm_ref, gik_ref,
                     qb_ref, kb_ref, vb_ref, qi_ref, ki_ref, wi_ref):
    tm = x_ref.shape[0]
    h = _modulated(x_ref[...], g_ref[...], mod_ref, 0, 1)
    hb = h.astype(BF16)
    ang = pos_ref[...] * fv_ref[...]
    cs = jnp.cos(ang)
    sn = jnp.sin(ang) * sg_ref[...]
    lane = lax.broadcasted_iota(I32, (tm, LANES), 1)
    first = (lane & (DSA_HD - 1)) < (DSA_HD // ROT_FRAC // 2)

    def rope(t):
        width = t.shape[1]
        rep = width // LANES
        tile = (lambda a: jnp.concatenate([a] * rep, axis=1)) if rep > 1 else (lambda a: a)
        half = DSA_HD // ROT_FRAC // 2
        fwd = pltpu.roll(t, half, 1)
        bwd = pltpu.roll(t, width - half, 1)
        partner = jnp.where(tile(first), bwd, fwd)
        return t * tile(cs) + partner * tile(sn)

    p = _dot(hb, w_ref[...])
    qb_ref[...] = (rope(p[:, 0:DSA_W]) * (DSA_HD ** -0.5 * LOG2E)).astype(BF16)
    kb_ref[...] = rope(p[:, DSA_W:2 * DSA_W]).astype(BF16)
    vb_ref[...] = p[:, 2 * DSA_W:3 * DSA_W].astype(BF16)
    qi_ref[...] = (rope(p[:, 3 * DSA_W:]) * (IDX_DIM ** -0.5)).astype(BF16)

    sm = _dot(hb, wsm_ref[...])
    is_ik = lane < IDX_DIM
    mu = jnp.sum(jnp.where(is_ik, sm, 0.0), axis=-1, keepdims=True) * (1.0 / IDX_DIM)
    xc = jnp.where(is_ik, sm - mu, 0.0)
    var = jnp.sum(xc * xc, axis=-1, keepdims=True) * (1.0 / IDX_DIM)
    y = xc * lax.rsqrt(var + EPS) * gik_ref[...]
    ki_ref[...] = rope(y)[:, 0:IDX_DIM].astype(BF16)
    wi_ref[...] = sm * (IDX_HEADS ** -0.5)


def _dsa_proj(x, mod, g, pos, fv, sg, w, wsm, gik):
    s = x.shape[0]
    tm = PROJ_ROWS
    row = lambda n: pl.BlockSpec((tm, n), lambda i: (i, 0))
    return pl.pallas_call(
        _dsa_proj_kernel,
        grid=(s // tm,),
        in_specs=[row(D_MODEL), _full(mod.shape), _full(g.shape), row(1), _full(fv.shape),
                  _full(sg.shape), _full(w.shape), _full(wsm.shape), _full(gik.shape)],
        out_specs=[row(DSA_W), row(DSA_W), row(DSA_W), row(IDX_Q), row(IDX_DIM), row(LANES)],
        out_shape=[jax.ShapeDtypeStruct((s, DSA_W), BF16), jax.ShapeDtypeStruct((s, DSA_W), BF16),
                   jax.ShapeDtypeStruct((s, DSA_W), BF16), jax.ShapeDtypeStruct((s, IDX_Q), BF16),
                   jax.ShapeDtypeStruct((s, IDX_DIM), BF16), jax.ShapeDtypeStruct((s, LANES), F32)],
        compiler_params=_params(("arbitrary",)),
        name="dsa_proj",
    )(x, mod, g, pos, fv, sg, w, wsm, gik)


def _gla_constants():
    c = GLA_CHUNK
    tril = np.tril(np.ones((c, c), np.float32))
    t = np.arange(c)
    mats = [tril]
    masks = [(t[:, None] == t[None, :])]
    for hs in (32, 16, 8, 4, 2, 1):
        blk = 2 * hs
        r = (t // blk) * blk + hs - 1
        mats.append(tril - tril[r, :])
        same = (t[:, None] // blk) == (t[None, :] // blk)
        masks.append(same & ((t[:, None] % blk) >= hs) & ((t[None, :] % blk) < hs))
    m_all = np.concatenate(mats, axis=0)
    lvl = np.stack([np.tile(m.astype(np.float32), (1, GLA_HEADS)) for m in masks])
    hrow = np.arange(GLA_HEADS * c) // c
    wmask = (hrow[:, None] == (np.arange(GLA_QK) // GLA_DK)[None, :]).astype(np.float32)
    vmask = (hrow[:, None] == (np.arange(GLA_V) // GLA_DV)[None, :]).astype(np.float32)
    smask = ((np.arange(GLA_V) // GLA_DV)[:, None] == (np.arange(GLA_QK) // GLA_DK)[None, :])
    return (jnp.asarray(m_all, BF16), jnp.asarray(lvl), jnp.asarray(wmask), jnp.asarray(vmask),
            jnp.asarray(smask.astype(np.float32)))


def _gla_kernel(q_ref, k_ref, v_ref, gg_ref, la_ref, gout_ref, mall_ref, lvl_ref, wmask_ref,
                vmask_ref, smask_ref, o_ref, st_ref):
    c = GLA_CHUNK

    @pl.when(pl.program_id(0) == 0)
    def _():
        st_ref[...] = jnp.zeros_like(st_ref)

    m_all = mall_ref[...]
    wmask = wmask_ref[...]
    vmask = vmask_ref[...]
    smask = smask_ref[...]
    gout = gout_ref[...]

    def chunk(ci, carry):
        r0 = pl.multiple_of(ci * c, c)
        rows = pl.ds(r0, c)
        q = q_ref[rows, :]
        k = k_ref[rows, :]
        v = v_ref[rows, :]
        la = la_ref[rows, :]
        hi = la.astype(BF16)
        r1 = la - hi.astype(F32)
        mid = r1.astype(BF16)
        lo = (r1 - mid.astype(F32)).astype(BF16)
        dall = _dot(m_all, hi) + _dot(m_all, mid) + _dot(m_all, lo)
        b = dall[0:c]
        b_last = b[c - 1:c, :]
        qhat = q * jnp.exp(b)
        khat = k * jnp.exp(b_last - b)

        a = jnp.zeros((c, GLA_HEADS * c), F32)
        for lv in range(GLA_LEVELS):
            if lv == 0:
                qt, kt = q, k
            else:
                d = dall[lv * c:(lv + 1) * c]
                qt = q * jnp.exp(jnp.minimum(d, 0.0))
                kt = k * jnp.exp(jnp.minimum(-d, 0.0))
            w = (jnp.concatenate([kt] * GLA_HEADS, axis=0) * wmask).astype(BF16)
            a = a + lvl_ref[lv] * _dot_nt(qt.astype(BF16), w)

        st = st_ref[...]
        vbd = (jnp.concatenate([v] * GLA_HEADS, axis=0) * vmask).astype(BF16)
        o = _dot(a.astype(BF16), vbd) + _dot_nt(qhat.astype(BF16), st.astype(BF16))
        st_ref[...] = st * jnp.exp(b_last) + smask * _dot_tn(v.astype(BF16), khat.astype(BF16))

        parts = []
        for hh in range(GLA_HEADS):
            oh = o[:, hh * GLA_DV:(hh + 1) * GLA_DV]
            ms = jnp.mean(oh * oh, axis=-1, keepdims=True)
            parts.append(oh * lax.rsqrt(ms + EPS))
        gg = gg_ref[rows, :]
        o_ref[rows, :] = jnp.concatenate(parts, axis=1) * gout * (gg * _sigmoid(gg))
        return carry

    lax.fori_loop(0, q_ref.shape[0] // c, chunk, 0)


def _gla(q, k, v, gg, la, gout):
    s = q.shape[0]
    tb = GLA_ROWS
    consts = _gla_constants()
    row = lambda n: pl.BlockSpec((tb, n), lambda i: (i, 0))
    return pl.pallas_call(
        _gla_kernel,
        grid=(s // tb,),
        in_specs=[row(GLA_QK), row(GLA_QK), row(GLA_V), row(GLA_V), row(GLA_QK), _full(gout.shape)]
                 + [_full(a.shape) for a in consts],
        out_specs=row(GLA_V),
        out_shape=jax.ShapeDtypeStruct((s, GLA_V), F32),
        scratch_shapes=[pltpu.VMEM((GLA_V, GLA_QK), F32)],
        compiler_params=_params(("arbitrary",)),
        name="gla",
    )(q, k, v, gg, la, gout, *consts)


I16 = jnp.int16
KEY_NEG_INF_HI = -32641
KEY_NEG_INF_LO = 32767
DSA_SLAB = 32


def _float_to_key(x):
    bits = lax.bitcast_convert_type(x, I32)
    return bits ^ ((bits >> 31) & jnp.int32(0x7FFFFFFF))


def _dsa_kernel(qi_ref, kit_ref, wi_ref, qb_ref, kt_ref, v_ref, o_ref,
                hi_ref, lo_ref, wb_ref, qbd_ref, s_ref, p_ref, bias_ref, m_ref, l_ref, alpha_ref,
                acc_ref, *, seq, topk):
    tq = DSA_TQ
    ch = DSA_CH
    hg = 4
    gw = hg * DSA_HD
    i = pl.program_id(0)
    n_ch = ((i + 1) * tq + ch - 1) // ch
    qpos = lax.broadcasted_iota(I32, (tq, ch), 0) + i * tq
    lane_c = lax.broadcasted_iota(I32, (tq, ch), 1)

    def tile_l(a, width):
        return jnp.concatenate([a] * (width // a.shape[1]), axis=1)

    def bcast16(col):
        return tile_l(jnp.broadcast_to(col, (tq, LANES)).astype(I16), ch)

    def chunk(c):
        return pl.ds(pl.multiple_of(c * ch, ch), ch)

    wv = wi_ref[...]
    for h in range(IDX_HEADS):
        wb_ref[h] = jnp.broadcast_to(wv[:, IDX_DIM + h:IDX_DIM + h + 1], (tq, LANES))
    qi = qi_ref[0]

    def score_body(c, carry):
        lg = _dot(qi, kit_ref[:, chunk(c)])
        sc = jnp.maximum(lg[0:tq], 0.0) * tile_l(wb_ref[0], ch)
        for h in range(1, IDX_HEADS):
            sc = sc + jnp.maximum(lg[h * tq:(h + 1) * tq], 0.0) * tile_l(wb_ref[h], ch)
        sc = jnp.where(sc == 0.0, 0.0, sc)
        key = _float_to_key(jnp.where(lane_c + c * ch <= qpos, sc, -jnp.inf))
        hi_ref[:, chunk(c)] = (key >> 16).astype(I16)
        lo_ref[:, chunk(c)] = ((key & 0xFFFF) - 32768).astype(I16)
        return carry

    lax.fori_loop(0, n_ch, score_body, 0)

    def count(pred):
        def body(c, acc):
            one = jnp.where(pred(c), jnp.int16(1), jnp.int16(0))
            for j in range(ch // LANES):
                acc = acc + one[:, j * LANES:(j + 1) * LANES]
            return acc
        acc = lax.fori_loop(0, n_ch, body, jnp.zeros((tq, LANES), I16))
        return jnp.sum(acc.astype(F32), axis=1, keepdims=True)

    def kth_largest_16(ref, kth):
        def bit_body(p, u):
            cand_u = u | jnp.left_shift(jnp.int32(1), 15 - p)
            cand = bcast16(cand_u - 32768)
            cnt = count(lambda c: ref[:, chunk(c)] >= cand)
            return jnp.where(cnt >= kth, cand_u, u)
        return lax.fori_loop(0, 16, bit_body, jnp.zeros((tq, 1), I32)) - 32768

    kf = float(topk)
    t_hi = kth_largest_16(hi_ref, kf)
    t_hi16 = bcast16(t_hi)
    cnt_gt_hi = count(lambda c: hi_ref[:, chunk(c)] > t_hi16)

    def low_body(c, carry):
        lo_ref[:, chunk(c)] = jnp.where(hi_ref[:, chunk(c)] == t_hi16, lo_ref[:, chunk(c)],
                                        jnp.int16(-32768))
        return carry

    lax.fori_loop(0, n_ch, low_body, 0)
    t_lo = kth_largest_16(lo_ref, kf - cnt_gt_hi)
    t_lo16 = bcast16(t_lo)

    def is_tie(c):
        return (hi_ref[:, chunk(c)] == t_hi16) & (lo_ref[:, chunk(c)] == t_lo16)

    cnt_gt = cnt_gt_hi + count(lambda c: lo_ref[:, chunk(c)] > t_lo16)
    need = kf - cnt_gt
    below_all = (t_hi == KEY_NEG_INF_HI) & (t_lo == KEY_NEG_INF_LO)
    tie_row = (count(is_tie) > need) & jnp.logical_not(below_all)
    row_pos = lax.broadcasted_iota(I32, (tq, 1), 0) + i * tq
    j_default = jnp.where(below_all, row_pos, jnp.int32(seq - 1))

    def idx16(c):
        return (lane_c + c * ch).astype(I16)

    def tie_break(_):
        nbits = max(1, int(np.ceil(np.log2(seq))))

        def jbit(p, m):
            cand = m | jnp.left_shift(jnp.int32(1), nbits - 1 - p)
            cand16 = bcast16(cand)
            cnt = count(lambda c: is_tie(c) & (idx16(c) < cand16))
            return jnp.where(cnt < need, cand, m)

        m = lax.fori_loop(0, nbits, jbit, jnp.zeros((tq, 1), I32))
        return jnp.where(tie_row, m, j_default)

    any_tie = jnp.max(jnp.where(tie_row, 1.0, 0.0)) > 0.0
    j_last16 = bcast16(lax.cond(any_tie, tie_break, lambda _: j_default, 0))

    def bias_body(c, carry):
        h = hi_ref[:, chunk(c)]
        z = lo_ref[:, chunk(c)]
        sel = (h > t_hi16) | (z > t_lo16) | ((h == t_hi16) & (z == t_lo16) & (idx16(c) <= j_last16))
        bias = jnp.where(sel, jnp.zeros((tq, ch), BF16), jnp.full((tq, ch), MASKED, BF16))
        hi_ref[:, chunk(c)] = pltpu.bitcast(bias, I16)
        return carry

    lax.fori_loop(0, n_ch, bias_body, 0)

    qt = qb_ref[...]
    head_of_lane = lax.broadcasted_iota(I32, (tq, gw), 1) >> 6
    for g in range(2):
        qg = qt[:, g * gw:(g + 1) * gw]
        for h in range(hg):
            qbd_ref[g, h * tq:(h + 1) * tq, :] = jnp.where(head_of_lane == h, qg, jnp.zeros_like(qg))
    m_ref[...] = jnp.full(m_ref.shape, MASKED, F32)
    l_ref[...] = jnp.zeros(l_ref.shape, F32)
    acc_ref[...] = jnp.zeros(acc_ref.shape, F32)

    def attn_body(c, carry):
        bias_ref[...] = pltpu.bitcast(hi_ref[:, chunk(c)], BF16).astype(F32)
        for g in range(2):
            s_ref[g] = _dot(qbd_ref[g], kt_ref[g * gw:(g + 1) * gw, chunk(c)])
        for g in range(2):
            for r in range(hg * tq // DSA_SLAB):
                rows = slice(r * DSA_SLAB, (r + 1) * DSA_SLAB)
                b0 = (r * DSA_SLAB) % tq
                s = s_ref[g, rows, :] + bias_ref[b0:b0 + DSA_SLAB, :]
                m_prev = m_ref[g, rows, :]
                m_new = jnp.maximum(m_prev, jnp.max(s, axis=1, keepdims=True))
                alpha = jnp.exp2(m_prev - m_new)
                p = jnp.exp2(s - tile_l(m_new, ch))
                l_ref[g, rows, :] = alpha * l_ref[g, rows, :] + jnp.sum(p, axis=1, keepdims=True)
                m_ref[g, rows, :] = m_new
                alpha_ref[g, rows, :] = alpha
                p_ref[g, rows, :] = p.astype(BF16)
            acc_ref[g] = (acc_ref[g] * tile_l(alpha_ref[g], gw)
                          + _dot(p_ref[g], v_ref[chunk(c), g * gw:(g + 1) * gw]))
        return carry

    lax.fori_loop(0, n_ch, attn_body, 0)

    outs = []
    for g in range(2):
        a = acc_ref[g] * tile_l(1.0 / l_ref[g], gw)
        og = jnp.zeros((tq, gw), F32)
        for h in range(hg):
            og = og + jnp.where(head_of_lane == h, a[h * tq:(h + 1) * tq], 0.0)
        outs.append(og)
    o_ref[...] = jnp.concatenate(outs, axis=1)


def _dsa(qi_r, kit, wi, qb, kt, v, topk):
    s = qb.shape[0]
    tq = DSA_TQ
    assert s % DSA_CH == 0 and s <= 32768 and s // DSA_CH <= 32
    rows4 = 4 * tq
    return pl.pallas_call(
        functools.partial(_dsa_kernel, seq=s, topk=topk),
        grid=(s // tq,),
        in_specs=[pl.BlockSpec((1, IDX_HEADS * tq, IDX_DIM), lambda i: (i, 0, 0)),
                  _resident(kit.shape),
                  pl.BlockSpec((tq, LANES), lambda i: (i, 0)),
                  pl.BlockSpec((tq, DSA_W), lambda i: (i, 0)),
                  _resident(kt.shape),
                  _resident(v.shape)],
        out_specs=pl.BlockSpec((tq, DSA_W), lambda i: (i, 0)),
        out_shape=jax.ShapeDtypeStruct((s, DSA_W), F32),
        scratch_shapes=[pltpu.VMEM((tq, s), I16),
                        pltpu.VMEM((tq, s), I16),
                        pltpu.VMEM((IDX_HEADS, tq, LANES), F32),
                        pltpu.VMEM((2, rows4, 4 * DSA_HD), BF16),
                        pltpu.VMEM((2, rows4, DSA_CH), F32),
                        pltpu.VMEM((2, rows4, DSA_CH), BF16),
                        pltpu.VMEM((tq, DSA_CH), F32),
                        pltpu.VMEM((2, rows4, LANES), F32),
                        pltpu.VMEM((2, rows4, LANES), F32),
                        pltpu.VMEM((2, rows4, LANES), F32),
                        pltpu.VMEM((2, rows4, 4 * DSA_HD), F32)],
        compiler_params=_params(("arbitrary",)),
        name="dsa",
    )(qi_r, kit, wi, qb, kt, v)


def _merge_kernel(x_ref, mod_ref, gpre_ref, gpost_ref, oa_ref, ob_ref, wbg_ref, wpg_ref, wpd_ref,
                  wout_ref, o_ref):
    x = x_ref[...]
    h = _modulated(x, gpre_ref[...], mod_ref, 0, 1)
    gates = _sigmoid(_dot(h.astype(BF16), wbg_ref[...]))
    yg = _dot(oa_ref[...].astype(BF16), wpg_ref[...])
    yd = _dot(ob_ref[...].astype(BF16), wpd_ref[...])
    mix = gates[:, 0:D_MODEL] * yg + gates[:, D_MODEL:] * yd
    out = _dot(mix.astype(BF16), wout_ref[...])
    o_ref[...] = x + mod_ref[2:3, :] * _rms(out, gpost_ref[...])


def _merge(x, mod, gpre, gpost, oa, ob, wbg, wpg, wpd, wout):
    s = x.shape[0]
    tm = PROJ_ROWS
    row = lambda n: pl.BlockSpec((tm, n), lambda i: (i, 0))
    return pl.pallas_call(
        _merge_kernel,
        grid=(s // tm,),
        in_specs=[row(D_MODEL), _full(mod.shape), _full(gpre.shape), _full(gpost.shape),
                  row(GLA_V), row(DSA_W), _full(wbg.shape), _full(wpg.shape), _full(wpd.shape),
                  _full(wout.shape)],
        out_specs=row(D_MODEL),
        out_shape=jax.ShapeDtypeStruct((s, D_MODEL), F32),
        compiler_params=_params(("arbitrary",)),
        name="merge",
    )(x, mod, gpre, gpost, oa, ob, wbg, wpg, wpd, wout)


def _router(lg):
    t = lg.shape[0]
    lane = lax.broadcasted_iota(I32, (t, LANES), 1)
    lanef = lane.astype(F32)
    big = 1e9
    gm = lane < N_GROUPS
    gmax = jnp.max(jnp.where(gm, lg, -jnp.inf), axis=1, keepdims=True)
    gsum = jnp.sum(jnp.where(gm, jnp.exp(lg - gmax), 0.0), axis=1, keepdims=True)
    p_g = 1.0 / gsum
    g_sel = jnp.min(jnp.where(gm & (lg == gmax), lanef, big), axis=1, keepdims=True)
    lo = N_GROUPS + EXPERTS_PER_GROUP * g_sel
    em = (lanef >= lo) & (lanef < lo + EXPERTS_PER_GROUP)
    m1 = jnp.max(jnp.where(em, lg, -jnp.inf), axis=1, keepdims=True)
    i1 = jnp.min(jnp.where(em & (lg == m1), lanef, big), axis=1, keepdims=True)
    em2 = em & (lanef != i1)
    m2 = jnp.max(jnp.where(em2, lg, -jnp.inf), axis=1, keepdims=True)
    i2 = jnp.min(jnp.where(em2 & (lg == m2), lanef, big), axis=1, keepdims=True)
    e2 = jnp.exp(m2 - m1)
    inv = 1.0 / (1.0 + e2)
    return (jnp.where(lanef == i1, p_g * inv, 0.0) + jnp.where(lanef == i2, p_g * (e2 * inv), 0.0))


def _moe_kernel(x_ref, mod_ref, gpre_ref, gpost_ref, wr_ref, br_ref, wg_ref, wu_ref, wd_ref,
                o_ref, hb_ref, comb_ref, acc_ref):
    j = pl.program_id(1)
    tm = x_ref.shape[0]

    @pl.when(j == 0)
    def _():
        h = _modulated(x_ref[...], gpre_ref[...], mod_ref, 3, 4)
        hb = h.astype(BF16)
        hb_ref[...] = hb
        comb_ref[...] = _router(_dot(hb, wr_ref[...]) + br_ref[...])
        acc_ref[...] = jnp.zeros_like(acc_ref)

    hb = hb_ref[...]
    hgate = _dot(hb, wg_ref[...])
    hup = _dot(hb, wu_ref[...])
    act = hgate * _sigmoid(hgate) * hup
    comb = comb_ref[...]
    lane = lax.broadcasted_iota(I32, (tm, LANES), 1)
    parts = []
    for e in range(MOE_EB):
        sel = lane == (N_GROUPS + j * MOE_EB + e)
        cw = jnp.sum(jnp.where(sel, comb, 0.0), axis=1, keepdims=True)
        parts.append((act[:, e * D_EXPERT:(e + 1) * D_EXPERT] * cw).astype(BF16))
    acc_ref[...] += _dot(jnp.concatenate(parts, axis=1), wd_ref[...])

    @pl.when(j == pl.num_programs(1) - 1)
    def _():
        o_ref[...] = x_ref[...] + mod_ref[5:6, :] * _rms(acc_ref[...], gpost_ref[...])


def _moe(x, mod, gpre, gpost, wr, br, wg, wu, wd):
    s = x.shape[0]
    tm = MOE_ROWS
    bw = MOE_EB * D_EXPERT
    return pl.pallas_call(
        _moe_kernel,
        grid=(s // tm, N_EXPERTS // MOE_EB),
        in_specs=[pl.BlockSpec((tm, D_MODEL), lambda i, j: (i, 0)),
                  _full(mod.shape), _full(gpre.shape), _full(gpost.shape), _full(wr.shape),
                  _full(br.shape),
                  pl.BlockSpec((D_MODEL, bw), lambda i, j: (0, j)),
                  pl.BlockSpec((D_MODEL, bw), lambda i, j: (0, j)),
                  pl.BlockSpec((bw, D_MODEL), lambda i, j: (j, 0))],
        out_specs=pl.BlockSpec((tm, D_MODEL), lambda i, j: (i, 0)),
        out_shape=jax.ShapeDtypeStruct((s, D_MODEL), F32),
        scratch_shapes=[pltpu.VMEM((tm, D_MODEL), BF16), pltpu.VMEM((tm, LANES), F32),
                        pltpu.VMEM((tm, D_MODEL), F32)],
        compiler_params=_params(("arbitrary", "arbitrary")),
        name="moe",
    )(x, mod, gpre, gpost, wr, br, wg, wu, wd)


def _rope_lane_constants():
    rot = DSA_HD // ROT_FRAC
    half = rot // 2
    freqs = np.power(np.float32(ROPE_THETA), -np.arange(half, dtype=np.float32) * np.float32(2.0) / rot)
    j = np.arange(LANES) % DSA_HD
    fv = np.where(j < rot, freqs[j % half], 0.0).astype(np.float32)
    sg = np.where(j < half, -1.0, np.where(j < rot, 1.0, 0.0)).astype(np.float32)
    return jnp.asarray(fv)[None, :], jnp.asarray(sg)[None, :]


def _pad_cols(w, n):
    return jnp.pad(w, ((0, 0), (0, n - w.shape[1])))


def _layer(x, c, pos, w_ada, b_ada, g_pre_mix, g_post_mix, g_pre_ffn, g_post_ffn, w_in, w_gla_a2,
           b_gla_a, g_gla_out, g_idx_k, w_proj_gla, w_proj_dsa, w_out, w_router_g, b_router_g,
           w_router_e, b_router_e, w_e_gate, w_e_up, w_e_down):
    s = x.shape[0]
    mod = _ada(jnp.broadcast_to(c, (8, D_MODEL)), w_ada, b_ada[None, :])[0].reshape(N_MOD, D_MODEL)

    o = np.cumsum((GLA_QK, GLA_QK, GLA_V, GLA_V, GLA_GATE_RANK, DSA_W, DSA_W, DSA_W, IDX_Q, IDX_DIM,
                   IDX_HEADS, 2 * D_MODEL))
    wb = w_in.astype(BF16)
    w_gla = wb[:, 0:o[3]]
    w_ga = _pad_cols(wb[:, o[3]:o[4]], LANES)
    w_dsa = wb[:, o[4]:o[8]]
    w_sm = _pad_cols(jnp.concatenate([wb[:, o[8]:o[9]], wb[:, o[9]:o[10]]], axis=1), LANES)
    w_bg = wb[:, o[10]:o[11]]
    w_a2 = jnp.pad(w_gla_a2.astype(BF16), ((0, LANES - GLA_GATE_RANK), (0, 0)))

    q_a, k_a, v_a, gg, la = _gla_proj(x, mod, g_pre_mix[None, :], w_gla, w_ga, w_a2, b_gla_a[None, :])
    o_a = _gla(q_a, k_a, v_a, gg, la, jnp.tile(g_gla_out, GLA_HEADS)[None, :])

    fv, sg = _rope_lane_constants()
    gik = jnp.pad(g_idx_k, (0, LANES - IDX_DIM))[None, :]
    q_b, k_b, v_b, qi, ki, wi = _dsa_proj(x, mod, g_pre_mix[None, :], pos.astype(F32)[:, None], fv, sg,
                                          w_dsa, w_sm, gik)
    nqb = s // DSA_TQ
    qi_r = qi.reshape(nqb, DSA_TQ, IDX_HEADS, IDX_DIM).transpose(0, 2, 1, 3).reshape(
        nqb, IDX_HEADS * DSA_TQ, IDX_DIM)
    o_b = _dsa(qi_r, ki.T, wi, q_b, k_b.T, v_b, min(DSA_TOPK_MAX, s // 4))

    x1 = _merge(x, mod, g_pre_mix[None, :], g_post_mix[None, :], o_a, o_b, w_bg,
                w_proj_gla.astype(BF16), w_proj_dsa.astype(BF16), w_out.astype(BF16))

    wr = _pad_cols(jnp.concatenate([w_router_g, w_router_e], axis=1).astype(BF16), LANES)
    br = jnp.pad(jnp.concatenate([b_router_g, b_router_e]), (0, LANES - N_GROUPS - N_EXPERTS))[None, :]
    wg = w_e_gate.astype(BF16).transpose(1, 0, 2).reshape(D_MODEL, N_EXPERTS * D_EXPERT)
    wu = w_e_up.astype(BF16).transpose(1, 0, 2).reshape(D_MODEL, N_EXPERTS * D_EXPERT)
    wd = w_e_down.astype(BF16).reshape(N_EXPERTS * D_EXPERT, D_MODEL)
    return _moe(x1, mod, g_pre_ffn[None, :], g_post_ffn[None, :], wr, br, wg, wu, wd)


def kernel(x, c, positions, w_ada, b_ada, g_pre_mix, g_post_mix, g_pre_ffn, g_post_ffn, w_in, w_gla_a2,
           b_gla_a, g_gla_out, g_idx_k, w_proj_gla, w_proj_dsa, w_out, w_router_g, b_router_g,
           w_router_e, b_router_e, w_e_gate, w_e_up, w_e_down):
    batch, depth = x.shape[0], w_ada.shape[0]
    outs = []
    for bi in range(batch):
        xb = x[bi]
        for l in range(depth):
            xb = _layer(xb, c[bi:bi + 1], positions[bi], w_ada[l], b_ada[l], g_pre_mix[l], g_post_mix[l],
                        g_pre_ffn[l], g_post_ffn[l], w_in[l], w_gla_a2[l], b_gla_a[l], g_gla_out[l],
                        g_idx_k[l], w_proj_gla[l], w_proj_dsa[l], w_out[l], w_router_g[l],
                        b_router_g[l], w_router_e[l], b_router_e[l], w_e_gate[l], w_e_up[l],
                        w_e_down[l])
        outs.append(xb)
    return jnp.stack(outs, axis=0)
```

```python
import functools

import numpy as np
import jax
import jax.numpy as jnp
from jax import lax
from jax.experimental import pallas as pl
from jax.experimental.pallas import tpu as pltpu

F32 = jnp.float32
BF16 = jnp.bfloat16
I32 = jnp.int32

D_MODEL = 1024
EPS = 1e-6
ROPE_THETA = 500000.0
ROT_FRAC = 4
GLA_HEADS = 4
GLA_DK = 64
GLA_DV = 128
GLA_GATE_RANK = 16
GLA_TAU = 16.0
GLA_CHUNK = 64
DSA_HEADS = 8
DSA_HD = 64
IDX_HEADS = 8
IDX_DIM = 64
DSA_TOPK_MAX = 256
N_GROUPS = 4
EXPERTS_PER_GROUP = 8
N_EXPERTS = N_GROUPS * EXPERTS_PER_GROUP
D_EXPERT = D_MODEL // 4
N_MOD = 6

GLA_QK = GLA_HEADS * GLA_DK
GLA_V = GLA_HEADS * GLA_DV
DSA_W = DSA_HEADS * DSA_HD
IDX_Q = IDX_HEADS * IDX_DIM

LANES = 128
VMEM_LIMIT_BYTES = 56 * 1024 * 1024

PROJ_ROWS = 512
GLA_ROWS = 512
DSA_TQ = 128
DSA_CH = 512
MOE_ROWS = 512
MOE_EB = 4
MASKED = -1e30
LOG2E = float(np.log2(np.e))
GLA_LEVELS = 7


def _dot(a, b):
    return jnp.dot(a, b, preferred_element_type=F32)


def _dot_nt(a, b):
    return lax.dot_general(a, b, (((1,), (1,)), ((), ())), preferred_element_type=F32)


def _dot_tn(a, b):
    return lax.dot_general(a, b, (((0,), (0,)), ((), ())), preferred_element_type=F32)


def _sigmoid(x):
    return 1.0 / (1.0 + jnp.exp(-x))


def _rms(x, g):
    ms = jnp.mean(x * x, axis=-1, keepdims=True)
    return x * lax.rsqrt(ms + EPS) * g


def _modulated(x, g, mod_ref, shift_row, scale_row):
    return (_rms(x, g) * (1.0 + mod_ref[scale_row:scale_row + 1, :])
            + mod_ref[shift_row:shift_row + 1, :])


def _params(sem):
    return pltpu.CompilerParams(dimension_semantics=sem, vmem_limit_bytes=VMEM_LIMIT_BYTES)


def _full(shape):
    return pl.BlockSpec(shape, lambda *_: (0,) * len(shape))


def _resident(shape):
    return pl.BlockSpec(shape, lambda *_: (0,) * len(shape), pipeline_mode=pl.Buffered(1))


def _ada_kernel(c_ref, w_ref, b_ref, o_ref):
    c = c_ref[...]
    a = c * _sigmoid(c)
    o_ref[...] = _dot(a.astype(BF16), w_ref[...].astype(BF16)) + b_ref[...]


def _ada(c8, w, b):
    n = w.shape[1]
    bn = 1536
    return pl.pallas_call(
        _ada_kernel,
        grid=(n // bn,),
        in_specs=[_full(c8.shape),
                  pl.BlockSpec((D_MODEL, bn), lambda j: (0, j)),
                  pl.BlockSpec((1, bn), lambda j: (0, j))],
        out_specs=pl.BlockSpec((8, bn), lambda j: (0, j)),
        out_shape=jax.ShapeDtypeStruct((8, n), F32),
        compiler_params=_params(("arbitrary",)),
        name="ada",
    )(c8, w, b)


def _gla_proj_kernel(x_ref, mod_ref, g_ref, w_ref, wga_ref, wa2_ref, ba_ref,
                     q_ref, k_ref, v_ref, gg_ref, la_ref):
    h = _modulated(x_ref[...], g_ref[...], mod_ref, 0, 1)
    hb = h.astype(BF16)
    p = _dot(hb, w_ref[...])
    q_ref[...] = p[:, 0:GLA_QK] * (GLA_DK ** -0.5)
    k_ref[...] = p[:, GLA_QK:2 * GLA_QK]
    v_ref[...] = p[:, 2 * GLA_QK:2 * GLA_QK + GLA_V]
    gg_ref[...] = p[:, 2 * GLA_QK + GLA_V:]
    ga = _dot(hb, wga_ref[...])
    z = _dot(ga.astype(BF16), wa2_ref[...]) + ba_ref[...]
    log_sig = jnp.minimum(z, 0.0) - jnp.log1p(jnp.exp(-jnp.abs(z)))
    la_ref[...] = log_sig * (1.0 / GLA_TAU)


def _gla_proj(x, mod, g, w, wga, wa2, ba):
    s = x.shape[0]
    tm = PROJ_ROWS
    row = lambda n: pl.BlockSpec((tm, n), lambda i: (i, 0))
    return pl.pallas_call(
        _gla_proj_kernel,
        grid=(s // tm,),
        in_specs=[row(D_MODEL), _full(mod.shape), _full(g.shape), _full(w.shape),
                  _full(wga.shape), _full(wa2.shape), _full(ba.shape)],
        out_specs=[row(GLA_QK), row(GLA_QK), row(GLA_V), row(GLA_V), row(GLA_QK)],
        out_shape=[jax.ShapeDtypeStruct((s, GLA_QK), F32), jax.ShapeDtypeStruct((s, GLA_QK), F32),
                   jax.ShapeDtypeStruct((s, GLA_V), F32), jax.ShapeDtypeStruct((s, GLA_V), F32),
                   jax.ShapeDtypeStruct((s, GLA_QK), F32)],
        compiler_params=_params(("arbitrary",)),
        name="gla_proj",
    )(x, mod, g, w, wga, wa2, ba)


def _dsa_proj_kernel(x_ref, mod_ref, g_ref, pos_ref, fv_ref, sg_ref, w_ref, wsm_ref, gik_ref,
                     qb_ref, kb_ref, vb_ref, qi_ref, ki_ref, wi_ref):
    tm = x_ref.shape[0]
    h = _modulated(x_ref[...], g_ref[...], mod_ref, 0, 1)
    hb = h.astype(BF16)
    ang = pos_ref[...] * fv_ref[...]
    cs = jnp.cos(ang)
    sn = jnp.sin(ang) * sg_ref[...]
    lane = lax.broadcasted_iota(I32, (tm, LANES), 1)
    first = (lane & (DSA_HD - 1)) < (DSA_HD // ROT_FRAC // 2)

    def rope(t):
        width = t.shape[1]
        rep = width // LANES
        tile = (lambda a: jnp.concatenate([a] * rep, axis=1)) if rep > 1 else (lambda a: a)
        half = DSA_HD // ROT_FRAC // 2
        fwd = pltpu.roll(t, half, 1)
        bwd = pltpu.roll(t, width - half, 1)
        partner = jnp.where(tile(first), bwd, fwd)
        return t * tile(cs) + partner * tile(sn)

    p = _dot(hb, w_ref[...])
    qb_ref[...] = (rope(p[:, 0:DSA_W]) * (DSA_HD ** -0.5 * LOG2E)).astype(BF16)
    kb_ref[...] = rope(p[:, DSA_W:2 * DSA_W]).astype(BF16)
    vb_ref[...] = p[:, 2 * DSA_W:3 * DSA_W].astype(BF16)
    qi_ref[...] = (rope(p[:, 3 * DSA_W:]) * (IDX_DIM ** -0.5)).astype(BF16)

    sm = _dot(hb, wsm_ref[...])
    is_ik = lane < IDX_DIM
    mu = jnp.sum(jnp.where(is_ik, sm, 0.0), axis=-1, keepdims=True) * (1.0 / IDX_DIM)
    xc = jnp.where(is_ik, sm - mu, 0.0)
    var = jnp.sum(xc * xc, axis=-1, keepdims=True) * (1.0 / IDX_DIM)
    y = xc * lax.rsqrt(var + EPS) * gik_ref[...]
    ki_ref[...] = rope(y)[:, 0:IDX_DIM].astype(BF16)
    wi_ref[...] = sm * (IDX_HEADS ** -0.5)


def _dsa_proj(x, mod, g, pos, fv, sg, w, wsm, gik):
    s = x.shape[0]
    tm = PROJ_ROWS
    row = lambda n: pl.BlockSpec((tm, n), lambda i: (i, 0))
    return pl.pallas_call(
        _dsa_proj_kernel,
        grid=(s // tm,),
        in_specs=[row(D_MODEL), _full(mod.shape), _full(g.shape), row(1), _full(fv.shape),
                  _full(sg.shape), _full(w.shape), _full(wsm.shape), _full(gik.shape)],
        out_specs=[row(DSA_W), row(DSA_W), row(DSA_W), row(IDX_Q), row(IDX_DIM), row(LANES)],
        out_shape=[jax.ShapeDtypeStruct((s, DSA_W), BF16), jax.ShapeDtypeStruct((s, DSA_W), BF16),
                   jax.ShapeDtypeStruct((s, DSA_W), BF16), jax.ShapeDtypeStruct((s, IDX_Q), BF16),
                   jax.ShapeDtypeStruct((s, IDX_DIM), BF16), jax.ShapeDtypeStruct((s, LANES), F32)],
        compiler_params=_params(("arbitrary",)),
        name="dsa_proj",
    )(x, mod, g, pos, fv, sg, w, wsm, gik)


def _gla_constants():
    c = GLA_CHUNK
    tril = np.tril(np.ones((c, c), np.float32))
    t = np.arange(c)
    mats = [tril]
    masks = [(t[:, None] == t[None, :])]
    for hs in (32, 16, 8, 4, 2, 1):
        blk = 2 * hs
        r = (t // blk) * blk + hs - 1
        mats.append(tril - tril[r, :])
        same = (t[:, None] // blk) == (t[None, :] // blk)
        masks.append(same & ((t[:, None] % blk) >= hs) & ((t[None, :] % blk) < hs))
    m_all = np.concatenate(mats, axis=0)
    lvl = np.stack([np.tile(m.astype(np.float32), (1, GLA_HEADS)) for m in masks])
    hrow = np.arange(GLA_HEADS * c) // c
    wmask = (hrow[:, None] == (np.arange(GLA_QK) // GLA_DK)[None, :]).astype(np.float32)
    vmask = (hrow[:, None] == (np.arange(GLA_V) // GLA_DV)[None, :]).astype(np.float32)
    smask = ((np.arange(GLA_V) // GLA_DV)[:, None] == (np.arange(GLA_QK) // GLA_DK)[None, :])
    return (jnp.asarray(m_all, BF16), jnp.asarray(lvl), jnp.asarray(wmask), jnp.asarray(vmask),
            jnp.asarray(smask.astype(np.float32)))


def _gla_kernel(q_ref, k_ref, v_ref, gg_ref, la_ref, gout_ref, mall_ref, lvl_ref, wmask_ref,
                vmask_ref, smask_ref, o_ref, st_ref):
    c = GLA_CHUNK

    @pl.when(pl.program_id(0) == 0)
    def _():
        st_ref[...] = jnp.zeros_like(st_ref)

    m_all = mall_ref[...]
    wmask = wmask_ref[...]
    vmask = vmask_ref[...]
    smask = smask_ref[...]
    gout = gout_ref[...]

    def chunk(ci, carry):
        r0 = pl.multiple_of(ci * c, c)
        rows = pl.ds(r0, c)
        q = q_ref[rows, :]
        k = k_ref[rows, :]
        v = v_ref[rows, :]
        la = la_ref[rows, :]
        hi = la.astype(BF16)
        r1 = la - hi.astype(F32)
        mid = r1.astype(BF16)
        lo = (r1 - mid.astype(F32)).astype(BF16)
        dall = _dot(m_all, hi) + _dot(m_all, mid) + _dot(m_all, lo)
        b = dall[0:c]
        b_last = b[c - 1:c, :]
        qhat = q * jnp.exp(b)
        khat = k * jnp.exp(b_last - b)

        a = jnp.zeros((c, GLA_HEADS * c), F32)
        for lv in range(GLA_LEVELS):
            if lv == 0:
                qt, kt = q, k
            else:
                d = dall[lv * c:(lv + 1) * c]
                qt = q * jnp.exp(jnp.minimum(d, 0.0))
                kt = k * jnp.exp(jnp.minimum(-d, 0.0))
            w = (jnp.concatenate([kt] * GLA_HEADS, axis=0) * wmask).astype(BF16)
            a = a + lvl_ref[lv] * _dot_nt(qt.astype(BF16), w)

        st = st_ref[...]
        vbd = (jnp.concatenate([v] * GLA_HEADS, axis=0) * vmask).astype(BF16)
        o = _dot(a.astype(BF16), vbd) + _dot_nt(qhat.astype(BF16), st.astype(BF16))
        st_ref[...] = st * jnp.exp(b_last) + smask * _dot_tn(v.astype(BF16), khat.astype(BF16))

        parts = []
        for hh in range(GLA_HEADS):
            oh = o[:, hh * GLA_DV:(hh + 1) * GLA_DV]
            ms = jnp.mean(oh * oh, axis=-1, keepdims=True)
            parts.append(oh * lax.rsqrt(ms + EPS))
        gg = gg_ref[rows, :]
        o_ref[rows, :] = jnp.concatenate(parts, axis=1) * gout * (gg * _sigmoid(gg))
        return carry

    lax.fori_loop(0, q_ref.shape[0] // c, chunk, 0)


def _gla(q, k, v, gg, la, gout):
    s = q.shape[0]
    tb = GLA_ROWS
    consts = _gla_constants()
    row = lambda n: pl.BlockSpec((tb, n), lambda i: (i, 0))
    return pl.pallas_call(
        _gla_kernel,
        grid=(s // tb,),
        in_specs=[row(GLA_QK), row(GLA_QK), row(GLA_V), row(GLA_V), row(GLA_QK), _full(gout.shape)]
                 + [_full(a.shape) for a in consts],
        out_specs=row(GLA_V),
        out_shape=jax.ShapeDtypeStruct((s, GLA_V), F32),
        scratch_shapes=[pltpu.VMEM((GLA_V, GLA_QK), F32)],
        compiler_params=_params(("arbitrary",)),
        name="gla",
    )(q, k, v, gg, la, gout, *consts)


I16 = jnp.int16
KEY_NEG_INF_HI = -32641
KEY_NEG_INF_LO = 32767
DSA_SLAB = 32
DSA_SWEEP = 4


def _float_to_key(x):
    bits = lax.bitcast_convert_type(x, I32)
    return bits ^ ((bits >> 31) & jnp.int32(0x7FFFFFFF))


def _dsa_kernel(qi_ref, kit_ref, wi_ref, qb_ref, kt_ref, v_ref, o_ref,
                hi_ref, lo_ref, wb_ref, qbd_ref, s_ref, p_ref, bias_ref, m_ref, l_ref, alpha_ref,
                acc_ref, *, seq, topk):
    tq = DSA_TQ
    ch = DSA_CH
    hg = 4
    gw = hg * DSA_HD
    i = pl.program_id(0)
    n_ch = ((i + 1) * tq + ch - 1) // ch
    qpos = lax.broadcasted_iota(I32, (tq, ch), 0) + i * tq
    lane_c = lax.broadcasted_iota(I32, (tq, ch), 1)

    def tile_l(a, width):
        return jnp.concatenate([a] * (width // a.shape[1]), axis=1)

    def bcast16(col):
        return tile_l(jnp.broadcast_to(col, (tq, LANES)).astype(I16), ch)

    def chunk(c):
        return pl.ds(pl.multiple_of(c * ch, ch), ch)

    wv = wi_ref[...]
    for h in range(IDX_HEADS):
        wb_ref[h] = jnp.broadcast_to(wv[:, IDX_DIM + h:IDX_DIM + h + 1], (tq, LANES))
    qi = qi_ref[0]

    def score_body(c, carry):
        lg = _dot(qi, kit_ref[:, chunk(c)])
        sc = jnp.maximum(lg[0:tq], 0.0) * tile_l(wb_ref[0], ch)
        for h in range(1, IDX_HEADS):
            sc = sc + jnp.maximum(lg[h * tq:(h + 1) * tq], 0.0) * tile_l(wb_ref[h], ch)
        sc = jnp.where(sc == 0.0, 0.0, sc)
        key = _float_to_key(jnp.where(lane_c + c * ch <= qpos, sc, -jnp.inf))
        hi_ref[:, chunk(c)] = (key >> 16).astype(I16)
        lo_ref[:, chunk(c)] = ((key & 0xFFFF) - 32768).astype(I16)
        return carry

    lax.fori_loop(0, n_ch, score_body, 0)

    n_sw = (n_ch + DSA_SWEEP - 1) // DSA_SWEEP

    def pad_body(c, carry):
        hi_ref[:, chunk(c)] = jnp.full((tq, ch), KEY_NEG_INF_HI, I16)
        lo_ref[:, chunk(c)] = jnp.full((tq, ch), KEY_NEG_INF_LO, I16)
        return carry

    lax.fori_loop(n_ch, n_sw * DSA_SWEEP, pad_body, 0)

    def sweep(body, init):
        def trip(t, carry):
            for u in range(DSA_SWEEP):
                carry = body(t * DSA_SWEEP + u, carry)
            return carry
        return lax.fori_loop(0, n_sw, trip, init)

    def count(pred):
        def body(c, acc):
            one = jnp.where(pred(c), jnp.int16(1), jnp.int16(0))
            for j in range(ch // LANES):
                acc = acc + one[:, j * LANES:(j + 1) * LANES]
            return acc
        acc = sweep(body, jnp.zeros((tq, LANES), I16))
        return jnp.sum(acc.astype(F32), axis=1, keepdims=True)

    def kth_largest_16(ref, kth):
        def bit_body(p, u):
            cand_u = u | jnp.left_shift(jnp.int32(1), 15 - p)
            cand = bcast16(cand_u - 32768)
            cnt = count(lambda c: ref[:, chunk(c)] >= cand)
            return jnp.where(cnt >= kth, cand_u, u)
        return lax.fori_loop(0, 16, bit_body, jnp.zeros((tq, 1), I32)) - 32768

    kf = float(topk)
    t_hi = kth_largest_16(hi_ref, kf)
    t_hi16 = bcast16(t_hi)
    cnt_gt_hi = count(lambda c: hi_ref[:, chunk(c)] > t_hi16)

    def low_body(c, carry):
        lo_ref[:, chunk(c)] = jnp.where(hi_ref[:, chunk(c)] == t_hi16, lo_ref[:, chunk(c)],
                                        jnp.int16(-32768))
        return carry

    sweep(low_body, 0)
    t_lo = kth_largest_16(lo_ref, kf - cnt_gt_hi)
    t_lo16 = bcast16(t_lo)

    def is_tie(c):
        return (hi_ref[:, chunk(c)] == t_hi16) & (lo_ref[:, chunk(c)] == t_lo16)

    cnt_gt = cnt_gt_hi + count(lambda c: lo_ref[:, chunk(c)] > t_lo16)
    need = kf - cnt_gt
    below_all = (t_hi == KEY_NEG_INF_HI) & (t_lo == KEY_NEG_INF_LO)
    tie_row = (count(is_tie) > need) & jnp.logical_not(below_all)
    row_pos = lax.broadcasted_iota(I32, (tq, 1), 0) + i * tq
    j_default = jnp.where(below_all, row_pos, jnp.int32(seq - 1))

    def idx16(c):
        return (lane_c + c * ch).astype(I16)

    def tie_break(_):
        nbits = max(1, int(np.ceil(np.log2(seq))))

        def jbit(p, m):
            cand = m | jnp.left_shift(jnp.int32(1), nbits - 1 - p)
            cand16 = bcast16(cand)
            cnt = count(lambda c: is_tie(c) & (idx16(c) < cand16))
            return jnp.where(cnt < need, cand, m)

        m = lax.fori_loop(0, nbits, jbit, jnp.zeros((tq, 1), I32))
        return jnp.where(tie_row, m, j_default)

    any_tie = jnp.max(jnp.where(tie_row, 1.0, 0.0)) > 0.0
    j_last16 = bcast16(lax.cond(any_tie, tie_break, lambda _: j_default, 0))

    def bias_body(c, carry):
        h = hi_ref[:, chunk(c)]
        z = lo_ref[:, chunk(c)]
        sel = (h > t_hi16) | (z > t_lo16) | ((h == t_hi16) & (z == t_lo16) & (idx16(c) <= j_last16))
        bias = jnp.where(sel, jnp.zeros((tq, ch), BF16), jnp.full((tq, ch), MASKED, BF16))
        hi_ref[:, chunk(c)] = pltpu.bitcast(bias, I16)
        return carry

    lax.fori_loop(0, n_ch, bias_body, 0)

    qt = qb_ref[...]
    head_of_lane = lax.broadcasted_iota(I32, (tq, gw), 1) >> 6
    for g in range(2):
        qg = qt[:, g * gw:(g + 1) * gw]
        for h in range(hg):
            qbd_ref[g, h * tq:(h + 1) * tq, :] = jnp.where(head_of_lane == h, qg, jnp.zeros_like(qg))
    m_ref[...] = jnp.full(m_ref.shape, MASKED, F32)
    l_ref[...] = jnp.zeros(l_ref.shape, F32)
    acc_ref[...] = jnp.zeros(acc_ref.shape, F32)

    def attn_body(c, carry):
        bias_ref[...] = pltpu.bitcast(hi_ref[:, chunk(c)], BF16).astype(F32)
        for g in range(2):
            s_ref[g] = _dot(qbd_ref[g], kt_ref[g * gw:(g + 1) * gw, chunk(c)])
        for g in range(2):
            for r in range(hg * tq // DSA_SLAB):
                rows = slice(r * DSA_SLAB, (r + 1) * DSA_SLAB)
                b0 = (r * DSA_SLAB) % tq
                s = s_ref[g, rows, :] + bias_ref[b0:b0 + DSA_SLAB, :]
                m_prev = m_ref[g, rows, :]
                m_new = jnp.maximum(m_prev, jnp.max(s, axis=1, keepdims=True))
                alpha = jnp.exp2(m_prev - m_new)
                p = jnp.exp2(s - tile_l(m_new, ch))
                l_ref[g, rows, :] = alpha * l_ref[g, rows, :] + jnp.sum(p, axis=1, keepdims=True)
                m_ref[g, rows, :] = m_new
                alpha_ref[g, rows, :] = alpha
                p_ref[g, rows, :] = p.astype(BF16)
            acc_ref[g] = (acc_ref[g] * tile_l(alpha_ref[g], gw)
                          + _dot(p_ref[g], v_ref[chunk(c), g * gw:(g + 1) * gw]))
        return carry

    lax.fori_loop(0, n_ch, attn_body, 0)

    outs = []
    for g in range(2):
        a = acc_ref[g] * tile_l(1.0 / l_ref[g], gw)
        og = jnp.zeros((tq, gw), F32)
        for h in range(hg):
            og = og + jnp.where(head_of_lane == h, a[h * tq:(h + 1) * tq], 0.0)
        outs.append(og)
    o_ref[...] = jnp.concatenate(outs, axis=1)


def _dsa(qi_r, kit, wi, qb, kt, v, topk):
    s = qb.shape[0]
    tq = DSA_TQ
    assert s % (DSA_CH * DSA_SWEEP) == 0 and s <= 32768
    rows4 = 4 * tq
    return pl.pallas_call(
        functools.partial(_dsa_kernel, seq=s, topk=topk),
        grid=(s // tq,),
        in_specs=[pl.BlockSpec((1, IDX_HEADS * tq, IDX_DIM), lambda i: (i, 0, 0)),
                  _resident(kit.shape),
                  pl.BlockSpec((tq, LANES), lambda i: (i, 0)),
                  pl.BlockSpec((tq, DSA_W), lambda i: (i, 0)),
                  _resident(kt.shape),
                  _resident(v.shape)],
        out_specs=pl.BlockSpec((tq, DSA_W), lambda i: (i, 0)),
        out_shape=jax.ShapeDtypeStruct((s, DSA_W), F32),
        scratch_shapes=[pltpu.VMEM((tq, s), I16),
                        pltpu.VMEM((tq, s), I16),
                        pltpu.VMEM((IDX_HEADS, tq, LANES), F32),
                        pltpu.VMEM((2, rows4, 4 * DSA_HD), BF16),
                        pltpu.VMEM((2, rows4, DSA_CH), F32),
                        pltpu.VMEM((2, rows4, DSA_CH), BF16),
                        pltpu.VMEM((tq, DSA_CH), F32),
                        pltpu.VMEM((2, rows4, LANES), F32),
                        pltpu.VMEM((2, rows4, LANES), F32),
                        pltpu.VMEM((2, rows4, LANES), F32),
                        pltpu.VMEM((2, rows4, 4 * DSA_HD), F32)],
        compiler_params=_params(("arbitrary",)),
        name="dsa",
    )(qi_r, kit, wi, qb, kt, v)


def _merge_kernel(x_ref, mod_ref, gpre_ref, gpost_ref, oa_ref, ob_ref, wbg_ref, wpg_ref, wpd_ref,
                  wout_ref, o_ref):
    x = x_ref[...]
    h = _modulated(x, gpre_ref[...], mod_ref, 0, 1)
    gates = _sigmoid(_dot(h.astype(BF16), wbg_ref[...]))
    yg = _dot(oa_ref[...].astype(BF16), wpg_ref[...])
    yd = _dot(ob_ref[...].astype(BF16), wpd_ref[...])
    mix = gates[:, 0:D_MODEL] * yg + gates[:, D_MODEL:] * yd
    out = _dot(mix.astype(BF16), wout_ref[...])
    o_ref[...] = x + mod_ref[2:3, :] * _rms(out, gpost_ref[...])


def _merge(x, mod, gpre, gpost, oa, ob, wbg, wpg, wpd, wout):
    s = x.shape[0]
    tm = PROJ_ROWS
    row = lambda n: pl.BlockSpec((tm, n), lambda i: (i, 0))
    return pl.pallas_call(
        _merge_kernel,
        grid=(s // tm,),
        in_specs=[row(D_MODEL), _full(mod.shape), _full(gpre.shape), _full(gpost.shape),
                  row(GLA_V), row(DSA_W), _full(wbg.shape), _full(wpg.shape), _full(wpd.shape),
                  _full(wout.shape)],
        out_specs=row(D_MODEL),
        out_shape=jax.ShapeDtypeStruct((s, D_MODEL), F32),
        compiler_params=_params(("arbitrary",)),
        name="merge",
    )(x, mod, gpre, gpost, oa, ob, wbg, wpg, wpd, wout)


def _router(lg):
    t = lg.shape[0]
    lane = lax.broadcasted_iota(I32, (t, LANES), 1)
    lanef = lane.astype(F32)
    big = 1e9
    gm = lane < N_GROUPS
    gmax = jnp.max(jnp.where(gm, lg, -jnp.inf), axis=1, keepdims=True)
    gsum = jnp.sum(jnp.where(gm, jnp.exp(lg - gmax), 0.0), axis=1, keepdims=True)
    p_g = 1.0 / gsum
    g_sel = jnp.min(jnp.where(gm & (lg == gmax), lanef, big), axis=1, keepdims=True)
    lo = N_GROUPS + EXPERTS_PER_GROUP * g_sel
    em = (lanef >= lo) & (lanef < lo + EXPERTS_PER_GROUP)
    m1 = jnp.max(jnp.where(em, lg, -jnp.inf), axis=1, keepdims=True)
    i1 = jnp.min(jnp.where(em & (lg == m1), lanef, big), axis=1, keepdims=True)
    em2 = em & (lanef != i1)
    m2 = jnp.max(jnp.where(em2, lg, -jnp.inf), axis=1, keepdims=True)
    i2 = jnp.min(jnp.where(em2 & (lg == m2), lanef, big), axis=1, keepdims=True)
    e2 = jnp.exp(m2 - m1)
    inv = 1.0 / (1.0 + e2)
    return (jnp.where(lanef == i1, p_g * inv, 0.0) + jnp.where(lanef == i2, p_g * (e2 * inv), 0.0))


def _moe_kernel(x_ref, mod_ref, gpre_ref, gpost_ref, wr_ref, br_ref, wg_ref, wu_ref, wd_ref,
                o_ref, hb_ref, comb_ref, acc_ref):
    j = pl.program_id(1)
    tm = x_ref.shape[0]

    @pl.when(j == 0)
    def _():
        h = _modulated(x_ref[...], gpre_ref[...], mod_ref, 3, 4)
        hb = h.astype(BF16)
        hb_ref[...] = hb
        comb_ref[...] = _router(_dot(hb, wr_ref[...]) + br_ref[...])
        acc_ref[...] = jnp.zeros_like(acc_ref)

    hb = hb_ref[...]
    hgate = _dot(hb, wg_ref[...])
    hup = _dot(hb, wu_ref[...])
    act = hgate * _sigmoid(hgate) * hup
    comb = comb_ref[...]
    lane = lax.broadcasted_iota(I32, (tm, LANES), 1)
    parts = []
    for e in range(MOE_EB):
        sel = lane == (N_GROUPS + j * MOE_EB + e)
        cw = jnp.sum(jnp.where(sel, comb, 0.0), axis=1, keepdims=True)
        parts.append((act[:, e * D_EXPERT:(e + 1) * D_EXPERT] * cw).astype(BF16))
    acc_ref[...] += _dot(jnp.concatenate(parts, axis=1), wd_ref[...])

    @pl.when(j == pl.num_programs(1) - 1)
    def _():
        o_ref[...] = x_ref[...] + mod_ref[5:6, :] * _rms(acc_ref[...], gpost_ref[...])


def _moe(x, mod, gpre, gpost, wr, br, wg, wu, wd):
    s = x.shape[0]
    tm = MOE_ROWS
    bw = MOE_EB * D_EXPERT
    return pl.pallas_call(
        _moe_kernel,
        grid=(s // tm, N_EXPERTS // MOE_EB),
        in_specs=[pl.BlockSpec((tm, D_MODEL), lambda i, j: (i, 0)),
                  _full(mod.shape), _full(gpre.shape), _full(gpost.shape), _full(wr.shape),
                  _full(br.shape),
                  pl.BlockSpec((D_MODEL, bw), lambda i, j: (0, j)),
                  pl.BlockSpec((D_MODEL, bw), lambda i, j: (0, j)),
                  pl.BlockSpec((bw, D_MODEL), lambda i, j: (j, 0))],
        out_specs=pl.BlockSpec((tm, D_MODEL), lambda i, j: (i, 0)),
        out_shape=jax.ShapeDtypeStruct((s, D_MODEL), F32),
        scratch_shapes=[pltpu.VMEM((tm, D_MODEL), BF16), pltpu.VMEM((tm, LANES), F32),
                        pltpu.VMEM((tm, D_MODEL), F32)],
        compiler_params=_params(("arbitrary", "arbitrary")),
        name="moe",
    )(x, mod, gpre, gpost, wr, br, wg, wu, wd)


def _rope_lane_constants():
    rot = DSA_HD // ROT_FRAC
    half = rot // 2
    freqs = np.power(np.float32(ROPE_THETA), -np.arange(half, dtype=np.float32) * np.float32(2.0) / rot)
    j = np.arange(LANES) % DSA_HD
    fv = np.where(j < rot, freqs[j % half], 0.0).astype(np.float32)
    sg = np.where(j < half, -1.0, np.where(j < rot, 1.0, 0.0)).astype(np.float32)
    return jnp.asarray(fv)[None, :], jnp.asarray(sg)[None, :]


def _pad_cols(w, n):
    return jnp.pad(w, ((0, 0), (0, n - w.shape[1])))


def _layer(x, c, pos, w_ada, b_ada, g_pre_mix, g_post_mix, g_pre_ffn, g_post_ffn, w_in, w_gla_a2,
           b_gla_a, g_gla_out, g_idx_k, w_proj_gla, w_proj_dsa, w_out, w_router_g, b_router_g,
           w_router_e, b_router_e, w_e_gate, w_e_up, w_e_down):
    s = x.shape[0]
    mod = _ada(jnp.broadcast_to(c, (8, D_MODEL)), w_ada, b_ada[None, :])[0].reshape(N_MOD, D_MODEL)

    o = np.cumsum((GLA_QK, GLA_QK, GLA_V, GLA_V, GLA_GATE_RANK, DSA_W, DSA_W, DSA_W, IDX_Q, IDX_DIM,
                   IDX_HEADS, 2 * D_MODEL))
    wb = w_in.astype(BF16)
    w_gla = wb[:, 0:o[3]]
    w_ga = _pad_cols(wb[:, o[3]:o[4]], LANES)
    w_dsa = wb[:, o[4]:o[8]]
    w_sm = _pad_cols(jnp.concatenate([wb[:, o[8]:o[9]], wb[:, o[9]:o[10]]], axis=1), LANES)
    w_bg = wb[:, o[10]:o[11]]
    w_a2 = jnp.pad(w_gla_a2.astype(BF16), ((0, LANES - GLA_GATE_RANK), (0, 0)))

    q_a, k_a, v_a, gg, la = _gla_proj(x, mod, g_pre_mix[None, :], w_gla, w_ga, w_a2, b_gla_a[None, :])
    o_a = _gla(q_a, k_a, v_a, gg, la, jnp.tile(g_gla_out, GLA_HEADS)[None, :])

    fv, sg = _rope_lane_constants()
    gik = jnp.pad(g_idx_k, (0, LANES - IDX_DIM))[None, :]
    q_b, k_b, v_b, qi, ki, wi = _dsa_proj(x, mod, g_pre_mix[None, :], pos.astype(F32)[:, None], fv, sg,
                                          w_dsa, w_sm, gik)
    nqb = s // DSA_TQ
    qi_r = qi.reshape(nqb, DSA_TQ, IDX_HEADS, IDX_DIM).transpose(0, 2, 1, 3).reshape(
        nqb, IDX_HEADS * DSA_TQ, IDX_DIM)
    o_b = _dsa(qi_r, ki.T, wi, q_b, k_b.T, v_b, min(DSA_TOPK_MAX, s // 4))

    x1 = _merge(x, mod, g_pre_mix[None, :], g_post_mix[None, :], o_a, o_b, w_bg,
                w_proj_gla.astype(BF16), w_proj_dsa.astype(BF16), w_out.astype(BF16))

    wr = _pad_cols(jnp.concatenate([w_router_g, w_router_e], axis=1).astype(BF16), LANES)
    br = jnp.pad(jnp.concatenate([b_router_g, b_router_e]), (0, LANES - N_GROUPS - N_EXPERTS))[None, :]
    wg = w_e_gate.astype(BF16).transpose(1, 0, 2).reshape(D_MODEL, N_EXPERTS * D_EXPERT)
    wu = w_e_up.astype(BF16).transpose(1, 0, 2).reshape(D_MODEL, N_EXPERTS * D_EXPERT)
    wd = w_e_down.astype(BF16).reshape(N_EXPERTS * D_EXPERT, D_MODEL)
    return _moe(x1, mod, g_pre_ffn[None, :], g_post_ffn[None, :], wr, br, wg, wu, wd)


def kernel(x, c, positions, w_ada, b_ada, g_pre_mix, g_post_mix, g_pre_ffn, g_post_ffn, w_in, w_gla_a2,
           b_gla_a, g_gla_out, g_idx_k, w_proj_gla, w_proj_dsa, w_out, w_router_g, b_router_g,
           w_router_e, b_router_e, w_e_gate, w_e_up, w_e_down):
    batch, depth = x.shape[0], w_ada.shape[0]
    outs = []
    for bi in range(batch):
        xb = x[bi]
        for l in range(depth):
            xb = _layer(xb, c[bi:bi + 1], positions[bi], w_ada[l], b_ada[l], g_pre_mix[l], g_post_mix[l],
                        g_pre_ffn[l], g_post_ffn[l], w_in[l], w_gla_a2[l], b_gla_a[l], g_gla_out[l],
                        g_idx_k[l], w_proj_gla[l], w_proj_dsa[l], w_out[l], w_router_g[l],
                        b_router_g[l], w_router_e[l], b_router_e[l], w_e_gate[l], w_e_up[l],
                        w_e_down[l])
        outs.append(xb)
    return jnp.stack(outs, axis=0)
```

```python
import functools

import numpy as np
import jax
import jax.numpy as jnp
from jax import lax
from jax.experimental import pallas as pl
from jax.experimental.pallas import tpu as pltpu

F32 = jnp.float32
BF16 = jnp.bfloat16
I32 = jnp.int32

D_MODEL = 1024
EPS = 1e-6
ROPE_THETA = 500000.0
ROT_FRAC = 4
GLA_HEADS = 4
GLA_DK = 64
GLA_DV = 128
GLA_GATE_RANK = 16
GLA_TAU = 16.0
GLA_CHUNK = 64
DSA_HEADS = 8
DSA_HD = 64
IDX_HEADS = 8
IDX_DIM = 64
DSA_TOPK_MAX = 256
N_GROUPS = 4
EXPERTS_PER_GROUP = 8
N_EXPERTS = N_GROUPS * EXPERTS_PER_GROUP
D_EXPERT = D_MODEL // 4
N_MOD = 6

GLA_QK = GLA_HEADS * GLA_DK
GLA_V = GLA_HEADS * GLA_DV
DSA_W = DSA_HEADS * DSA_HD
IDX_Q = IDX_HEADS * IDX_DIM

LANES = 128
VMEM_LIMIT_BYTES = 56 * 1024 * 1024

PROJ_ROWS = 512
GLA_ROWS = 512
DSA_TQ = 128
DSA_CH = 512
MOE_ROWS = 512
MOE_EB = 4
MASKED = -1e30
LOG2E = float(np.log2(np.e))
GLA_LEVELS = 7


def _dot(a, b):
    return jnp.dot(a, b, preferred_element_type=F32)


def _dot_nt(a, b):
    return lax.dot_general(a, b, (((1,), (1,)), ((), ())), preferred_element_type=F32)


def _dot_tn(a, b):
    return lax.dot_general(a, b, (((0,), (0,)), ((), ())), preferred_element_type=F32)


def _sigmoid(x):
    return 1.0 / (1.0 + jnp.exp(-x))


def _rms(x, g):
    ms = jnp.mean(x * x, axis=-1, keepdims=True)
    return x * lax.rsqrt(ms + EPS) * g


def _modulated(x, g, mod_ref, shift_row, scale_row):
    return (_rms(x, g) * (1.0 + mod_ref[scale_row:scale_row + 1, :])
            + mod_ref[shift_row:shift_row + 1, :])


def _params(sem):
    return pltpu.CompilerParams(dimension_semantics=sem, vmem_limit_bytes=VMEM_LIMIT_BYTES)


def _full(shape):
    return pl.BlockSpec(shape, lambda *_: (0,) * len(shape))


def _resident(shape):
    return pl.BlockSpec(shape, lambda *_: (0,) * len(shape), pipeline_mode=pl.Buffered(1))


def _ada_kernel(c_ref, w_ref, b_ref, o_ref):
    c = c_ref[...]
    a = c * _sigmoid(c)
    o_ref[...] = _dot(a.astype(BF16), w_ref[...].astype(BF16)) + b_ref[...]


def _ada(c8, w, b):
    n = w.shape[1]
    bn = 1536
    return pl.pallas_call(
        _ada_kernel,
        grid=(n // bn,),
        in_specs=[_full(c8.shape),
                  pl.BlockSpec((D_MODEL, bn), lambda j: (0, j)),
                  pl.BlockSpec((1, bn), lambda j: (0, j))],
        out_specs=pl.BlockSpec((8, bn), lambda j: (0, j)),
        out_shape=jax.ShapeDtypeStruct((8, n), F32),
        compiler_params=_params(("arbitrary",)),
        name="ada",
    )(c8, w, b)


def _gla_proj_kernel(x_ref, mod_ref, g_ref, w_ref, wga_ref, wa2_ref, ba_ref,
                     q_ref, k_ref, v_ref, gg_ref, la_ref):
    h = _modulated(x_ref[...], g_ref[...], mod_ref, 0, 1)
    hb = h.astype(BF16)
    p = _dot(hb, w_ref[...])
    q_ref[...] = p[:, 0:GLA_QK] * (GLA_DK ** -0.5)
    k_ref[...] = p[:, GLA_QK:2 * GLA_QK]
    v_ref[...] = p[:, 2 * GLA_QK:2 * GLA_QK + GLA_V]
    gg_ref[...] = p[:, 2 * GLA_QK + GLA_V:]
    ga = _dot(hb, wga_ref[...])
    z = _dot(ga.astype(BF16), wa2_ref[...]) + ba_ref[...]
    log_sig = jnp.minimum(z, 0.0) - jnp.log1p(jnp.exp(-jnp.abs(z)))
    la_ref[...] = log_sig * (1.0 / GLA_TAU)


def _gla_proj(x, mod, g, w, wga, wa2, ba):
    s = x.shape[0]
    tm = PROJ_ROWS
    row = lambda n: pl.BlockSpec((tm, n), lambda i: (i, 0))
    return pl.pallas_call(
        _gla_proj_kernel,
        grid=(s // tm,),
        in_specs=[row(D_MODEL), _full(mod.shape), _full(g.shape), _full(w.shape),
                  _full(wga.shape), _full(wa2.shape), _full(ba.shape)],
        out_specs=[row(GLA_QK), row(GLA_QK), row(GLA_V), row(GLA_V), row(GLA_QK)],
        out_shape=[jax.ShapeDtypeStruct((s, GLA_QK), F32), jax.ShapeDtypeStruct((s, GLA_QK), F32),
                   jax.ShapeDtypeStruct((s, GLA_V), F32), jax.ShapeDtypeStruct((s, GLA_V), F32),
                   jax.ShapeDtypeStruct((s, GLA_QK), F32)],
        compiler_params=_params(("arbitrary",)),
        name="gla_proj",
    )(x, mod, g, w, wga, wa2, ba)


def _dsa_proj_kernel(x_ref, mod_ref, g_ref, pos_ref, fv_ref, sg_ref, w_ref, wsm_ref, gik_ref,
                     qb_ref, kb_ref, vb_ref, qi_ref, ki_ref, wi_ref):
    tm = x_ref.shape[0]
    h = _modulated(x_ref[...], g_ref[...], mod_ref, 0, 1)
    hb = h.astype(BF16)
    ang = pos_ref[...] * fv_ref[...]
    cs = jnp.cos(ang)
    sn = jnp.sin(ang) * sg_ref[...]
    lane = lax.broadcasted_iota(I32, (tm, LANES), 1)
    first = (lane & (DSA_HD - 1)) < (DSA_HD // ROT_FRAC // 2)

    def rope(t):
        width = t.shape[1]
        rep = width // LANES
        tile = (lambda a: jnp.concatenate([a] * rep, axis=1)) if rep > 1 else (lambda a: a)
        half = DSA_HD // ROT_FRAC // 2
        fwd = pltpu.roll(t, half, 1)
        bwd = pltpu.roll(t, width - half, 1)
        partner = jnp.where(tile(first), bwd, fwd)
        return t * tile(cs) + partner * tile(sn)

    p = _dot(hb, w_ref[...])
    qb_ref[...] = (rope(p[:, 0:DSA_W]) * (DSA_HD ** -0.5 * LOG2E)).astype(BF16)
    kb_ref[...] = rope(p[:, DSA_W:2 * DSA_W]).astype(BF16)
    vb_ref[...] = p[:, 2 * DSA_W:3 * DSA_W].astype(BF16)
    qi_ref[...] = (rope(p[:, 3 * DSA_W:]) * (IDX_DIM ** -0.5)).astype(BF16)

    sm = _dot(hb, wsm_ref[...])
    is_ik = lane < IDX_DIM
    mu = jnp.sum(jnp.where(is_ik, sm, 0.0), axis=-1, keepdims=True) * (1.0 / IDX_DIM)
    xc = jnp.where(is_ik, sm - mu, 0.0)
    var = jnp.sum(xc * xc, axis=-1, keepdims=True) * (1.0 / IDX_DIM)
    y = xc * lax.rsqrt(var + EPS) * gik_ref[...]
    ki_ref[...] = rope(y)[:, 0:IDX_DIM].astype(BF16)
    wi_ref[...] = sm * (IDX_HEADS ** -0.5)


def _dsa_proj(x, mod, g, pos, fv, sg, w, wsm, gik):
    s = x.shape[0]
    tm = PROJ_ROWS
    row = lambda n: pl.BlockSpec((tm, n), lambda i: (i, 0))
    return pl.pallas_call(
        _dsa_proj_kernel,
        grid=(s // tm,),
        in_specs=[row(D_MODEL), _full(mod.shape), _full(g.shape), row(1), _full(fv.shape),
                  _full(sg.shape), _full(w.shape), _full(wsm.shape), _full(gik.shape)],
        out_specs=[row(DSA_W), row(DSA_W), row(DSA_W), row(IDX_Q), row(IDX_DIM), row(LANES)],
        out_shape=[jax.ShapeDtypeStruct((s, DSA_W), BF16), jax.ShapeDtypeStruct((s, DSA_W), BF16),
                   jax.ShapeDtypeStruct((s, DSA_W), BF16), jax.ShapeDtypeStruct((s, IDX_Q), BF16),
                   jax.ShapeDtypeStruct((s, IDX_DIM), BF16), jax.ShapeDtypeStruct((s, LANES), F32)],
        compiler_params=_params(("arbitrary",)),
        name="dsa_proj",
    )(x, mod, g, pos, fv, sg, w, wsm, gik)


def _gla_constants():
    c = GLA_CHUNK
    tril = np.tril(np.ones((c, c), np.float32))
    t = np.arange(c)
    mats = [tril]
    masks = [(t[:, None] == t[None, :])]
    for hs in (32, 16, 8, 4, 2, 1):
        blk = 2 * hs
        r = (t // blk) * blk + hs - 1
        mats.append(tril - tril[r, :])
        same = (t[:, None] // blk) == (t[None, :] // blk)
        masks.append(same & ((t[:, None] % blk) >= hs) & ((t[None, :] % blk) < hs))
    m_all = np.concatenate(mats, axis=0)
    lvl = np.stack([np.tile(m.astype(np.float32), (1, GLA_HEADS)) for m in masks])
    hrow = np.arange(GLA_HEADS * c) // c
    wmask = (hrow[:, None] == (np.arange(GLA_QK) // GLA_DK)[None, :]).astype(np.float32)
    vmask = (hrow[:, None] == (np.arange(GLA_V) // GLA_DV)[None, :]).astype(np.float32)
    smask = ((np.arange(GLA_V) // GLA_DV)[:, None] == (np.arange(GLA_QK) // GLA_DK)[None, :])
    return (jnp.asarray(m_all, BF16), jnp.asarray(lvl), jnp.asarray(wmask), jnp.asarray(vmask),
            jnp.asarray(smask.astype(np.float32)))


def _gla_kernel(q_ref, k_ref, v_ref, gg_ref, la_ref, gout_ref, mall_ref, lvl_ref, wmask_ref,
                vmask_ref, smask_ref, o_ref, st_ref):
    c = GLA_CHUNK

    @pl.when(pl.program_id(0) == 0)
    def _():
        st_ref[...] = jnp.zeros_like(st_ref)

    m_all = mall_ref[...]
    wmask = wmask_ref[...]
    vmask = vmask_ref[...]
    smask = smask_ref[...]
    gout = gout_ref[...]

    def chunk(ci, carry):
        r0 = pl.multiple_of(ci * c, c)
        rows = pl.ds(r0, c)
        q = q_ref[rows, :]
        k = k_ref[rows, :]
        v = v_ref[rows, :]
        la = la_ref[rows, :]
        hi = la.astype(BF16)
        r1 = la - hi.astype(F32)
        mid = r1.astype(BF16)
        lo = (r1 - mid.astype(F32)).astype(BF16)
        dall = _dot(m_all, hi) + _dot(m_all, mid) + _dot(m_all, lo)
        b = dall[0:c]
        b_last = b[c - 1:c, :]
        qhat = q * jnp.exp(b)
        khat = k * jnp.exp(b_last - b)

        a = jnp.zeros((c, GLA_HEADS * c), F32)
        for lv in range(GLA_LEVELS):
            if lv == 0:
                qt, kt = q, k
            else:
                d = dall[lv * c:(lv + 1) * c]
                qt = q * jnp.exp(jnp.minimum(d, 0.0))
                kt = k * jnp.exp(jnp.minimum(-d, 0.0))
            w = (jnp.concatenate([kt] * GLA_HEADS, axis=0) * wmask).astype(BF16)
            a = a + lvl_ref[lv] * _dot_nt(qt.astype(BF16), w)

        st = st_ref[...]
        vbd = (jnp.concatenate([v] * GLA_HEADS, axis=0) * vmask).astype(BF16)
        o = _dot(a.astype(BF16), vbd) + _dot_nt(qhat.astype(BF16), st.astype(BF16))
        st_ref[...] = st * jnp.exp(b_last) + smask * _dot_tn(v.astype(BF16), khat.astype(BF16))

        parts = []
        for hh in range(GLA_HEADS):
            oh = o[:, hh * GLA_DV:(hh + 1) * GLA_DV]
            ms = jnp.mean(oh * oh, axis=-1, keepdims=True)
            parts.append(oh * lax.rsqrt(ms + EPS))
        gg = gg_ref[rows, :]
        o_ref[rows, :] = jnp.concatenate(parts, axis=1) * gout * (gg * _sigmoid(gg))
        return carry

    lax.fori_loop(0, q_ref.shape[0] // c, chunk, 0)


def _gla(q, k, v, gg, la, gout):
    s = q.shape[0]
    tb = GLA_ROWS
    consts = _gla_constants()
    row = lambda n: pl.BlockSpec((tb, n), lambda i: (i, 0))
    return pl.pallas_call(
        _gla_kernel,
        grid=(s // tb,),
        in_specs=[row(GLA_QK), row(GLA_QK), row(GLA_V), row(GLA_V), row(GLA_QK), _full(gout.shape)]
                 + [_full(a.shape) for a in consts],
        out_specs=row(GLA_V),
        out_shape=jax.ShapeDtypeStruct((s, GLA_V), F32),
        scratch_shapes=[pltpu.VMEM((GLA_V, GLA_QK), F32)],
        compiler_params=_params(("arbitrary",)),
        name="gla",
    )(q, k, v, gg, la, gout, *consts)


DSA_SLAB = 32
WORD = 32
DSA_GROUP = WORD * LANES


def _float_to_ordered_bits(x):
    bits = lax.bitcast_convert_type(x, I32)
    key = bits ^ ((bits >> 31) & jnp.int32(0x7FFFFFFF))
    return key ^ jnp.int32(-2147483648)


def _transpose_bits(a):
    a = list(a)
    j, msk = 16, 0x0000FFFF
    while j:
        m32 = jnp.int32(np.array(msk, np.uint32).view(np.int32))
        k = 0
        while k < WORD:
            t = (a[k] ^ (a[k + j] >> j)) & m32
            a[k] = a[k] ^ t
            a[k + j] = a[k + j] ^ (t << j)
            k = (k + j + 1) & ~j
        j >>= 1
        msk = (msk ^ (msk << j)) & 0xFFFFFFFF
    return a


def _dsa_kernel(qi_ref, kit_ref, wi_ref, qb_ref, kt_ref, v_ref, o_ref,
                key_ref, alive_ref, great_ref, wb_ref, qbd_ref, s_ref, p_ref, bias_ref, m_ref, l_ref,
                alpha_ref, acc_ref, *, seq, topk):
    tq = DSA_TQ
    ch = DSA_CH
    hg = 4
    gw = hg * DSA_HD
    bpc = ch // LANES
    cpg = DSA_GROUP // ch
    n_groups = seq // DSA_GROUP
    i = pl.program_id(0)
    n_ch = ((i + 1) * tq + ch - 1) // ch
    lane_i = lax.broadcasted_iota(I32, (tq, LANES), 1)
    row_pos = lax.broadcasted_iota(I32, (tq, 1), 0) + i * tq
    kf = float(topk)

    def tile_l(a, width):
        return jnp.concatenate([a] * (width // a.shape[1]), axis=1)

    def chunk(c):
        return pl.ds(pl.multiple_of(c * ch, ch), ch)

    def lane_sum(a):
        return jnp.sum(a.astype(F32), axis=1, keepdims=True)

    @pl.when(i == 0)
    def _():
        key_ref[...] = jnp.zeros(key_ref.shape, I32)

    wv = wi_ref[...]
    for h in range(IDX_HEADS):
        wb_ref[h] = jnp.broadcast_to(wv[:, IDX_DIM + h:IDX_DIM + h + 1], (tq, LANES))
    qi = qi_ref[0]

    def score_body(c, carry):
        lg = _dot(qi, kit_ref[:, chunk(c)])
        sc = jnp.maximum(lg[0:tq], 0.0) * tile_l(wb_ref[0], ch)
        for h in range(1, IDX_HEADS):
            sc = sc + jnp.maximum(lg[h * tq:(h + 1) * tq], 0.0) * tile_l(wb_ref[h], ch)
        sc = jnp.where(sc == 0.0, 0.0, sc)
        key_ref[c] = _float_to_ordered_bits(sc)
        return carry

    lax.fori_loop(0, n_ch, score_body, 0)

    n_pg = (n_ch + cpg - 1) // cpg

    def plane_body(t, carry):
        g = t // (tq // 8)
        rows = pl.ds(pl.multiple_of((t % (tq // 8)) * 8, 8), 8)

        def slot(b):
            return (g * cpg + b // bpc, rows, slice((b % bpc) * LANES, (b % bpc + 1) * LANES))

        planes = _transpose_bits([key_ref[slot(WORD - 1 - k)] for k in range(WORD)])
        for b in range(WORD):
            key_ref[slot(b)] = planes[b]
        return carry

    lax.fori_loop(0, n_pg * (tq // 8), plane_body, 0)

    def index_mask(g, bound):
        r = bound - g * DSA_GROUP - lane_i
        q = jnp.clip((r + (LANES - 1)) >> 7, 0, WORD)
        return jnp.where(q >= WORD, jnp.int32(-1), (jnp.int32(1) << jnp.minimum(q, WORD - 1)) - 1)

    for g in range(n_groups):
        alive_ref[g] = index_mask(g, row_pos + 1)
        great_ref[g] = jnp.zeros((tq, LANES), I32)

    def bit_body(p, cnt_great):
        pc = p // bpc
        pl0 = pl.multiple_of((p % bpc) * LANES, LANES)
        ones = []
        acc = jnp.zeros((tq, LANES), I32)
        for g in range(n_groups):
            x = alive_ref[g] & key_ref[g * cpg + pc, :, pl.ds(pl0, LANES)]
            acc = acc + lax.population_count(x)
            ones.append(x)
        cnt_one = lane_sum(acc)
        take = (cnt_great + cnt_one) >= kf
        take_b = jnp.broadcast_to(take, (tq, LANES))
        for g in range(n_groups):
            a = alive_ref[g]
            alive_ref[g] = jnp.where(take_b, ones[g], a ^ ones[g])
            great_ref[g] = jnp.where(take_b, great_ref[g], great_ref[g] | ones[g])
        return jnp.where(take, cnt_great, cnt_great + cnt_one)

    cnt_great = lax.fori_loop(0, WORD, bit_body, jnp.zeros((tq, 1), F32))
    need = kf - cnt_great

    def count_alive(bound):
        acc = jnp.zeros((tq, LANES), I32)
        for g in range(n_groups):
            acc = acc + lax.population_count(alive_ref[g] & index_mask(g, bound))
        return lane_sum(acc)

    tie_row = count_alive(jnp.full((tq, 1), seq, I32)) > need

    def tie_break(_):
        nbits = max(1, int(np.ceil(np.log2(seq))))

        def jbit(p, m):
            cand = m | jnp.left_shift(jnp.int32(1), nbits - 1 - p)
            return jnp.where(count_alive(cand) < need, cand, m)

        m = lax.fori_loop(0, nbits, jbit, jnp.zeros((tq, 1), I32))
        return jnp.where(tie_row, m + 1, jnp.int32(seq))

    any_tie = jnp.max(jnp.where(tie_row, 1.0, 0.0)) > 0.0
    bound = lax.cond(any_tie, tie_break, lambda _: jnp.full((tq, 1), seq, I32), 0)
    for g in range(n_groups):
        great_ref[g] = great_ref[g] | (alive_ref[g] & index_mask(g, bound))

    qt = qb_ref[...]
    head_of_lane = lax.broadcasted_iota(I32, (tq, gw), 1) >> 6
    for g in range(2):
        qg = qt[:, g * gw:(g + 1) * gw]
        for h in range(hg):
            qbd_ref[g, h * tq:(h + 1) * tq, :] = jnp.where(head_of_lane == h, qg, jnp.zeros_like(qg))
    m_ref[...] = jnp.full(m_ref.shape, MASKED, F32)
    l_ref[...] = jnp.zeros(l_ref.shape, F32)
    acc_ref[...] = jnp.zeros(acc_ref.shape, F32)

    def attn_body(c, carry):
        sel = great_ref[c // cpg]
        for jj in range(bpc):
            bit = (sel >> ((c % cpg) * bpc + jj)) & 1
            bias_ref[:, jj * LANES:(jj + 1) * LANES] = jnp.where(bit != 0, 0.0, MASKED)
        for g in range(2):
            s_ref[g] = _dot(qbd_ref[g], kt_ref[g * gw:(g + 1) * gw, chunk(c)])
        for g in range(2):
            for r in range(hg * tq // DSA_SLAB):
                rows = slice(r * DSA_SLAB, (r + 1) * DSA_SLAB)
                b0 = (r * DSA_SLAB) % tq
                s = s_ref[g, rows, :] + bias_ref[b0:b0 + DSA_SLAB, :]
                m_prev = m_ref[g, rows, :]
                m_new = jnp.maximum(m_prev, jnp.max(s, axis=1, keepdims=True))
                alpha = jnp.exp2(m_prev - m_new)
                p = jnp.exp2(s - tile_l(m_new, ch))
                l_ref[g, rows, :] = alpha * l_ref[g, rows, :] + jnp.sum(p, axis=1, keepdims=True)
                m_ref[g, rows, :] = m_new
                alpha_ref[g, rows, :] = alpha
                p_ref[g, rows, :] = p.astype(BF16)
            acc_ref[g] = (acc_ref[g] * tile_l(alpha_ref[g], gw)
                          + _dot(p_ref[g], v_ref[chunk(c), g * gw:(g + 1) * gw]))
        return carry

    lax.fori_loop(0, n_ch, attn_body, 0)

    outs = []
    for g in range(2):
        a = acc_ref[g] * tile_l(1.0 / l_ref[g], gw)
        og = jnp.zeros((tq, gw), F32)
        for h in range(hg):
            og = og + jnp.where(head_of_lane == h, a[h * tq:(h + 1) * tq], 0.0)
        outs.append(og)
    o_ref[...] = jnp.concatenate(outs, axis=1)


def _dsa(qi_r, kit, wi, qb, kt, v, topk):
    s = qb.shape[0]
    tq = DSA_TQ
    assert s % DSA_GROUP == 0 and DSA_GROUP % DSA_CH == 0
    rows4 = 4 * tq
    return pl.pallas_call(
        functools.partial(_dsa_kernel, seq=s, topk=topk),
        grid=(s // tq,),
        in_specs=[pl.BlockSpec((1, IDX_HEADS * tq, IDX_DIM), lambda i: (i, 0, 0)),
                  _resident(kit.shape),
                  pl.BlockSpec((tq, LANES), lambda i: (i, 0)),
                  pl.BlockSpec((tq, DSA_W), lambda i: (i, 0)),
                  _resident(kt.shape),
                  _resident(v.shape)],
        out_specs=pl.BlockSpec((tq, DSA_W), lambda i: (i, 0)),
        out_shape=jax.ShapeDtypeStruct((s, DSA_W), F32),
        scratch_shapes=[pltpu.VMEM((s // DSA_CH, tq, DSA_CH), I32),
                        pltpu.VMEM((s // DSA_GROUP, tq, LANES), I32),
                        pltpu.VMEM((s // DSA_GROUP, tq, LANES), I32),
                        pltpu.VMEM((IDX_HEADS, tq, LANES), F32),
                        pltpu.VMEM((2, rows4, 4 * DSA_HD), BF16),
                        pltpu.VMEM((2, rows4, DSA_CH), F32),
                        pltpu.VMEM((2, rows4, DSA_CH), BF16),
                        pltpu.VMEM((tq, DSA_CH), F32),
                        pltpu.VMEM((2, rows4, LANES), F32),
                        pltpu.VMEM((2, rows4, LANES), F32),
                        pltpu.VMEM((2, rows4, LANES), F32),
                        pltpu.VMEM((2, rows4, 4 * DSA_HD), F32)],
        compiler_params=_params(("arbitrary",)),
        name="dsa",
    )(qi_r, kit, wi, qb, kt, v)


def _merge_kernel(x_ref, mod_ref, gpre_ref, gpost_ref, oa_ref, ob_ref, wbg_ref, wpg_ref, wpd_ref,
                  wout_ref, o_ref):
    x = x_ref[...]
    h = _modulated(x, gpre_ref[...], mod_ref, 0, 1)
    gates = _sigmoid(_dot(h.astype(BF16), wbg_ref[...]))
    yg = _dot(oa_ref[...].astype(BF16), wpg_ref[...])
    yd = _dot(ob_ref[...].astype(BF16), wpd_ref[...])
    mix = gates[:, 0:D_MODEL] * yg + gates[:, D_MODEL:] * yd
    out = _dot(mix.astype(BF16), wout_ref[...])
    o_ref[...] = x + mod_ref[2:3, :] * _rms(out, gpost_ref[...])


def _merge(x, mod, gpre, gpost, oa, ob, wbg, wpg, wpd, wout):
    s = x.shape[0]
    tm = PROJ_ROWS
    row = lambda n: pl.BlockSpec((tm, n), lambda i: (i, 0))
    return pl.pallas_call(
        _merge_kernel,
        grid=(s // tm,),
        in_specs=[row(D_MODEL), _full(mod.shape), _full(gpre.shape), _full(gpost.shape),
                  row(GLA_V), row(DSA_W), _full(wbg.shape), _full(wpg.shape), _full(wpd.shape),
                  _full(wout.shape)],
        out_specs=row(D_MODEL),
        out_shape=jax.ShapeDtypeStruct((s, D_MODEL), F32),
        compiler_params=_params(("arbitrary",)),
        name="merge",
    )(x, mod, gpre, gpost, oa, ob, wbg, wpg, wpd, wout)


def _router(lg):
    t = lg.shape[0]
    lane = lax.broadcasted_iota(I32, (t, LANES), 1)
    lanef = lane.astype(F32)
    big = 1e9
    gm = lane < N_GROUPS
    gmax = jnp.max(jnp.where(gm, lg, -jnp.inf), axis=1, keepdims=True)
    gsum = jnp.sum(jnp.where(gm, jnp.exp(lg - gmax), 0.0), axis=1, keepdims=True)
    p_g = 1.0 / gsum
    g_sel = jnp.min(jnp.where(gm & (lg == gmax), lanef, big), axis=1, keepdims=True)
    lo = N_GROUPS + EXPERTS_PER_GROUP * g_sel
    em = (lanef >= lo) & (lanef < lo + EXPERTS_PER_GROUP)
    m1 = jnp.max(jnp.where(em, lg, -jnp.inf), axis=1, keepdims=True)
    i1 = jnp.min(jnp.where(em & (lg == m1), lanef, big), axis=1, keepdims=True)
    em2 = em & (lanef != i1)
    m2 = jnp.max(jnp.where(em2, lg, -jnp.inf), axis=1, keepdims=True)
    i2 = jnp.min(jnp.where(em2 & (lg == m2), lanef, big), axis=1, keepdims=True)
    e2 = jnp.exp(m2 - m1)
    inv = 1.0 / (1.0 + e2)
    return (jnp.where(lanef == i1, p_g * inv, 0.0) + jnp.where(lanef == i2, p_g * (e2 * inv), 0.0))


def _moe_kernel(x_ref, mod_ref, gpre_ref, gpost_ref, wr_ref, br_ref, wg_ref, wu_ref, wd_ref,
                o_ref, hb_ref, comb_ref, acc_ref):
    j = pl.program_id(1)
    tm = x_ref.shape[0]

    @pl.when(j == 0)
    def _():
        h = _modulated(x_ref[...], gpre_ref[...], mod_ref, 3, 4)
        hb = h.astype(BF16)
        hb_ref[...] = hb
        comb_ref[...] = _router(_dot(hb, wr_ref[...]) + br_ref[...])
        acc_ref[...] = jnp.zeros_like(acc_ref)

    hb = hb_ref[...]
    hgate = _dot(hb, wg_ref[...])
    hup = _dot(hb, wu_ref[...])
    act = hgate * _sigmoid(hgate) * hup
    comb = comb_ref[...]
    lane = lax.broadcasted_iota(I32, (tm, LANES), 1)
    parts = []
    for e in range(MOE_EB):
        sel = lane == (N_GROUPS + j * MOE_EB + e)
        cw = jnp.sum(jnp.where(sel, comb, 0.0), axis=1, keepdims=True)
        parts.append((act[:, e * D_EXPERT:(e + 1) * D_EXPERT] * cw).astype(BF16))
    acc_ref[...] += _dot(jnp.concatenate(parts, axis=1), wd_ref[...])

    @pl.when(j == pl.num_programs(1) - 1)
    def _():
        o_ref[...] = x_ref[...] + mod_ref[5:6, :] * _rms(acc_ref[...], gpost_ref[...])


def _moe(x, mod, gpre, gpost, wr, br, wg, wu, wd):
    s = x.shape[0]
    tm = MOE_ROWS
    bw = MOE_EB * D_EXPERT
    return pl.pallas_call(
        _moe_kernel,
        grid=(s // tm, N_EXPERTS // MOE_EB),
        in_specs=[pl.BlockSpec((tm, D_MODEL), lambda i, j: (i, 0)),
                  _full(mod.shape), _full(gpre.shape), _full(gpost.shape), _full(wr.shape),
                  _full(br.shape),
                  pl.BlockSpec((D_MODEL, bw), lambda i, j: (0, j)),
                  pl.BlockSpec((D_MODEL, bw), lambda i, j: (0, j)),
                  pl.BlockSpec((bw, D_MODEL), lambda i, j: (j, 0))],
        out_specs=pl.BlockSpec((tm, D_MODEL), lambda i, j: (i, 0)),
        out_shape=jax.ShapeDtypeStruct((s, D_MODEL), F32),
        scratch_shapes=[pltpu.VMEM((tm, D_MODEL), BF16), pltpu.VMEM((tm, LANES), F32),
                        pltpu.VMEM((tm, D_MODEL), F32)],
        compiler_params=_params(("arbitrary", "arbitrary")),
        name="moe",
    )(x, mod, gpre, gpost, wr, br, wg, wu, wd)


def _rope_lane_constants():
    rot = DSA_HD // ROT_FRAC
    half = rot // 2
    freqs = np.power(np.float32(ROPE_THETA), -np.arange(half, dtype=np.float32) * np.float32(2.0) / rot)
    j = np.arange(LANES) % DSA_HD
    fv = np.where(j < rot, freqs[j % half], 0.0).astype(np.float32)
    sg = np.where(j < half, -1.0, np.where(j < rot, 1.0, 0.0)).astype(np.float32)
    return jnp.asarray(fv)[None, :], jnp.asarray(sg)[None, :]


def _pad_cols(w, n):
    return jnp.pad(w, ((0, 0), (0, n - w.shape[1])))


def _layer(x, c, pos, w_ada, b_ada, g_pre_mix, g_post_mix, g_pre_ffn, g_post_ffn, w_in, w_gla_a2,
           b_gla_a, g_gla_out, g_idx_k, w_proj_gla, w_proj_dsa, w_out, w_router_g, b_router_g,
           w_router_e, b_router_e, w_e_gate, w_e_up, w_e_down):
    s = x.shape[0]
    mod = _ada(jnp.broadcast_to(c, (8, D_MODEL)), w_ada, b_ada[None, :])[0].reshape(N_MOD, D_MODEL)

    o = np.cumsum((GLA_QK, GLA_QK, GLA_V, GLA_V, GLA_GATE_RANK, DSA_W, DSA_W, DSA_W, IDX_Q, IDX_DIM,
                   IDX_HEADS, 2 * D_MODEL))
    wb = w_in.astype(BF16)
    w_gla = wb[:, 0:o[3]]
    w_ga = _pad_cols(wb[:, o[3]:o[4]], LANES)
    w_dsa = wb[:, o[4]:o[8]]
    w_sm = _pad_cols(jnp.concatenate([wb[:, o[8]:o[9]], wb[:, o[9]:o[10]]], axis=1), LANES)
    w_bg = wb[:, o[10]:o[11]]
    w_a2 = jnp.pad(w_gla_a2.astype(BF16), ((0, LANES - GLA_GATE_RANK), (0, 0)))

    q_a, k_a, v_a, gg, la = _gla_proj(x, mod, g_pre_mix[None, :], w_gla, w_ga, w_a2, b_gla_a[None, :])
    o_a = _gla(q_a, k_a, v_a, gg, la, jnp.tile(g_gla_out, GLA_HEADS)[None, :])

    fv, sg = _rope_lane_constants()
    gik = jnp.pad(g_idx_k, (0, LANES - IDX_DIM))[None, :]
    q_b, k_b, v_b, qi, ki, wi = _dsa_proj(x, mod, g_pre_mix[None, :], pos.astype(F32)[:, None], fv, sg,
                                          w_dsa, w_sm, gik)
    nqb = s // DSA_TQ
    qi_r = qi.reshape(nqb, DSA_TQ, IDX_HEADS, IDX_DIM).transpose(0, 2, 1, 3).reshape(
        nqb, IDX_HEADS * DSA_TQ, IDX_DIM)
    o_b = _dsa(qi_r, ki.T, wi, q_b, k_b.T, v_b, min(DSA_TOPK_MAX, s // 4))

    x1 = _merge(x, mod, g_pre_mix[None, :], g_post_mix[None, :], o_a, o_b, w_bg,
                w_proj_gla.astype(BF16), w_proj_dsa.astype(BF16), w_out.astype(BF16))

    wr = _pad_cols(jnp.concatenate([w_router_g, w_router_e], axis=1).astype(BF16), LANES)
    br = jnp.pad(jnp.concatenate([b_router_g, b_router_e]), (0, LANES - N_GROUPS - N_EXPERTS))[None, :]
    wg = w_e_gate.astype(BF16).transpose(1, 0, 2).reshape(D_MODEL, N_EXPERTS * D_EXPERT)
    wu = w_e_up.astype(BF16).transpose(1, 0, 2).reshape(D_MODEL, N_EXPERTS * D_EXPERT)
    wd = w_e_down.astype(BF16).reshape(N_EXPERTS * D_EXPERT, D_MODEL)
    return _moe(x1, mod, g_pre_ffn[None, :], g_post_ffn[None, :], wr, br, wg, wu, wd)


def kernel(x, c, positions, w_ada, b_ada, g_pre_mix, g_post_mix, g_pre_ffn, g_post_ffn, w_in, w_gla_a2,
           b_gla_a, g_gla_out, g_idx_k, w_proj_gla, w_proj_dsa, w_out, w_router_g, b_router_g,
           w_router_e, b_router_e, w_e_gate, w_e_up, w_e_down):
    batch, depth = x.shape[0], w_ada.shape[0]
    outs = []
    for bi in range(batch):
        xb = x[bi]
        for l in range(depth):
            xb = _layer(xb, c[bi:bi + 1], positions[bi], w_ada[l], b_ada[l], g_pre_mix[l], g_post_mix[l],
                        g_pre_ffn[l], g_post_ffn[l], w_in[l], w_gla_a2[l], b_gla_a[l], g_gla_out[l],
                        g_idx_k[l], w_proj_gla[l], w_proj_dsa[l], w_out[l], w_router_g[l],
                        b_router_g[l], w_router_e[l], b_router_e[l], w_e_gate[l], w_e_up[l],
                        w_e_down[l])
        outs.append(xb)
    return jnp.stack(outs, axis=0)
```

```python
import functools

import numpy as np
import jax
import jax.numpy as jnp
from jax import lax
from jax.experimental import pallas as pl
from jax.experimental.pallas import tpu as pltpu

F32 = jnp.float32
BF16 = jnp.bfloat16
I32 = jnp.int32

D_MODEL = 1024
EPS = 1e-6
ROPE_THETA = 500000.0
ROT_FRAC = 4
GLA_HEADS = 4
GLA_DK = 64
GLA_DV = 128
GLA_GATE_RANK = 16
GLA_TAU = 16.0
GLA_CHUNK = 64
DSA_HEADS = 8
DSA_HD = 64
IDX_HEADS = 8
IDX_DIM = 64
DSA_TOPK_MAX = 256
N_GROUPS = 4
EXPERTS_PER_GROUP = 8
N_EXPERTS = N_GROUPS * EXPERTS_PER_GROUP
D_EXPERT = D_MODEL // 4
N_MOD = 6

GLA_QK = GLA_HEADS * GLA_DK
GLA_V = GLA_HEADS * GLA_DV
DSA_W = DSA_HEADS * DSA_HD
IDX_Q = IDX_HEADS * IDX_DIM

LANES = 128
VMEM_LIMIT_BYTES = 56 * 1024 * 1024

PROJ_ROWS = 512
GLA_ROWS = 512
DSA_TQ = 128
DSA_CH = 1024
MOE_ROWS = 512
MOE_EB = 4
MASKED = -1e30
LOG2E = float(np.log2(np.e))
GLA_LEVELS = 7


def _dot(a, b):
    return jnp.dot(a, b, preferred_element_type=F32)


def _dot_nt(a, b):
    return lax.dot_general(a, b, (((1,), (1,)), ((), ())), preferred_element_type=F32)


def _dot_tn(a, b):
    return lax.dot_general(a, b, (((0,), (0,)), ((), ())), preferred_element_type=F32)


def _sigmoid(x):
    return 1.0 / (1.0 + jnp.exp(-x))


def _rms(x, g):
    ms = jnp.mean(x * x, axis=-1, keepdims=True)
    return x * lax.rsqrt(ms + EPS) * g


def _modulated(x, g, mod_ref, shift_row, scale_row):
    return (_rms(x, g) * (1.0 + mod_ref[scale_row:scale_row + 1, :])
            + mod_ref[shift_row:shift_row + 1, :])


def _params(sem):
    return pltpu.CompilerParams(dimension_semantics=sem, vmem_limit_bytes=VMEM_LIMIT_BYTES)


def _full(shape):
    return pl.BlockSpec(shape, lambda *_: (0,) * len(shape))


def _resident(shape):
    return pl.BlockSpec(shape, lambda *_: (0,) * len(shape), pipeline_mode=pl.Buffered(1))


def _ada_kernel(c_ref, w_ref, b_ref, o_ref):
    c = c_ref[...]
    a = c * _sigmoid(c)
    o_ref[...] = _dot(a.astype(BF16), w_ref[...].astype(BF16)) + b_ref[...]


def _ada(c8, w, b):
    n = w.shape[1]
    bn = 1536
    return pl.pallas_call(
        _ada_kernel,
        grid=(n // bn,),
        in_specs=[_full(c8.shape),
                  pl.BlockSpec((D_MODEL, bn), lambda j: (0, j)),
                  pl.BlockSpec((1, bn), lambda j: (0, j))],
        out_specs=pl.BlockSpec((8, bn), lambda j: (0, j)),
        out_shape=jax.ShapeDtypeStruct((8, n), F32),
        compiler_params=_params(("arbitrary",)),
        name="ada",
    )(c8, w, b)


def _gla_proj_kernel(x_ref, mod_ref, g_ref, w_ref, wga_ref, wa2_ref, ba_ref,
                     q_ref, k_ref, v_ref, gg_ref, la_ref):
    h = _modulated(x_ref[...], g_ref[...], mod_ref, 0, 1)
    hb = h.astype(BF16)
    p = _dot(hb, w_ref[...])
    q_ref[...] = p[:, 0:GLA_QK] * (GLA_DK ** -0.5)
    k_ref[...] = p[:, GLA_QK:2 * GLA_QK]
    v_ref[...] = p[:, 2 * GLA_QK:2 * GLA_QK + GLA_V]
    gg_ref[...] = p[:, 2 * GLA_QK + GLA_V:]
    ga = _dot(hb, wga_ref[...])
    z = _dot(ga.astype(BF16), wa2_ref[...]) + ba_ref[...]
    log_sig = jnp.minimum(z, 0.0) - jnp.log1p(jnp.exp(-jnp.abs(z)))
    la_ref[...] = log_sig * (1.0 / GLA_TAU)


def _gla_proj(x, mod, g, w, wga, wa2, ba):
    s = x.shape[0]
    tm = PROJ_ROWS
    row = lambda n: pl.BlockSpec((tm, n), lambda i: (i, 0))
    return pl.pallas_call(
        _gla_proj_kernel,
        grid=(s // tm,),
        in_specs=[row(D_MODEL), _full(mod.shape), _full(g.shape), _full(w.shape),
                  _full(wga.shape), _full(wa2.shape), _full(ba.shape)],
        out_specs=[row(GLA_QK), row(GLA_QK), row(GLA_V), row(GLA_V), row(GLA_QK)],
        out_shape=[jax.ShapeDtypeStruct((s, GLA_QK), F32), jax.ShapeDtypeStruct((s, GLA_QK), F32),
                   jax.ShapeDtypeStruct((s, GLA_V), F32), jax.ShapeDtypeStruct((s, GLA_V), F32),
                   jax.ShapeDtypeStruct((s, GLA_QK), F32)],
        compiler_params=_params(("arbitrary",)),
        name="gla_proj",
    )(x, mod, g, w, wga, wa2, ba)


def _dsa_proj_kernel(x_ref, mod_ref, g_ref, pos_ref, fv_ref, sg_ref, w_ref, wsm_ref, gik_ref,
                     qb_ref, kb_ref, vb_ref, qi_ref, ki_ref, wi_ref):
    tm = x_ref.shape[0]
    h = _modulated(x_ref[...], g_ref[...], mod_ref, 0, 1)
    hb = h.astype(BF16)
    ang = pos_ref[...] * fv_ref[...]
    cs = jnp.cos(ang)
    sn = jnp.sin(ang) * sg_ref[...]
    lane = lax.broadcasted_iota(I32, (tm, LANES), 1)
    first = (lane & (DSA_HD - 1)) < (DSA_HD // ROT_FRAC // 2)

    def rope(t):
        width = t.shape[1]
        rep = width // LANES
        tile = (lambda a: jnp.concatenate([a] * rep, axis=1)) if rep > 1 else (lambda a: a)
        half = DSA_HD // ROT_FRAC // 2
        fwd = pltpu.roll(t, half, 1)
        bwd = pltpu.roll(t, width - half, 1)
        partner = jnp.where(tile(first), bwd, fwd)
        return t * tile(cs) + partner * tile(sn)

    p = _dot(hb, w_ref[...])
    qb_ref[...] = (rope(p[:, 0:DSA_W]) * (DSA_HD ** -0.5 * LOG2E)).astype(BF16)
    kb_ref[...] = rope(p[:, DSA_W:2 * DSA_W]).astype(BF16)
    vb_ref[...] = p[:, 2 * DSA_W:3 * DSA_W].astype(BF16)
    qi_ref[...] = (rope(p[:, 3 * DSA_W:]) * (IDX_DIM ** -0.5)).astype(BF16)

    sm = _dot(hb, wsm_ref[...])
    is_ik = lane < IDX_DIM
    mu = jnp.sum(jnp.where(is_ik, sm, 0.0), axis=-1, keepdims=True) * (1.0 / IDX_DIM)
    xc = jnp.where(is_ik, sm - mu, 0.0)
    var = jnp.sum(xc * xc, axis=-1, keepdims=True) * (1.0 / IDX_DIM)
    y = xc * lax.rsqrt(var + EPS) * gik_ref[...]
    ki_ref[...] = rope(y)[:, 0:IDX_DIM].astype(BF16)
    wi_ref[...] = sm * (IDX_HEADS ** -0.5)


def _dsa_proj(x, mod, g, pos, fv, sg, w, wsm, gik):
    s = x.shape[0]
    tm = PROJ_ROWS
    row = lambda n: pl.BlockSpec((tm, n), lambda i: (i, 0))
    return pl.pallas_call(
        _dsa_proj_kernel,
        grid=(s // tm,),
        in_specs=[row(D_MODEL), _full(mod.shape), _full(g.shape), row(1), _full(fv.shape),
                  _full(sg.shape), _full(w.shape), _full(wsm.shape), _full(gik.shape)],
        out_specs=[row(DSA_W), row(DSA_W), row(DSA_W), row(IDX_Q), row(IDX_DIM), row(LANES)],
        out_shape=[jax.ShapeDtypeStruct((s, DSA_W), BF16), jax.ShapeDtypeStruct((s, DSA_W), BF16),
                   jax.ShapeDtypeStruct((s, DSA_W), BF16), jax.ShapeDtypeStruct((s, IDX_Q), BF16),
                   jax.ShapeDtypeStruct((s, IDX_DIM), BF16), jax.ShapeDtypeStruct((s, LANES), F32)],
        compiler_params=_params(("arbitrary",)),
        name="dsa_proj",
    )(x, mod, g, pos, fv, sg, w, wsm, gik)


def _gla_constants():
    c = GLA_CHUNK
    tril = np.tril(np.ones((c, c), np.float32))
    t = np.arange(c)
    mats = [tril]
    masks = [(t[:, None] == t[None, :])]
    for hs in (32, 16, 8, 4, 2, 1):
        blk = 2 * hs
        r = (t // blk) * blk + hs - 1
        mats.append(tril - tril[r, :])
        same = (t[:, None] // blk) == (t[None, :] // blk)
        masks.append(same & ((t[:, None] % blk) >= hs) & ((t[None, :] % blk) < hs))
    m_all = np.concatenate(mats, axis=0)
    lvl = np.stack([np.tile(m.astype(np.float32), (1, GLA_HEADS)) for m in masks])
    hrow = np.arange(GLA_HEADS * c) // c
    wmask = (hrow[:, None] == (np.arange(GLA_QK) // GLA_DK)[None, :]).astype(np.float32)
    vmask = (hrow[:, None] == (np.arange(GLA_V) // GLA_DV)[None, :]).astype(np.float32)
    smask = ((np.arange(GLA_V) // GLA_DV)[:, None] == (np.arange(GLA_QK) // GLA_DK)[None, :])
    return (jnp.asarray(m_all, BF16), jnp.asarray(lvl), jnp.asarray(wmask), jnp.asarray(vmask),
            jnp.asarray(smask.astype(np.float32)))


def _gla_kernel(q_ref, k_ref, v_ref, gg_ref, la_ref, gout_ref, mall_ref, lvl_ref, wmask_ref,
                vmask_ref, smask_ref, o_ref, st_ref):
    c = GLA_CHUNK

    @pl.when(pl.program_id(0) == 0)
    def _():
        st_ref[...] = jnp.zeros_like(st_ref)

    m_all = mall_ref[...]
    wmask = wmask_ref[...]
    vmask = vmask_ref[...]
    smask = smask_ref[...]
    gout = gout_ref[...]

    def chunk(ci, carry):
        r0 = pl.multiple_of(ci * c, c)
        rows = pl.ds(r0, c)
        q = q_ref[rows, :]
        k = k_ref[rows, :]
        v = v_ref[rows, :]
        la = la_ref[rows, :]
        hi = la.astype(BF16)
        r1 = la - hi.astype(F32)
        mid = r1.astype(BF16)
        lo = (r1 - mid.astype(F32)).astype(BF16)
        dall = _dot(m_all, hi) + _dot(m_all, mid) + _dot(m_all, lo)
        b = dall[0:c]
        b_last = b[c - 1:c, :]
        qhat = q * jnp.exp(b)
        khat = k * jnp.exp(b_last - b)

        a = jnp.zeros((c, GLA_HEADS * c), F32)
        for lv in range(GLA_LEVELS):
            if lv == 0:
                qt, kt = q, k
            else:
                d = dall[lv * c:(lv + 1) * c]
                qt = q * jnp.exp(jnp.minimum(d, 0.0))
                kt = k * jnp.exp(jnp.minimum(-d, 0.0))
            w = (jnp.concatenate([kt] * GLA_HEADS, axis=0) * wmask).astype(BF16)
            a = a + lvl_ref[lv] * _dot_nt(qt.astype(BF16), w)

        st = st_ref[...]
        vbd = (jnp.concatenate([v] * GLA_HEADS, axis=0) * vmask).astype(BF16)
        o = _dot(a.astype(BF16), vbd) + _dot_nt(qhat.astype(BF16), st.astype(BF16))
        st_ref[...] = st * jnp.exp(b_last) + smask * _dot_tn(v.astype(BF16), khat.astype(BF16))

        parts = []
        for hh in range(GLA_HEADS):
            oh = o[:, hh * GLA_DV:(hh + 1) * GLA_DV]
            ms = jnp.mean(oh * oh, axis=-1, keepdims=True)
            parts.append(oh * lax.rsqrt(ms + EPS))
        gg = gg_ref[rows, :]
        o_ref[rows, :] = jnp.concatenate(parts, axis=1) * gout * (gg * _sigmoid(gg))
        return carry

    lax.fori_loop(0, q_ref.shape[0] // c, chunk, 0)


def _gla(q, k, v, gg, la, gout):
    s = q.shape[0]
    tb = GLA_ROWS
    consts = _gla_constants()
    row = lambda n: pl.BlockSpec((tb, n), lambda i: (i, 0))
    return pl.pallas_call(
        _gla_kernel,
        grid=(s // tb,),
        in_specs=[row(GLA_QK), row(GLA_QK), row(GLA_V), row(GLA_V), row(GLA_QK), _full(gout.shape)]
                 + [_full(a.shape) for a in consts],
        out_specs=row(GLA_V),
        out_shape=jax.ShapeDtypeStruct((s, GLA_V), F32),
        scratch_shapes=[pltpu.VMEM((GLA_V, GLA_QK), F32)],
        compiler_params=_params(("arbitrary",)),
        name="gla",
    )(q, k, v, gg, la, gout, *consts)


DSA_SLAB = 32
WORD = 32
DSA_GROUP = WORD * LANES


def _float_to_ordered_bits(x):
    bits = lax.bitcast_convert_type(x, I32)
    key = bits ^ ((bits >> 31) & jnp.int32(0x7FFFFFFF))
    return key ^ jnp.int32(-2147483648)


def _transpose_bits(a):
    a = list(a)
    j, msk = 16, 0x0000FFFF
    while j:
        m32 = jnp.int32(np.array(msk, np.uint32).view(np.int32))
        k = 0
        while k < WORD:
            t = (a[k] ^ (a[k + j] >> j)) & m32
            a[k] = a[k] ^ t
            a[k + j] = a[k + j] ^ (t << j)
            k = (k + j + 1) & ~j
        j >>= 1
        msk = (msk ^ (msk << j)) & 0xFFFFFFFF
    return a


def _dsa_kernel(qi_ref, kit_ref, wi_ref, qb_ref, kt_ref, v_ref, o_ref,
                key_ref, alive_ref, great_ref, wb_ref, qbd_ref, s_ref, p_ref, bias_ref, m_ref, l_ref,
                alpha_ref, acc_ref, *, seq, topk):
    tq = DSA_TQ
    ch = DSA_CH
    hg = 4
    gw = hg * DSA_HD
    bpc = ch // LANES
    cpg = DSA_GROUP // ch
    n_groups = seq // DSA_GROUP
    i = pl.program_id(0)
    n_ch = ((i + 1) * tq + ch - 1) // ch
    lane_i = lax.broadcasted_iota(I32, (tq, LANES), 1)
    row_pos = lax.broadcasted_iota(I32, (tq, 1), 0) + i * tq
    kf = float(topk)

    def tile_l(a, width):
        return jnp.concatenate([a] * (width // a.shape[1]), axis=1)

    def chunk(c):
        return pl.ds(pl.multiple_of(c * ch, ch), ch)

    def lane_sum(a):
        return jnp.sum(a.astype(F32), axis=1, keepdims=True)

    @pl.when(i == 0)
    def _():
        key_ref[...] = jnp.zeros(key_ref.shape, I32)

    wv = wi_ref[...]
    for h in range(IDX_HEADS):
        wb_ref[h] = jnp.broadcast_to(wv[:, IDX_DIM + h:IDX_DIM + h + 1], (tq, LANES))
    qi = qi_ref[0]

    def score_body(c, carry):
        lg = _dot(qi, kit_ref[:, chunk(c)])
        sc = jnp.maximum(lg[0:tq], 0.0) * tile_l(wb_ref[0], ch)
        for h in range(1, IDX_HEADS):
            sc = sc + jnp.maximum(lg[h * tq:(h + 1) * tq], 0.0) * tile_l(wb_ref[h], ch)
        sc = jnp.where(sc == 0.0, 0.0, sc)
        key_ref[c] = _float_to_ordered_bits(sc)
        return carry

    lax.fori_loop(0, n_ch, score_body, 0)

    n_pg = (n_ch + cpg - 1) // cpg

    def plane_body(t, carry):
        g = t // (tq // 8)
        rows = pl.ds(pl.multiple_of((t % (tq // 8)) * 8, 8), 8)

        def slot(b):
            return (g * cpg + b // bpc, rows, slice((b % bpc) * LANES, (b % bpc + 1) * LANES))

        planes = _transpose_bits([key_ref[slot(WORD - 1 - k)] for k in range(WORD)])
        for b in range(WORD):
            key_ref[slot(b)] = planes[b]
        return carry

    lax.fori_loop(0, n_pg * (tq // 8), plane_body, 0)

    def index_mask(g, bound):
        r = bound - g * DSA_GROUP - lane_i
        q = jnp.clip((r + (LANES - 1)) >> 7, 0, WORD)
        return jnp.where(q >= WORD, jnp.int32(-1), (jnp.int32(1) << jnp.minimum(q, WORD - 1)) - 1)

    for g in range(n_groups):
        alive_ref[g] = index_mask(g, row_pos + 1)
        great_ref[g] = jnp.zeros((tq, LANES), I32)

    def bit_body(p, cnt_great):
        pc = p // bpc
        pl0 = pl.multiple_of((p % bpc) * LANES, LANES)
        ones = []
        acc = jnp.zeros((tq, LANES), I32)
        for g in range(n_groups):
            x = alive_ref[g] & key_ref[g * cpg + pc, :, pl.ds(pl0, LANES)]
            acc = acc + lax.population_count(x)
            ones.append(x)
        cnt_one = lane_sum(acc)
        take = (cnt_great + cnt_one) >= kf
        take_b = jnp.broadcast_to(take, (tq, LANES))
        for g in range(n_groups):
            a = alive_ref[g]
            alive_ref[g] = jnp.where(take_b, ones[g], a ^ ones[g])
            great_ref[g] = jnp.where(take_b, great_ref[g], great_ref[g] | ones[g])
        return jnp.where(take, cnt_great, cnt_great + cnt_one)

    cnt_great = lax.fori_loop(0, WORD, bit_body, jnp.zeros((tq, 1), F32))
    need = kf - cnt_great

    def count_alive(bound):
        acc = jnp.zeros((tq, LANES), I32)
        for g in range(n_groups):
            acc = acc + lax.population_count(alive_ref[g] & index_mask(g, bound))
        return lane_sum(acc)

    tie_row = count_alive(jnp.full((tq, 1), seq, I32)) > need

    def tie_break(_):
        nbits = max(1, int(np.ceil(np.log2(seq))))

        def jbit(p, m):
            cand = m | jnp.left_shift(jnp.int32(1), nbits - 1 - p)
            return jnp.where(count_alive(cand) < need, cand, m)

        m = lax.fori_loop(0, nbits, jbit, jnp.zeros((tq, 1), I32))
        return jnp.where(tie_row, m + 1, jnp.int32(seq))

    any_tie = jnp.max(jnp.where(tie_row, 1.0, 0.0)) > 0.0
    bound = lax.cond(any_tie, tie_break, lambda _: jnp.full((tq, 1), seq, I32), 0)
    for g in range(n_groups):
        great_ref[g] = great_ref[g] | (alive_ref[g] & index_mask(g, bound))

    qt = qb_ref[...]
    head_of_lane = lax.broadcasted_iota(I32, (tq, gw), 1) >> 6
    for g in range(2):
        qg = qt[:, g * gw:(g + 1) * gw]
        for h in range(hg):
            qbd_ref[g, h * tq:(h + 1) * tq, :] = jnp.where(head_of_lane == h, qg, jnp.zeros_like(qg))
    m_ref[...] = jnp.full(m_ref.shape, MASKED, F32)
    l_ref[...] = jnp.zeros(l_ref.shape, F32)
    acc_ref[...] = jnp.zeros(acc_ref.shape, F32)

    def attn_body(c, carry):
        sel = great_ref[c // cpg]
        for jj in range(bpc):
            bit = (sel >> ((c % cpg) * bpc + jj)) & 1
            bias_ref[:, jj * LANES:(jj + 1) * LANES] = jnp.where(bit != 0, 0.0, MASKED)
        for g in range(2):
            s_ref[g] = _dot(qbd_ref[g], kt_ref[g * gw:(g + 1) * gw, chunk(c)])
        for g in range(2):
            for r in range(hg * tq // DSA_SLAB):
                rows = slice(r * DSA_SLAB, (r + 1) * DSA_SLAB)
                b0 = (r * DSA_SLAB) % tq
                s = s_ref[g, rows, :] + bias_ref[b0:b0 + DSA_SLAB, :]
                m_prev = m_ref[g, rows, :]
                m_new = jnp.maximum(m_prev, jnp.max(s, axis=1, keepdims=True))
                alpha = jnp.exp2(m_prev - m_new)
                p = jnp.exp2(s - tile_l(m_new, ch))
                l_ref[g, rows, :] = alpha * l_ref[g, rows, :] + jnp.sum(p, axis=1, keepdims=True)
                m_ref[g, rows, :] = m_new
                alpha_ref[g, rows, :] = alpha
                p_ref[g, rows, :] = p.astype(BF16)
            acc_ref[g] = (acc_ref[g] * tile_l(alpha_ref[g], gw)
                          + _dot(p_ref[g], v_ref[chunk(c), g * gw:(g + 1) * gw]))
        return carry

    lax.fori_loop(0, n_ch, attn_body, 0)

    outs = []
    for g in range(2):
        a = acc_ref[g] * tile_l(1.0 / l_ref[g], gw)
        og = jnp.zeros((tq, gw), F32)
        for h in range(hg):
            og = og + jnp.where(head_of_lane == h, a[h * tq:(h + 1) * tq], 0.0)
        outs.append(og)
    o_ref[...] = jnp.concatenate(outs, axis=1)


def _dsa(qi_r, kit, wi, qb, kt, v, topk):
    s = qb.shape[0]
    tq = DSA_TQ
    assert s % DSA_GROUP == 0 and DSA_GROUP % DSA_CH == 0
    rows4 = 4 * tq
    return pl.pallas_call(
        functools.partial(_dsa_kernel, seq=s, topk=topk),
        grid=(s // tq,),
        in_specs=[pl.BlockSpec((1, IDX_HEADS * tq, IDX_DIM), lambda i: (i, 0, 0)),
                  _resident(kit.shape),
                  pl.BlockSpec((tq, LANES), lambda i: (i, 0)),
                  pl.BlockSpec((tq, DSA_W), lambda i: (i, 0)),
                  _resident(kt.shape),
                  _resident(v.shape)],
        out_specs=pl.BlockSpec((tq, DSA_W), lambda i: (i, 0)),
        out_shape=jax.ShapeDtypeStruct((s, DSA_W), F32),
        scratch_shapes=[pltpu.VMEM((s // DSA_CH, tq, DSA_CH), I32),
                        pltpu.VMEM((s // DSA_GROUP, tq, LANES), I32),
                        pltpu.VMEM((s // DSA_GROUP, tq, LANES), I32),
                        pltpu.VMEM((IDX_HEADS, tq, LANES), F32),
                        pltpu.VMEM((2, rows4, 4 * DSA_HD), BF16),
                        pltpu.VMEM((2, rows4, DSA_CH), F32),
                        pltpu.VMEM((2, rows4, DSA_CH), BF16),
                        pltpu.VMEM((tq, DSA_CH), F32),
                        pltpu.VMEM((2, rows4, LANES), F32),
                        pltpu.VMEM((2, rows4, LANES), F32),
                        pltpu.VMEM((2, rows4, LANES), F32),
                        pltpu.VMEM((2, rows4, 4 * DSA_HD), F32)],
        compiler_params=_params(("arbitrary",)),
        name="dsa",
    )(qi_r, kit, wi, qb, kt, v)


def _merge_kernel(x_ref, mod_ref, gpre_ref, gpost_ref, oa_ref, ob_ref, wbg_ref, wpg_ref, wpd_ref,
                  wout_ref, o_ref):
    x = x_ref[...]
    h = _modulated(x, gpre_ref[...], mod_ref, 0, 1)
    gates = _sigmoid(_dot(h.astype(BF16), wbg_ref[...]))
    yg = _dot(oa_ref[...].astype(BF16), wpg_ref[...])
    yd = _dot(ob_ref[...].astype(BF16), wpd_ref[...])
    mix = gates[:, 0:D_MODEL] * yg + gates[:, D_MODEL:] * yd
    out = _dot(mix.astype(BF16), wout_ref[...])
    o_ref[...] = x + mod_ref[2:3, :] * _rms(out, gpost_ref[...])


def _merge(x, mod, gpre, gpost, oa, ob, wbg, wpg, wpd, wout):
    s = x.shape[0]
    tm = PROJ_ROWS
    row = lambda n: pl.BlockSpec((tm, n), lambda i: (i, 0))
    return pl.pallas_call(
        _merge_kernel,
        grid=(s // tm,),
        in_specs=[row(D_MODEL), _full(mod.shape), _full(gpre.shape), _full(gpost.shape),
                  row(GLA_V), row(DSA_W), _full(wbg.shape), _full(wpg.shape), _full(wpd.shape),
                  _full(wout.shape)],
        out_specs=row(D_MODEL),
        out_shape=jax.ShapeDtypeStruct((s, D_MODEL), F32),
        compiler_params=_params(("arbitrary",)),
        name="merge",
    )(x, mod, gpre, gpost, oa, ob, wbg, wpg, wpd, wout)


def _router(lg):
    t = lg.shape[0]
    lane = lax.broadcasted_iota(I32, (t, LANES), 1)
    lanef = lane.astype(F32)
    big = 1e9
    gm = lane < N_GROUPS
    gmax = jnp.max(jnp.where(gm, lg, -jnp.inf), axis=1, keepdims=True)
    gsum = jnp.sum(jnp.where(gm, jnp.exp(lg - gmax), 0.0), axis=1, keepdims=True)
    p_g = 1.0 / gsum
    g_sel = jnp.min(jnp.where(gm & (lg == gmax), lanef, big), axis=1, keepdims=True)
    lo = N_GROUPS + EXPERTS_PER_GROUP * g_sel
    em = (lanef >= lo) & (lanef < lo + EXPERTS_PER_GROUP)
    m1 = jnp.max(jnp.where(em, lg, -jnp.inf), axis=1, keepdims=True)
    i1 = jnp.min(jnp.where(em & (lg == m1), lanef, big), axis=1, keepdims=True)
    em2 = em & (lanef != i1)
    m2 = jnp.max(jnp.where(em2, lg, -jnp.inf), axis=1, keepdims=True)
    i2 = jnp.min(jnp.where(em2 & (lg == m2), lanef, big), axis=1, keepdims=True)
    e2 = jnp.exp(m2 - m1)
    inv = 1.0 / (1.0 + e2)
    return (jnp.where(lanef == i1, p_g * inv, 0.0) + jnp.where(lanef == i2, p_g * (e2 * inv), 0.0))


def _moe_kernel(x_ref, mod_ref, gpre_ref, gpost_ref, wr_ref, br_ref, wg_ref, wu_ref, wd_ref,
                o_ref, hb_ref, comb_ref, acc_ref):
    j = pl.program_id(1)
    tm = x_ref.shape[0]

    @pl.when(j == 0)
    def _():
        h = _modulated(x_ref[...], gpre_ref[...], mod_ref, 3, 4)
        hb = h.astype(BF16)
        hb_ref[...] = hb
        comb_ref[...] = _router(_dot(hb, wr_ref[...]) + br_ref[...])
        acc_ref[...] = jnp.zeros_like(acc_ref)

    hb = hb_ref[...]
    hgate = _dot(hb, wg_ref[...])
    hup = _dot(hb, wu_ref[...])
    act = hgate * _sigmoid(hgate) * hup
    comb = comb_ref[...]
    lane = lax.broadcasted_iota(I32, (tm, LANES), 1)
    parts = []
    for e in range(MOE_EB):
        sel = lane == (N_GROUPS + j * MOE_EB + e)
        cw = jnp.sum(jnp.where(sel, comb, 0.0), axis=1, keepdims=True)
        parts.append((act[:, e * D_EXPERT:(e + 1) * D_EXPERT] * cw).astype(BF16))
    acc_ref[...] += _dot(jnp.concatenate(parts, axis=1), wd_ref[...])

    @pl.when(j == pl.num_programs(1) - 1)
    def _():
        o_ref[...] = x_ref[...] + mod_ref[5:6, :] * _rms(acc_ref[...], gpost_ref[...])


def _moe(x, mod, gpre, gpost, wr, br, wg, wu, wd):
    s = x.shape[0]
    tm = MOE_ROWS
    bw = MOE_EB * D_EXPERT
    return pl.pallas_call(
        _moe_kernel,
        grid=(s // tm, N_EXPERTS // MOE_EB),
        in_specs=[pl.BlockSpec((tm, D_MODEL), lambda i, j: (i, 0)),
                  _full(mod.shape), _full(gpre.shape), _full(gpost.shape), _full(wr.shape),
                  _full(br.shape),
                  pl.BlockSpec((D_MODEL, bw), lambda i, j: (0, j)),
                  pl.BlockSpec((D_MODEL, bw), lambda i, j: (0, j)),
                  pl.BlockSpec((bw, D_MODEL), lambda i, j: (j, 0))],
        out_specs=pl.BlockSpec((tm, D_MODEL), lambda i, j: (i, 0)),
        out_shape=jax.ShapeDtypeStruct((s, D_MODEL), F32),
        scratch_shapes=[pltpu.VMEM((tm, D_MODEL), BF16), pltpu.VMEM((tm, LANES), F32),
                        pltpu.VMEM((tm, D_MODEL), F32)],
        compiler_params=_params(("arbitrary", "arbitrary")),
        name="moe",
    )(x, mod, gpre, gpost, wr, br, wg, wu, wd)


def _rope_lane_constants():
    rot = DSA_HD // ROT_FRAC
    half = rot // 2
    freqs = np.power(np.float32(ROPE_THETA), -np.arange(half, dtype=np.float32) * np.float32(2.0) / rot)
    j = np.arange(LANES) % DSA_HD
    fv = np.where(j < rot, freqs[j % half], 0.0).astype(np.float32)
    sg = np.where(j < half, -1.0, np.where(j < rot, 1.0, 0.0)).astype(np.float32)
    return jnp.asarray(fv)[None, :], jnp.asarray(sg)[None, :]


def _pad_cols(w, n):
    return jnp.pad(w, ((0, 0), (0, n - w.shape[1])))


def _layer(x, c, pos, w_ada, b_ada, g_pre_mix, g_post_mix, g_pre_ffn, g_post_ffn, w_in, w_gla_a2,
           b_gla_a, g_gla_out, g_idx_k, w_proj_gla, w_proj_dsa, w_out, w_router_g, b_router_g,
           w_router_e, b_router_e, w_e_gate, w_e_up, w_e_down):
    s = x.shape[0]
    mod = _ada(jnp.broadcast_to(c, (8, D_MODEL)), w_ada, b_ada[None, :])[0].reshape(N_MOD, D_MODEL)

    o = np.cumsum((GLA_QK, GLA_QK, GLA_V, GLA_V, GLA_GATE_RANK, DSA_W, DSA_W, DSA_W, IDX_Q, IDX_DIM,
                   IDX_HEADS, 2 * D_MODEL))
    wb = w_in.astype(BF16)
    w_gla = wb[:, 0:o[3]]
    w_ga = _pad_cols(wb[:, o[3]:o[4]], LANES)
    w_dsa = wb[:, o[4]:o[8]]
    w_sm = _pad_cols(jnp.concatenate([wb[:, o[8]:o[9]], wb[:, o[9]:o[10]]], axis=1), LANES)
    w_bg = wb[:, o[10]:o[11]]
    w_a2 = jnp.pad(w_gla_a2.astype(BF16), ((0, LANES - GLA_GATE_RANK), (0, 0)))

    q_a, k_a, v_a, gg, la = _gla_proj(x, mod, g_pre_mix[None, :], w_gla, w_ga, w_a2, b_gla_a[None, :])
    o_a = _gla(q_a, k_a, v_a, gg, la, jnp.tile(g_gla_out, GLA_HEADS)[None, :])

    fv, sg = _rope_lane_constants()
    gik = jnp.pad(g_idx_k, (0, LANES - IDX_DIM))[None, :]
    q_b, k_b, v_b, qi, ki, wi = _dsa_proj(x, mod, g_pre_mix[None, :], pos.astype(F32)[:, None], fv, sg,
                                          w_dsa, w_sm, gik)
    nqb = s // DSA_TQ
    qi_r = qi.reshape(nqb, DSA_TQ, IDX_HEADS, IDX_DIM).transpose(0, 2, 1, 3).reshape(
        nqb, IDX_HEADS * DSA_TQ, IDX_DIM)
    o_b = _dsa(qi_r, ki.T, wi, q_b, k_b.T, v_b, min(DSA_TOPK_MAX, s // 4))

    x1 = _merge(x, mod, g_pre_mix[None, :], g_post_mix[None, :], o_a, o_b, w_bg,
                w_proj_gla.astype(BF16), w_proj_dsa.astype(BF16), w_out.astype(BF16))

    wr = _pad_cols(jnp.concatenate([w_router_g, w_router_e], axis=1).astype(BF16), LANES)
    br = jnp.pad(jnp.concatenate([b_router_g, b_router_e]), (0, LANES - N_GROUPS - N_EXPERTS))[None, :]
    wg = w_e_gate.astype(BF16).transpose(1, 0, 2).reshape(D_MODEL, N_EXPERTS * D_EXPERT)
    wu = w_e_up.astype(BF16).transpose(1, 0, 2).reshape(D_MODEL, N_EXPERTS * D_EXPERT)
    wd = w_e_down.astype(BF16).reshape(N_EXPERTS * D_EXPERT, D_MODEL)
    return _moe(x1, mod, g_pre_ffn[None, :], g_post_ffn[None, :], wr, br, wg, wu, wd)


def kernel(x, c, positions, w_ada, b_ada, g_pre_mix, g_post_mix, g_pre_ffn, g_post_ffn, w_in, w_gla_a2,
           b_gla_a, g_gla_out, g_idx_k, w_proj_gla, w_proj_dsa, w_out, w_router_g, b_router_g,
           w_router_e, b_router_e, w_e_gate, w_e_up, w_e_down):
    batch, depth = x.shape[0], w_ada.shape[0]
    outs = []
    for bi in range(batch):
        xb = x[bi]
        for l in range(depth):
            xb = _layer(xb, c[bi:bi + 1], positions[bi], w_ada[l], b_ada[l], g_pre_mix[l], g_post_mix[l],
                        g_pre_ffn[l], g_post_ffn[l], w_in[l], w_gla_a2[l], b_gla_a[l], g_gla_out[l],
                        g_idx_k[l], w_proj_gla[l], w_proj_dsa[l], w_out[l], w_router_g[l],
                        b_router_g[l], w_router_e[l], b_router_e[l], w_e_gate[l], w_e_up[l],
                        w_e_down[l])
        outs.append(xb)
    return jnp.stack(outs, axis=0)
```

```python
import functools

import numpy as np
import jax
import jax.numpy as jnp
from jax import lax
from jax.experimental import pallas as pl
from jax.experimental.pallas import tpu as pltpu

F32 = jnp.float32
BF16 = jnp.bfloat16
I32 = jnp.int32

D_MODEL = 1024
EPS = 1e-6
ROPE_THETA = 500000.0
ROT_FRAC = 4
GLA_HEADS = 4
GLA_DK = 64
GLA_DV = 128
GLA_GATE_RANK = 16
GLA_TAU = 16.0
GLA_CHUNK = 64
DSA_HEADS = 8
DSA_HD = 64
IDX_HEADS = 8
IDX_DIM = 64
DSA_TOPK_MAX = 256
N_GROUPS = 4
EXPERTS_PER_GROUP = 8
N_EXPERTS = N_GROUPS * EXPERTS_PER_GROUP
D_EXPERT = D_MODEL // 4
N_MOD = 6

GLA_QK = GLA_HEADS * GLA_DK
GLA_V = GLA_HEADS * GLA_DV
DSA_W = DSA_HEADS * DSA_HD
IDX_Q = IDX_HEADS * IDX_DIM

LANES = 128
VMEM_LIMIT_BYTES = 56 * 1024 * 1024

PROJ_ROWS = 512
GLA_ROWS = 512
DSA_TQ = 128
DSA_CH = 1024
MOE_ROWS = 512
MOE_EB = 8
MASKED = -1e30
LOG2E = float(np.log2(np.e))
GLA_LEVELS = 7


def _dot(a, b):
    return jnp.dot(a, b, preferred_element_type=F32)


def _dot_nt(a, b):
    return lax.dot_general(a, b, (((1,), (1,)), ((), ())), preferred_element_type=F32)


def _dot_tn(a, b):
    return lax.dot_general(a, b, (((0,), (0,)), ((), ())), preferred_element_type=F32)


def _sigmoid(x):
    return 1.0 / (1.0 + jnp.exp(-x))


def _rms(x, g):
    ms = jnp.mean(x * x, axis=-1, keepdims=True)
    return x * lax.rsqrt(ms + EPS) * g


def _modulated(x, g, mod_ref, shift_row, scale_row):
    return (_rms(x, g) * (1.0 + mod_ref[scale_row:scale_row + 1, :])
            + mod_ref[shift_row:shift_row + 1, :])


def _params(sem):
    return pltpu.CompilerParams(dimension_semantics=sem, vmem_limit_bytes=VMEM_LIMIT_BYTES)


def _full(shape):
    return pl.BlockSpec(shape, lambda *_: (0,) * len(shape))


def _resident(shape):
    return pl.BlockSpec(shape, lambda *_: (0,) * len(shape), pipeline_mode=pl.Buffered(1))


def _ada_kernel(c_ref, w_ref, b_ref, o_ref):
    c = c_ref[...]
    a = c * _sigmoid(c)
    o_ref[...] = _dot(a.astype(BF16), w_ref[...].astype(BF16)) + b_ref[...]


def _ada(c8, w, b):
    n = w.shape[1]
    bn = 1536
    return pl.pallas_call(
        _ada_kernel,
        grid=(n // bn,),
        in_specs=[_full(c8.shape),
                  pl.BlockSpec((D_MODEL, bn), lambda j: (0, j)),
                  pl.BlockSpec((1, bn), lambda j: (0, j))],
        out_specs=pl.BlockSpec((8, bn), lambda j: (0, j)),
        out_shape=jax.ShapeDtypeStruct((8, n), F32),
        compiler_params=_params(("arbitrary",)),
        name="ada",
    )(c8, w, b)


def _gla_proj_kernel(x_ref, mod_ref, g_ref, w_ref, wga_ref, wa2_ref, ba_ref,
                     q_ref, k_ref, v_ref, gg_ref, la_ref):
    h = _modulated(x_ref[...], g_ref[...], mod_ref, 0, 1)
    hb = h.astype(BF16)
    p = _dot(hb, w_ref[...])
    q_ref[...] = p[:, 0:GLA_QK] * (GLA_DK ** -0.5)
    k_ref[...] = p[:, GLA_QK:2 * GLA_QK]
    v_ref[...] = p[:, 2 * GLA_QK:2 * GLA_QK + GLA_V]
    gg_ref[...] = p[:, 2 * GLA_QK + GLA_V:]
    ga = _dot(hb, wga_ref[...])
    z = _dot(ga.astype(BF16), wa2_ref[...]) + ba_ref[...]
    log_sig = jnp.minimum(z, 0.0) - jnp.log1p(jnp.exp(-jnp.abs(z)))
    la_ref[...] = log_sig * (1.0 / GLA_TAU)


def _gla_proj(x, mod, g, w, wga, wa2, ba):
    s = x.shape[0]
    tm = PROJ_ROWS
    row = lambda n: pl.BlockSpec((tm, n), lambda i: (i, 0))
    return pl.pallas_call(
        _gla_proj_kernel,
        grid=(s // tm,),
        in_specs=[row(D_MODEL), _full(mod.shape), _full(g.shape), _full(w.shape),
                  _full(wga.shape), _full(wa2.shape), _full(ba.shape)],
        out_specs=[row(GLA_QK), row(GLA_QK), row(GLA_V), row(GLA_V), row(GLA_QK)],
        out_shape=[jax.ShapeDtypeStruct((s, GLA_QK), F32), jax.ShapeDtypeStruct((s, GLA_QK), F32),
                   jax.ShapeDtypeStruct((s, GLA_V), F32), jax.ShapeDtypeStruct((s, GLA_V), F32),
                   jax.ShapeDtypeStruct((s, GLA_QK), F32)],
        compiler_params=_params(("arbitrary",)),
        name="gla_proj",
    )(x, mod, g, w, wga, wa2, ba)


def _dsa_proj_kernel(x_ref, mod_ref, g_ref, pos_ref, fv_ref, sg_ref, w_ref, wsm_ref, gik_ref,
                     qb_ref, kb_ref, vb_ref, qi_ref, ki_ref, wi_ref):
    tm = x_ref.shape[0]
    h = _modulated(x_ref[...], g_ref[...], mod_ref, 0, 1)
    hb = h.astype(BF16)
    ang = pos_ref[...] * fv_ref[...]
    cs = jnp.cos(ang)
    sn = jnp.sin(ang) * sg_ref[...]
    lane = lax.broadcasted_iota(I32, (tm, LANES), 1)
    first = (lane & (DSA_HD - 1)) < (DSA_HD // ROT_FRAC // 2)

    def rope(t):
        width = t.shape[1]
        rep = width // LANES
        tile = (lambda a: jnp.concatenate([a] * rep, axis=1)) if rep > 1 else (lambda a: a)
        half = DSA_HD // ROT_FRAC // 2
        fwd = pltpu.roll(t, half, 1)
        bwd = pltpu.roll(t, width - half, 1)
        partner = jnp.where(tile(first), bwd, fwd)
        return t * tile(cs) + partner * tile(sn)

    p = _dot(hb, w_ref[...])
    qb_ref[...] = (rope(p[:, 0:DSA_W]) * (DSA_HD ** -0.5 * LOG2E)).astype(BF16)
    kb_ref[...] = rope(p[:, DSA_W:2 * DSA_W]).astype(BF16)
    vb_ref[...] = p[:, 2 * DSA_W:3 * DSA_W].astype(BF16)
    qi_ref[...] = (rope(p[:, 3 * DSA_W:]) * (IDX_DIM ** -0.5)).astype(BF16)

    sm = _dot(hb, wsm_ref[...])
    is_ik = lane < IDX_DIM
    mu = jnp.sum(jnp.where(is_ik, sm, 0.0), axis=-1, keepdims=True) * (1.0 / IDX_DIM)
    xc = jnp.where(is_ik, sm - mu, 0.0)
    var = jnp.sum(xc * xc, axis=-1, keepdims=True) * (1.0 / IDX_DIM)
    y = xc * lax.rsqrt(var + EPS) * gik_ref[...]
    ki_ref[...] = rope(y)[:, 0:IDX_DIM].astype(BF16)
    wi_ref[...] = sm * (IDX_HEADS ** -0.5)


def _dsa_proj(x, mod, g, pos, fv, sg, w, wsm, gik):
    s = x.shape[0]
    tm = PROJ_ROWS
    row = lambda n: pl.BlockSpec((tm, n), lambda i: (i, 0))
    return pl.pallas_call(
        _dsa_proj_kernel,
        grid=(s // tm,),
        in_specs=[row(D_MODEL), _full(mod.shape), _full(g.shape), row(1), _full(fv.shape),
                  _full(sg.shape), _full(w.shape), _full(wsm.shape), _full(gik.shape)],
        out_specs=[row(DSA_W), row(DSA_W), row(DSA_W), row(IDX_Q), row(IDX_DIM), row(LANES)],
        out_shape=[jax.ShapeDtypeStruct((s, DSA_W), BF16), jax.ShapeDtypeStruct((s, DSA_W), BF16),
                   jax.ShapeDtypeStruct((s, DSA_W), BF16), jax.ShapeDtypeStruct((s, IDX_Q), BF16),
                   jax.ShapeDtypeStruct((s, IDX_DIM), BF16), jax.ShapeDtypeStruct((s, LANES), F32)],
        compiler_params=_params(("arbitrary",)),
        name="dsa_proj",
    )(x, mod, g, pos, fv, sg, w, wsm, gik)


def _gla_constants():
    c = GLA_CHUNK
    tril = np.tril(np.ones((c, c), np.float32))
    t = np.arange(c)
    mats = [tril]
    masks = [(t[:, None] == t[None, :])]
    for hs in (32, 16, 8, 4, 2, 1):
        blk = 2 * hs
        r = (t // blk) * blk + hs - 1
        mats.append(tril - tril[r, :])
        same = (t[:, None] // blk) == (t[None, :] // blk)
        masks.append(same & ((t[:, None] % blk) >= hs) & ((t[None, :] % blk) < hs))
    m_all = np.concatenate(mats, axis=0)
    lvl = np.stack([np.tile(m.astype(np.float32), (1, GLA_HEADS)) for m in masks])
    hrow = np.arange(GLA_HEADS * c) // c
    wmask = (hrow[:, None] == (np.arange(GLA_QK) // GLA_DK)[None, :]).astype(np.float32)
    vmask = (hrow[:, None] == (np.arange(GLA_V) // GLA_DV)[None, :]).astype(np.float32)
    smask = ((np.arange(GLA_V) // GLA_DV)[:, None] == (np.arange(GLA_QK) // GLA_DK)[None, :])
    return (jnp.asarray(m_all, BF16), jnp.asarray(lvl), jnp.asarray(wmask), jnp.asarray(vmask),
            jnp.asarray(smask.astype(np.float32)))


def _gla_kernel(q_ref, k_ref, v_ref, gg_ref, la_ref, gout_ref, mall_ref, lvl_ref, wmask_ref,
                vmask_ref, smask_ref, o_ref, st_ref):
    c = GLA_CHUNK

    @pl.when(pl.program_id(0) == 0)
    def _():
        st_ref[...] = jnp.zeros_like(st_ref)

    m_all = mall_ref[...]
    wmask = wmask_ref[...]
    vmask = vmask_ref[...]
    smask = smask_ref[...]
    gout = gout_ref[...]

    def chunk(ci, carry):
        r0 = pl.multiple_of(ci * c, c)
        rows = pl.ds(r0, c)
        q = q_ref[rows, :]
        k = k_ref[rows, :]
        v = v_ref[rows, :]
        la = la_ref[rows, :]
        hi = la.astype(BF16)
        r1 = la - hi.astype(F32)
        mid = r1.astype(BF16)
        lo = (r1 - mid.astype(F32)).astype(BF16)
        dall = _dot(m_all, hi) + _dot(m_all, mid) + _dot(m_all, lo)
        b = dall[0:c]
        b_last = b[c - 1:c, :]
        qhat = q * jnp.exp(b)
        khat = k * jnp.exp(b_last - b)

        a = jnp.zeros((c, GLA_HEADS * c), F32)
        for lv in range(GLA_LEVELS):
            if lv == 0:
                qt, kt = q, k
            else:
                d = dall[lv * c:(lv + 1) * c]
                qt = q * jnp.exp(jnp.minimum(d, 0.0))
                kt = k * jnp.exp(jnp.minimum(-d, 0.0))
            w = (jnp.concatenate([kt] * GLA_HEADS, axis=0) * wmask).astype(BF16)
            a = a + lvl_ref[lv] * _dot_nt(qt.astype(BF16), w)

        st = st_ref[...]
        vbd = (jnp.concatenate([v] * GLA_HEADS, axis=0) * vmask).astype(BF16)
        o = _dot(a.astype(BF16), vbd) + _dot_nt(qhat.astype(BF16), st.astype(BF16))
        st_ref[...] = st * jnp.exp(b_last) + smask * _dot_tn(v.astype(BF16), khat.astype(BF16))

        parts = []
        for hh in range(GLA_HEADS):
            oh = o[:, hh * GLA_DV:(hh + 1) * GLA_DV]
            ms = jnp.mean(oh * oh, axis=-1, keepdims=True)
            parts.append(oh * lax.rsqrt(ms + EPS))
        gg = gg_ref[rows, :]
        o_ref[rows, :] = jnp.concatenate(parts, axis=1) * gout * (gg * _sigmoid(gg))
        return carry

    lax.fori_loop(0, q_ref.shape[0] // c, chunk, 0, unroll=4)


def _gla(q, k, v, gg, la, gout):
    s = q.shape[0]
    tb = GLA_ROWS
    consts = _gla_constants()
    row = lambda n: pl.BlockSpec((tb, n), lambda i: (i, 0))
    return pl.pallas_call(
        _gla_kernel,
        grid=(s // tb,),
        in_specs=[row(GLA_QK), row(GLA_QK), row(GLA_V), row(GLA_V), row(GLA_QK), _full(gout.shape)]
                 + [_full(a.shape) for a in consts],
        out_specs=row(GLA_V),
        out_shape=jax.ShapeDtypeStruct((s, GLA_V), F32),
        scratch_shapes=[pltpu.VMEM((GLA_V, GLA_QK), F32)],
        compiler_params=_params(("arbitrary",)),
        name="gla",
    )(q, k, v, gg, la, gout, *consts)


DSA_SLAB = 32
WORD = 32
DSA_GROUP = WORD * LANES


def _float_to_ordered_bits(x):
    bits = lax.bitcast_convert_type(x, I32)
    key = bits ^ ((bits >> 31) & jnp.int32(0x7FFFFFFF))
    return key ^ jnp.int32(-2147483648)


def _transpose_bits(a):
    a = list(a)
    j, msk = 16, 0x0000FFFF
    while j:
        m32 = jnp.int32(np.array(msk, np.uint32).view(np.int32))
        k = 0
        while k < WORD:
            t = (a[k] ^ (a[k + j] >> j)) & m32
            a[k] = a[k] ^ t
            a[k + j] = a[k + j] ^ (t << j)
            k = (k + j + 1) & ~j
        j >>= 1
        msk = (msk ^ (msk << j)) & 0xFFFFFFFF
    return a


def _dsa_kernel(qi_ref, kit_ref, wi_ref, qb_ref, kt_ref, v_ref, o_ref,
                key_ref, alive_ref, great_ref, wb_ref, qbd_ref, s_ref, p_ref, bias_ref, m_ref, l_ref,
                alpha_ref, acc_ref, *, seq, topk):
    tq = DSA_TQ
    ch = DSA_CH
    hg = 4
    gw = hg * DSA_HD
    bpc = ch // LANES
    cpg = DSA_GROUP // ch
    n_groups = seq // DSA_GROUP
    i = pl.program_id(0)
    n_ch = ((i + 1) * tq + ch - 1) // ch
    lane_i = lax.broadcasted_iota(I32, (tq, LANES), 1)
    row_pos = lax.broadcasted_iota(I32, (tq, 1), 0) + i * tq
    kf = float(topk)

    def tile_l(a, width):
        return jnp.concatenate([a] * (width // a.shape[1]), axis=1)

    def chunk(c):
        return pl.ds(pl.multiple_of(c * ch, ch), ch)

    def lane_sum(a):
        return jnp.sum(a.astype(F32), axis=1, keepdims=True)

    @pl.when(i == 0)
    def _():
        key_ref[...] = jnp.zeros(key_ref.shape, I32)

    wv = wi_ref[...]
    for h in range(IDX_HEADS):
        wb_ref[h] = jnp.broadcast_to(wv[:, IDX_DIM + h:IDX_DIM + h + 1], (tq, LANES))
    qi = qi_ref[0]

    def score_body(c, carry):
        lg = _dot(qi, kit_ref[:, chunk(c)])
        sc = jnp.maximum(lg[0:tq], 0.0) * tile_l(wb_ref[0], ch)
        for h in range(1, IDX_HEADS):
            sc = sc + jnp.maximum(lg[h * tq:(h + 1) * tq], 0.0) * tile_l(wb_ref[h], ch)
        sc = jnp.where(sc == 0.0, 0.0, sc)
        key_ref[c] = _float_to_ordered_bits(sc)
        return carry

    lax.fori_loop(0, n_ch, score_body, 0)

    n_pg = (n_ch + cpg - 1) // cpg

    def plane_body(t, carry):
        g = t // (tq // 8)
        rows = pl.ds(pl.multiple_of((t % (tq // 8)) * 8, 8), 8)

        def slot(b):
            return (g * cpg + b // bpc, rows, slice((b % bpc) * LANES, (b % bpc + 1) * LANES))

        planes = _transpose_bits([key_ref[slot(WORD - 1 - k)] for k in range(WORD)])
        for b in range(WORD):
            key_ref[slot(b)] = planes[b]
        return carry

    lax.fori_loop(0, n_pg * (tq // 8), plane_body, 0)

    def index_mask(g, bound):
        r = bound - g * DSA_GROUP - lane_i
        q = jnp.clip((r + (LANES - 1)) >> 7, 0, WORD)
        return jnp.where(q >= WORD, jnp.int32(-1), (jnp.int32(1) << jnp.minimum(q, WORD - 1)) - 1)

    for g in range(n_groups):
        alive_ref[g] = index_mask(g, row_pos + 1)
        great_ref[g] = jnp.zeros((tq, LANES), I32)

    def bit_body(p, cnt_great):
        pc = p // bpc
        pl0 = pl.multiple_of((p % bpc) * LANES, LANES)
        ones = []
        acc = jnp.zeros((tq, LANES), I32)
        for g in range(n_groups):
            x = alive_ref[g] & key_ref[g * cpg + pc, :, pl.ds(pl0, LANES)]
            acc = acc + lax.population_count(x)
            ones.append(x)
        cnt_one = lane_sum(acc)
        take = (cnt_great + cnt_one) >= kf
        take_b = jnp.broadcast_to(take, (tq, LANES))
        for g in range(n_groups):
            a = alive_ref[g]
            alive_ref[g] = jnp.where(take_b, ones[g], a ^ ones[g])
            great_ref[g] = jnp.where(take_b, great_ref[g], great_ref[g] | ones[g])
        return jnp.where(take, cnt_great, cnt_great + cnt_one)

    cnt_great = lax.fori_loop(0, WORD, bit_body, jnp.zeros((tq, 1), F32))
    need = kf - cnt_great

    def count_alive(bound):
        acc = jnp.zeros((tq, LANES), I32)
        for g in range(n_groups):
            acc = acc + lax.population_count(alive_ref[g] & index_mask(g, bound))
        return lane_sum(acc)

    tie_row = count_alive(jnp.full((tq, 1), seq, I32)) > need

    def tie_break(_):
        nbits = max(1, int(np.ceil(np.log2(seq))))

        def jbit(p, m):
            cand = m | jnp.left_shift(jnp.int32(1), nbits - 1 - p)
            return jnp.where(count_alive(cand) < need, cand, m)

        m = lax.fori_loop(0, nbits, jbit, jnp.zeros((tq, 1), I32))
        return jnp.where(tie_row, m + 1, jnp.int32(seq))

    any_tie = jnp.max(jnp.where(tie_row, 1.0, 0.0)) > 0.0
    bound = lax.cond(any_tie, tie_break, lambda _: jnp.full((tq, 1), seq, I32), 0)
    for g in range(n_groups):
        great_ref[g] = great_ref[g] | (alive_ref[g] & index_mask(g, bound))

    qt = qb_ref[...]
    head_of_lane = lax.broadcasted_iota(I32, (tq, gw), 1) >> 6
    for g in range(2):
        qg = qt[:, g * gw:(g + 1) * gw]
        for h in range(hg):
            qbd_ref[g, h * tq:(h + 1) * tq, :] = jnp.where(head_of_lane == h, qg, jnp.zeros_like(qg))
    m_ref[...] = jnp.full(m_ref.shape, MASKED, F32)
    l_ref[...] = jnp.zeros(l_ref.shape, F32)
    acc_ref[...] = jnp.zeros(acc_ref.shape, F32)

    def attn_body(c, carry):
        sel = great_ref[c // cpg]
        for jj in range(bpc):
            bit = (sel >> ((c % cpg) * bpc + jj)) & 1
            bias_ref[:, jj * LANES:(jj + 1) * LANES] = jnp.where(bit != 0, 0.0, MASKED)
        for g in range(2):
            s_ref[g] = _dot(qbd_ref[g], kt_ref[g * gw:(g + 1) * gw, chunk(c)])
        for g in range(2):
            for r in range(hg * tq // DSA_SLAB):
                rows = slice(r * DSA_SLAB, (r + 1) * DSA_SLAB)
                b0 = (r * DSA_SLAB) % tq
                s = s_ref[g, rows, :] + bias_ref[b0:b0 + DSA_SLAB, :]
                m_prev = m_ref[g, rows, :]
                m_new = jnp.maximum(m_prev, jnp.max(s, axis=1, keepdims=True))
                alpha = jnp.exp2(m_prev - m_new)
                p = jnp.exp2(s - tile_l(m_new, ch))
                l_ref[g, rows, :] = alpha * l_ref[g, rows, :] + jnp.sum(p, axis=1, keepdims=True)
                m_ref[g, rows, :] = m_new
                alpha_ref[g, rows, :] = alpha
                p_ref[g, rows, :] = p.astype(BF16)
            acc_ref[g] = (acc_ref[g] * tile_l(alpha_ref[g], gw)
                          + _dot(p_ref[g], v_ref[chunk(c), g * gw:(g + 1) * gw]))
        return carry

    lax.fori_loop(0, n_ch, attn_body, 0)

    outs = []
    for g in range(2):
        a = acc_ref[g] * tile_l(1.0 / l_ref[g], gw)
        og = jnp.zeros((tq, gw), F32)
        for h in range(hg):
            og = og + jnp.where(head_of_lane == h, a[h * tq:(h + 1) * tq], 0.0)
        outs.append(og)
    o_ref[...] = jnp.concatenate(outs, axis=1)


def _dsa(qi_r, kit, wi, qb, kt, v, topk):
    s = qb.shape[0]
    tq = DSA_TQ
    assert s % DSA_GROUP == 0 and DSA_GROUP % DSA_CH == 0
    rows4 = 4 * tq
    return pl.pallas_call(
        functools.partial(_dsa_kernel, seq=s, topk=topk),
        grid=(s // tq,),
        in_specs=[pl.BlockSpec((1, IDX_HEADS * tq, IDX_DIM), lambda i: (i, 0, 0)),
                  _resident(kit.shape),
                  pl.BlockSpec((tq, LANES), lambda i: (i, 0)),
                  pl.BlockSpec((tq, DSA_W), lambda i: (i, 0)),
                  _resident(kt.shape),
                  _resident(v.shape)],
        out_specs=pl.BlockSpec((tq, DSA_W), lambda i: (i, 0)),
        out_shape=jax.ShapeDtypeStruct((s, DSA_W), F32),
        scratch_shapes=[pltpu.VMEM((s // DSA_CH, tq, DSA_CH), I32),
                        pltpu.VMEM((s // DSA_GROUP, tq, LANES), I32),
                        pltpu.VMEM((s // DSA_GROUP, tq, LANES), I32),
                        pltpu.VMEM((IDX_HEADS, tq, LANES), F32),
                        pltpu.VMEM((2, rows4, 4 * DSA_HD), BF16),
                        pltpu.VMEM((2, rows4, DSA_CH), F32),
                        pltpu.VMEM((2, rows4, DSA_CH), BF16),
                        pltpu.VMEM((tq, DSA_CH), F32),
                        pltpu.VMEM((2, rows4, LANES), F32),
                        pltpu.VMEM((2, rows4, LANES), F32),
                        pltpu.VMEM((2, rows4, LANES), F32),
                        pltpu.VMEM((2, rows4, 4 * DSA_HD), F32)],
        compiler_params=_params(("arbitrary",)),
        name="dsa",
    )(qi_r, kit, wi, qb, kt, v)


def _merge_kernel(x_ref, mod_ref, gpre_ref, gpost_ref, oa_ref, ob_ref, wbg_ref, wpg_ref, wpd_ref,
                  wout_ref, o_ref):
    x = x_ref[...]
    h = _modulated(x, gpre_ref[...], mod_ref, 0, 1)
    gates = _sigmoid(_dot(h.astype(BF16), wbg_ref[...]))
    yg = _dot(oa_ref[...].astype(BF16), wpg_ref[...])
    yd = _dot(ob_ref[...].astype(BF16), wpd_ref[...])
    mix = gates[:, 0:D_MODEL] * yg + gates[:, D_MODEL:] * yd
    out = _dot(mix.astype(BF16), wout_ref[...])
    o_ref[...] = x + mod_ref[2:3, :] * _rms(out, gpost_ref[...])


def _merge(x, mod, gpre, gpost, oa, ob, wbg, wpg, wpd, wout):
    s = x.shape[0]
    tm = PROJ_ROWS
    row = lambda n: pl.BlockSpec((tm, n), lambda i: (i, 0))
    return pl.pallas_call(
        _merge_kernel,
        grid=(s // tm,),
        in_specs=[row(D_MODEL), _full(mod.shape), _full(gpre.shape), _full(gpost.shape),
                  row(GLA_V), row(DSA_W), _full(wbg.shape), _full(wpg.shape), _full(wpd.shape),
                  _full(wout.shape)],
        out_specs=row(D_MODEL),
        out_shape=jax.ShapeDtypeStruct((s, D_MODEL), F32),
        compiler_params=_params(("arbitrary",)),
        name="merge",
    )(x, mod, gpre, gpost, oa, ob, wbg, wpg, wpd, wout)


def _router(lg):
    t = lg.shape[0]
    lane = lax.broadcasted_iota(I32, (t, LANES), 1)
    lanef = lane.astype(F32)
    big = 1e9
    gm = lane < N_GROUPS
    gmax = jnp.max(jnp.where(gm, lg, -jnp.inf), axis=1, keepdims=True)
    gsum = jnp.sum(jnp.where(gm, jnp.exp(lg - gmax), 0.0), axis=1, keepdims=True)
    p_g = 1.0 / gsum
    g_sel = jnp.min(jnp.where(gm & (lg == gmax), lanef, big), axis=1, keepdims=True)
    lo = N_GROUPS + EXPERTS_PER_GROUP * g_sel
    em = (lanef >= lo) & (lanef < lo + EXPERTS_PER_GROUP)
    m1 = jnp.max(jnp.where(em, lg, -jnp.inf), axis=1, keepdims=True)
    i1 = jnp.min(jnp.where(em & (lg == m1), lanef, big), axis=1, keepdims=True)
    em2 = em & (lanef != i1)
    m2 = jnp.max(jnp.where(em2, lg, -jnp.inf), axis=1, keepdims=True)
    i2 = jnp.min(jnp.where(em2 & (lg == m2), lanef, big), axis=1, keepdims=True)
    e2 = jnp.exp(m2 - m1)
    inv = 1.0 / (1.0 + e2)
    return (jnp.where(lanef == i1, p_g * inv, 0.0) + jnp.where(lanef == i2, p_g * (e2 * inv), 0.0))


def _moe_kernel(x_ref, mod_ref, gpre_ref, gpost_ref, wr_ref, br_ref, wg_ref, wu_ref, wd_ref,
                o_ref, hb_ref, comb_ref, acc_ref):
    j = pl.program_id(1)
    tm = x_ref.shape[0]

    @pl.when(j == 0)
    def _():
        h = _modulated(x_ref[...], gpre_ref[...], mod_ref, 3, 4)
        hb = h.astype(BF16)
        hb_ref[...] = hb
        comb_ref[...] = _router(_dot(hb, wr_ref[...]) + br_ref[...])
        acc_ref[...] = jnp.zeros_like(acc_ref)

    hb = hb_ref[...]
    hgate = _dot(hb, wg_ref[...])
    hup = _dot(hb, wu_ref[...])
    act = hgate * _sigmoid(hgate) * hup
    comb = comb_ref[...]
    lane = lax.broadcasted_iota(I32, (tm, LANES), 1)
    parts = []
    for e in range(MOE_EB):
        sel = lane == (N_GROUPS + j * MOE_EB + e)
        cw = jnp.sum(jnp.where(sel, comb, 0.0), axis=1, keepdims=True)
        parts.append((act[:, e * D_EXPERT:(e + 1) * D_EXPERT] * cw).astype(BF16))
    acc_ref[...] += _dot(jnp.concatenate(parts, axis=1), wd_ref[...])

    @pl.when(j == pl.num_programs(1) - 1)
    def _():
        o_ref[...] = x_ref[...] + mod_ref[5:6, :] * _rms(acc_ref[...], gpost_ref[...])


def _moe(x, mod, gpre, gpost, wr, br, wg, wu, wd):
    s = x.shape[0]
    tm = MOE_ROWS
    bw = MOE_EB * D_EXPERT
    return pl.pallas_call(
        _moe_kernel,
        grid=(s // tm, N_EXPERTS // MOE_EB),
        in_specs=[pl.BlockSpec((tm, D_MODEL), lambda i, j: (i, 0)),
                  _full(mod.shape), _full(gpre.shape), _full(gpost.shape), _full(wr.shape),
                  _full(br.shape),
                  pl.BlockSpec((D_MODEL, bw), lambda i, j: (0, j)),
                  pl.BlockSpec((D_MODEL, bw), lambda i, j: (0, j)),
                  pl.BlockSpec((bw, D_MODEL), lambda i, j: (j, 0))],
        out_specs=pl.BlockSpec((tm, D_MODEL), lambda i, j: (i, 0)),
        out_shape=jax.ShapeDtypeStruct((s, D_MODEL), F32),
        scratch_shapes=[pltpu.VMEM((tm, D_MODEL), BF16), pltpu.VMEM((tm, LANES), F32),
                        pltpu.VMEM((tm, D_MODEL), F32)],
        compiler_params=_params(("arbitrary", "arbitrary")),
        name="moe",
    )(x, mod, gpre, gpost, wr, br, wg, wu, wd)


def _rope_lane_constants():
    rot = DSA_HD // ROT_FRAC
    half = rot // 2
    freqs = np.power(np.float32(ROPE_THETA), -np.arange(half, dtype=np.float32) * np.float32(2.0) / rot)
    j = np.arange(LANES) % DSA_HD
    fv = np.where(j < rot, freqs[j % half], 0.0).astype(np.float32)
    sg = np.where(j < half, -1.0, np.where(j < rot, 1.0, 0.0)).astype(np.float32)
    return jnp.asarray(fv)[None, :], jnp.asarray(sg)[None, :]


def _pad_cols(w, n):
    return jnp.pad(w, ((0, 0), (0, n - w.shape[1])))


def _layer(x, c, pos, w_ada, b_ada, g_pre_mix, g_post_mix, g_pre_ffn, g_post_ffn, w_in, w_gla_a2,
           b_gla_a, g_gla_out, g_idx_k, w_proj_gla, w_proj_dsa, w_out, w_router_g, b_router_g,
           w_router_e, b_router_e, w_e_gate, w_e_up, w_e_down):
    s = x.shape[0]
    mod = _ada(jnp.broadcast_to(c, (8, D_MODEL)), w_ada, b_ada[None, :])[0].reshape(N_MOD, D_MODEL)

    o = np.cumsum((GLA_QK, GLA_QK, GLA_V, GLA_V, GLA_GATE_RANK, DSA_W, DSA_W, DSA_W, IDX_Q, IDX_DIM,
                   IDX_HEADS, 2 * D_MODEL))
    wb = w_in.astype(BF16)
    w_gla = wb[:, 0:o[3]]
    w_ga = _pad_cols(wb[:, o[3]:o[4]], LANES)
    w_dsa = wb[:, o[4]:o[8]]
    w_sm = _pad_cols(jnp.concatenate([wb[:, o[8]:o[9]], wb[:, o[9]:o[10]]], axis=1), LANES)
    w_bg = wb[:, o[10]:o[11]]
    w_a2 = jnp.pad(w_gla_a2.astype(BF16), ((0, LANES - GLA_GATE_RANK), (0, 0)))

    q_a, k_a, v_a, gg, la = _gla_proj(x, mod, g_pre_mix[None, :], w_gla, w_ga, w_a2, b_gla_a[None, :])
    o_a = _gla(q_a, k_a, v_a, gg, la, jnp.tile(g_gla_out, GLA_HEADS)[None, :])

    fv, sg = _rope_lane_constants()
    gik = jnp.pad(g_idx_k, (0, LANES - IDX_DIM))[None, :]
    q_b, k_b, v_b, qi, ki, wi = _dsa_proj(x, mod, g_pre_mix[None, :], pos.astype(F32)[:, None], fv, sg,
                                          w_dsa, w_sm, gik)
    nqb = s // DSA_TQ
    qi_r = qi.reshape(nqb, DSA_TQ, IDX_HEADS, IDX_DIM).transpose(0, 2, 1, 3).reshape(
        nqb, IDX_HEADS * DSA_TQ, IDX_DIM)
    o_b = _dsa(qi_r, ki.T, wi, q_b, k_b.T, v_b, min(DSA_TOPK_MAX, s // 4))

    x1 = _merge(x, mod, g_pre_mix[None, :], g_post_mix[None, :], o_a, o_b, w_bg,
                w_proj_gla.astype(BF16), w_proj_dsa.astype(BF16), w_out.astype(BF16))

    wr = _pad_cols(jnp.concatenate([w_router_g, w_router_e], axis=1).astype(BF16), LANES)
    br = jnp.pad(jnp.concatenate([b_router_g, b_router_e]), (0, LANES - N_GROUPS - N_EXPERTS))[None, :]
    wg = w_e_gate.astype(BF16).transpose(1, 0, 2).reshape(D_MODEL, N_EXPERTS * D_EXPERT)
    wu = w_e_up.astype(BF16).transpose(1, 0, 2).reshape(D_MODEL, N_EXPERTS * D_EXPERT)
    wd = w_e_down.astype(BF16).reshape(N_EXPERTS * D_EXPERT, D_MODEL)
    return _moe(x1, mod, g_pre_ffn[None, :], g_post_ffn[None, :], wr, br, wg, wu, wd)


def kernel(x, c, positions, w_ada, b_ada, g_pre_mix, g_post_mix, g_pre_ffn, g_post_ffn, w_in, w_gla_a2,
           b_gla_a, g_gla_out, g_idx_k, w_proj_gla, w_proj_dsa, w_out, w_router_g, b_router_g,
           w_router_e, b_router_e, w_e_gate, w_e_up, w_e_down):
    batch, depth = x.shape[0], w_ada.shape[0]
    outs = []
    for bi in range(batch):
        xb = x[bi]
        for l in range(depth):
            xb = _layer(xb, c[bi:bi + 1], positions[bi], w_ada[l], b_ada[l], g_pre_mix[l], g_post_mix[l],
                        g_pre_ffn[l], g_post_ffn[l], w_in[l], w_gla_a2[l], b_gla_a[l], g_gla_out[l],
                        g_idx_k[l], w_proj_gla[l], w_proj_dsa[l], w_out[l], w_router_g[l],
                        b_router_g[l], w_router_e[l], b_router_e[l], w_e_gate[l], w_e_up[l],
                        w_e_down[l])
        outs.append(xb)
    return jnp.stack(outs, axis=0)
```

```python
import functools

import numpy as np
import jax
import jax.numpy as jnp
from jax import lax
from jax.experimental import pallas as pl
from jax.experimental.pallas import tpu as pltpu

F32 = jnp.float32
BF16 = jnp.bfloat16
I32 = jnp.int32

D_MODEL = 1024
EPS = 1e-6
ROPE_THETA = 500000.0
ROT_FRAC = 4
GLA_HEADS = 4
GLA_DK = 64
GLA_DV = 128
GLA_GATE_RANK = 16
GLA_TAU = 16.0
GLA_CHUNK = 64
DSA_HEADS = 8
DSA_HD = 64
IDX_HEADS = 8
IDX_DIM = 64
DSA_TOPK_MAX = 256
N_GROUPS = 4
EXPERTS_PER_GROUP = 8
N_EXPERTS = N_GROUPS * EXPERTS_PER_GROUP
D_EXPERT = D_MODEL // 4
N_MOD = 6

GLA_QK = GLA_HEADS * GLA_DK
GLA_V = GLA_HEADS * GLA_DV
DSA_W = DSA_HEADS * DSA_HD
IDX_Q = IDX_HEADS * IDX_DIM

LANES = 128
VMEM_LIMIT_BYTES = 56 * 1024 * 1024

PROJ_ROWS = 512
GLA_ROWS = 512
DSA_TQ = 128
DSA_CH = 1024
MOE_ROWS = 512
MOE_EB = 8
MASKED = -1e30
LOG2E = float(np.log2(np.e))
GLA_LEVELS = 7


def _dot(a, b):
    return jnp.dot(a, b, preferred_element_type=F32)


def _dot_nt(a, b):
    return lax.dot_general(a, b, (((1,), (1,)), ((), ())), preferred_element_type=F32)


def _dot_tn(a, b):
    return lax.dot_general(a, b, (((0,), (0,)), ((), ())), preferred_element_type=F32)


def _sigmoid(x):
    return 1.0 / (1.0 + jnp.exp(-x))


def _rms(x, g):
    ms = jnp.mean(x * x, axis=-1, keepdims=True)
    return x * lax.rsqrt(ms + EPS) * g


def _modulated(x, g, mod_ref, shift_row, scale_row):
    return (_rms(x, g) * (1.0 + mod_ref[scale_row:scale_row + 1, :])
            + mod_ref[shift_row:shift_row + 1, :])


def _params(sem):
    return pltpu.CompilerParams(dimension_semantics=sem, vmem_limit_bytes=VMEM_LIMIT_BYTES)


def _full(shape):
    return pl.BlockSpec(shape, lambda *_: (0,) * len(shape))


def _resident(shape):
    return pl.BlockSpec(shape, lambda *_: (0,) * len(shape), pipeline_mode=pl.Buffered(1))


def _ada_kernel(c_ref, w_ref, b_ref, o_ref):
    c = c_ref[...]
    a = c * _sigmoid(c)
    o_ref[...] = _dot(a.astype(BF16), w_ref[...].astype(BF16)) + b_ref[...]


def _ada(c8, w, b):
    n = w.shape[1]
    bn = 1536
    return pl.pallas_call(
        _ada_kernel,
        grid=(n // bn,),
        in_specs=[_full(c8.shape),
                  pl.BlockSpec((D_MODEL, bn), lambda j: (0, j)),
                  pl.BlockSpec((1, bn), lambda j: (0, j))],
        out_specs=pl.BlockSpec((8, bn), lambda j: (0, j)),
        out_shape=jax.ShapeDtypeStruct((8, n), F32),
        compiler_params=_params(("arbitrary",)),
        name="ada",
    )(c8, w, b)


def _gla_proj_kernel(x_ref, mod_ref, g_ref, w_ref, wga_ref, wa2_ref, ba_ref,
                     q_ref, k_ref, v_ref, gg_ref, la_ref):
    h = _modulated(x_ref[...], g_ref[...], mod_ref, 0, 1)
    hb = h.astype(BF16)
    p = _dot(hb, w_ref[...])
    q_ref[...] = p[:, 0:GLA_QK] * (GLA_DK ** -0.5)
    k_ref[...] = p[:, GLA_QK:2 * GLA_QK]
    v_ref[...] = p[:, 2 * GLA_QK:2 * GLA_QK + GLA_V]
    gg_ref[...] = p[:, 2 * GLA_QK + GLA_V:]
    ga = _dot(hb, wga_ref[...])
    z = _dot(ga.astype(BF16), wa2_ref[...]) + ba_ref[...]
    log_sig = jnp.minimum(z, 0.0) - jnp.log1p(jnp.exp(-jnp.abs(z)))
    la_ref[...] = log_sig * (1.0 / GLA_TAU)


def _gla_proj(x, mod, g, w, wga, wa2, ba):
    s = x.shape[0]
    tm = PROJ_ROWS
    row = lambda n: pl.BlockSpec((tm, n), lambda i: (i, 0))
    return pl.pallas_call(
        _gla_proj_kernel,
        grid=(s // tm,),
        in_specs=[row(D_MODEL), _full(mod.shape), _full(g.shape), _full(w.shape),
                  _full(wga.shape), _full(wa2.shape), _full(ba.shape)],
        out_specs=[row(GLA_QK), row(GLA_QK), row(GLA_V), row(GLA_V), row(GLA_QK)],
        out_shape=[jax.ShapeDtypeStruct((s, GLA_QK), F32), jax.ShapeDtypeStruct((s, GLA_QK), F32),
                   jax.ShapeDtypeStruct((s, GLA_V), F32), jax.ShapeDtypeStruct((s, GLA_V), F32),
                   jax.ShapeDtypeStruct((s, GLA_QK), F32)],
        compiler_params=_params(("arbitrary",)),
        name="gla_proj",
    )(x, mod, g, w, wga, wa2, ba)


def _dsa_proj_kernel(x_ref, mod_ref, g_ref, pos_ref, fv_ref, sg_ref, w_ref, wsm_ref, gik_ref,
                     qb_ref, kb_ref, vb_ref, qi_ref, ki_ref, wi_ref):
    tm = x_ref.shape[0]
    h = _modulated(x_ref[...], g_ref[...], mod_ref, 0, 1)
    hb = h.astype(BF16)
    ang = pos_ref[...] * fv_ref[...]
    cs = jnp.cos(ang)
    sn = jnp.sin(ang) * sg_ref[...]
    lane = lax.broadcasted_iota(I32, (tm, LANES), 1)
    first = (lane & (DSA_HD - 1)) < (DSA_HD // ROT_FRAC // 2)

    def rope(t):
        width = t.shape[1]
        rep = width // LANES
        tile = (lambda a: jnp.concatenate([a] * rep, axis=1)) if rep > 1 else (lambda a: a)
        half = DSA_HD // ROT_FRAC // 2
        fwd = pltpu.roll(t, half, 1)
        bwd = pltpu.roll(t, width - half, 1)
        partner = jnp.where(tile(first), bwd, fwd)
        return t * tile(cs) + partner * tile(sn)

    p = _dot(hb, w_ref[...])
    qb_ref[...] = (rope(p[:, 0:DSA_W]) * (DSA_HD ** -0.5 * LOG2E)).astype(BF16)
    kb_ref[...] = rope(p[:, DSA_W:2 * DSA_W]).astype(BF16)
    vb_ref[...] = p[:, 2 * DSA_W:3 * DSA_W].astype(BF16)
    qi_ref[...] = (rope(p[:, 3 * DSA_W:]) * (IDX_DIM ** -0.5)).astype(BF16)

    sm = _dot(hb, wsm_ref[...])
    is_ik = lane < IDX_DIM
    mu = jnp.sum(jnp.where(is_ik, sm, 0.0), axis=-1, keepdims=True) * (1.0 / IDX_DIM)
    xc = jnp.where(is_ik, sm - mu, 0.0)
    var = jnp.sum(xc * xc, axis=-1, keepdims=True) * (1.0 / IDX_DIM)
    y = xc * lax.rsqrt(var + EPS) * gik_ref[...]
    ki_ref[...] = rope(y)[:, 0:IDX_DIM].astype(BF16)
    wi_ref[...] = sm * (IDX_HEADS ** -0.5)


def _dsa_proj(x, mod, g, pos, fv, sg, w, wsm, gik):
    s = x.shape[0]
    tm = PROJ_ROWS
    row = lambda n: pl.BlockSpec((tm, n), lambda i: (i, 0))
    return pl.pallas_call(
        _dsa_proj_kernel,
        grid=(s // tm,),
        in_specs=[row(D_MODEL), _full(mod.shape), _full(g.shape), row(1), _full(fv.shape),
                  _full(sg.shape), _full(w.shape), _full(wsm.shape), _full(gik.shape)],
        out_specs=[row(DSA_W), row(DSA_W), row(DSA_W), row(IDX_Q), row(IDX_DIM), row(LANES)],
        out_shape=[jax.ShapeDtypeStruct((s, DSA_W), BF16), jax.ShapeDtypeStruct((s, DSA_W), BF16),
                   jax.ShapeDtypeStruct((s, DSA_W), BF16), jax.ShapeDtypeStruct((s, IDX_Q), BF16),
                   jax.ShapeDtypeStruct((s, IDX_DIM), BF16), jax.ShapeDtypeStruct((s, LANES), F32)],
        compiler_params=_params(("arbitrary",)),
        name="dsa_proj",
    )(x, mod, g, pos, fv, sg, w, wsm, gik)


def _gla_constants():
    c = GLA_CHUNK
    tril = np.tril(np.ones((c, c), np.float32))
    t = np.arange(c)
    mats = [tril]
    masks = [(t[:, None] == t[None, :])]
    for hs in (32, 16, 8, 4, 2, 1):
        blk = 2 * hs
        r = (t // blk) * blk + hs - 1
        mats.append(tril - tril[r, :])
        same = (t[:, None] // blk) == (t[None, :] // blk)
        masks.append(same & ((t[:, None] % blk) >= hs) & ((t[None, :] % blk) < hs))
    m_all = np.concatenate(mats, axis=0)
    lvl = np.stack([np.tile(m.astype(np.float32), (1, GLA_HEADS)) for m in masks])
    hrow = np.arange(GLA_HEADS * c) // c
    wmask = (hrow[:, None] == (np.arange(GLA_QK) // GLA_DK)[None, :]).astype(np.float32)
    vmask = (hrow[:, None] == (np.arange(GLA_V) // GLA_DV)[None, :]).astype(np.float32)
    smask = ((np.arange(GLA_V) // GLA_DV)[:, None] == (np.arange(GLA_QK) // GLA_DK)[None, :])
    return (jnp.asarray(m_all, BF16), jnp.asarray(lvl), jnp.asarray(wmask), jnp.asarray(vmask),
            jnp.asarray(smask.astype(np.float32)))


def _gla_kernel(q_ref, k_ref, v_ref, gg_ref, la_ref, gout_ref, mall_ref, lvl_ref, wmask_ref,
                vmask_ref, smask_ref, o_ref, st_ref):
    c = GLA_CHUNK

    @pl.when(pl.program_id(0) == 0)
    def _():
        st_ref[...] = jnp.zeros_like(st_ref)

    m_all = mall_ref[...]
    wmask = wmask_ref[...]
    vmask = vmask_ref[...]
    smask = smask_ref[...]
    gout = gout_ref[...]

    def chunk(ci, carry):
        r0 = pl.multiple_of(ci * c, c)
        rows = pl.ds(r0, c)
        q = q_ref[rows, :]
        k = k_ref[rows, :]
        v = v_ref[rows, :]
        la = la_ref[rows, :]
        hi = la.astype(BF16)
        r1 = la - hi.astype(F32)
        mid = r1.astype(BF16)
        lo = (r1 - mid.astype(F32)).astype(BF16)
        dall = _dot(m_all, hi) + _dot(m_all, mid) + _dot(m_all, lo)
        b = dall[0:c]
        b_last = b[c - 1:c, :]
        qhat = q * jnp.exp(b)
        khat = k * jnp.exp(b_last - b)

        a = jnp.zeros((c, GLA_HEADS * c), F32)
        for lv in range(GLA_LEVELS):
            if lv == 0:
                qt, kt = q, k
            else:
                d = dall[lv * c:(lv + 1) * c]
                qt = q * jnp.exp(jnp.minimum(d, 0.0))
                kt = k * jnp.exp(jnp.minimum(-d, 0.0))
            w = (jnp.concatenate([kt] * GLA_HEADS, axis=0) * wmask).astype(BF16)
            a = a + lvl_ref[lv] * _dot_nt(qt.astype(BF16), w)

        st = st_ref[...]
        vbd = (jnp.concatenate([v] * GLA_HEADS, axis=0) * vmask).astype(BF16)
        o = _dot(a.astype(BF16), vbd) + _dot_nt(qhat.astype(BF16), st.astype(BF16))
        st_ref[...] = st * jnp.exp(b_last) + smask * _dot_tn(v.astype(BF16), khat.astype(BF16))

        parts = []
        for hh in range(GLA_HEADS):
            oh = o[:, hh * GLA_DV:(hh + 1) * GLA_DV]
            ms = jnp.mean(oh * oh, axis=-1, keepdims=True)
            parts.append(oh * lax.rsqrt(ms + EPS))
        gg = gg_ref[rows, :]
        o_ref[rows, :] = jnp.concatenate(parts, axis=1) * gout * (gg * _sigmoid(gg))
        return carry

    lax.fori_loop(0, q_ref.shape[0] // c, chunk, 0, unroll=4)


def _gla(q, k, v, gg, la, gout):
    s = q.shape[0]
    tb = GLA_ROWS
    consts = _gla_constants()
    row = lambda n: pl.BlockSpec((tb, n), lambda i: (i, 0))
    return pl.pallas_call(
        _gla_kernel,
        grid=(s // tb,),
        in_specs=[row(GLA_QK), row(GLA_QK), row(GLA_V), row(GLA_V), row(GLA_QK), _full(gout.shape)]
                 + [_full(a.shape) for a in consts],
        out_specs=row(GLA_V),
        out_shape=jax.ShapeDtypeStruct((s, GLA_V), F32),
        scratch_shapes=[pltpu.VMEM((GLA_V, GLA_QK), F32)],
        compiler_params=_params(("arbitrary",)),
        name="gla",
    )(q, k, v, gg, la, gout, *consts)


DSA_SLAB = 32
WORD = 32
DSA_GROUP = WORD * LANES


def _float_to_ordered_bits(x):
    bits = lax.bitcast_convert_type(x, I32)
    key = bits ^ ((bits >> 31) & jnp.int32(0x7FFFFFFF))
    return key ^ jnp.int32(-2147483648)


def _transpose_bits(a):
    a = list(a)
    j, msk = 16, 0x0000FFFF
    while j:
        m32 = jnp.int32(np.array(msk, np.uint32).view(np.int32))
        k = 0
        while k < WORD:
            t = (a[k] ^ (a[k + j] >> j)) & m32
            a[k] = a[k] ^ t
            a[k + j] = a[k + j] ^ (t << j)
            k = (k + j + 1) & ~j
        j >>= 1
        msk = (msk ^ (msk << j)) & 0xFFFFFFFF
    return a


def _dsa_kernel(qi_ref, kit_ref, wi_ref, qb_ref, kt_ref, v_ref, o_ref,
                key_ref, alive_ref, great_ref, wb_ref, qbd_ref, s_ref, p_ref, bias_ref, m_ref, l_ref,
                alpha_ref, acc_ref, *, seq, topk):
    tq = DSA_TQ
    ch = DSA_CH
    hg = 4
    gw = hg * DSA_HD
    bpc = ch // LANES
    cpg = DSA_GROUP // ch
    n_groups = seq // DSA_GROUP
    i = pl.program_id(0)
    n_ch = ((i + 1) * tq + ch - 1) // ch
    lane_i = lax.broadcasted_iota(I32, (tq, LANES), 1)
    row_pos = lax.broadcasted_iota(I32, (tq, 1), 0) + i * tq
    kf = float(topk)

    def tile_l(a, width):
        return jnp.concatenate([a] * (width // a.shape[1]), axis=1)

    def chunk(c):
        return pl.ds(pl.multiple_of(c * ch, ch), ch)

    def lane_sum(a):
        return jnp.sum(a.astype(F32), axis=1, keepdims=True)

    @pl.when(i == 0)
    def _():
        key_ref[...] = jnp.zeros(key_ref.shape, I32)

    wv = wi_ref[...]
    for h in range(IDX_HEADS):
        wb_ref[h] = jnp.broadcast_to(wv[:, IDX_DIM + h:IDX_DIM + h + 1], (tq, LANES))
    qi = qi_ref[0]

    def score_body(c, carry):
        lg = _dot(qi, kit_ref[:, chunk(c)])
        sc = jnp.maximum(lg[0:tq], 0.0) * tile_l(wb_ref[0], ch)
        for h in range(1, IDX_HEADS):
            sc = sc + jnp.maximum(lg[h * tq:(h + 1) * tq], 0.0) * tile_l(wb_ref[h], ch)
        sc = jnp.where(sc == 0.0, 0.0, sc)
        key_ref[c] = _float_to_ordered_bits(sc)
        return carry

    lax.fori_loop(0, n_ch, score_body, 0)

    n_pg = (n_ch + cpg - 1) // cpg

    def plane_body(t, carry):
        g = t // (tq // 8)
        rows = pl.ds(pl.multiple_of((t % (tq // 8)) * 8, 8), 8)

        def slot(b):
            return (g * cpg + b // bpc, rows, slice((b % bpc) * LANES, (b % bpc + 1) * LANES))

        planes = _transpose_bits([key_ref[slot(WORD - 1 - k)] for k in range(WORD)])
        for b in range(WORD):
            key_ref[slot(b)] = planes[b]
        return carry

    lax.fori_loop(0, n_pg * (tq // 8), plane_body, 0)

    def index_mask(g, bound):
        r = bound - g * DSA_GROUP - lane_i
        q = jnp.clip((r + (LANES - 1)) >> 7, 0, WORD)
        return jnp.where(q >= WORD, jnp.int32(-1), (jnp.int32(1) << jnp.minimum(q, WORD - 1)) - 1)

    for g in range(n_groups):
        alive_ref[g] = index_mask(g, row_pos + 1)
        great_ref[g] = jnp.zeros((tq, LANES), I32)

    def select_bits(ng):
        def bit_body(p, cnt_great):
            pc = p // bpc
            pl0 = pl.multiple_of((p % bpc) * LANES, LANES)
            ones = []
            acc = jnp.zeros((tq, LANES), I32)
            for g in range(ng):
                x = alive_ref[g] & key_ref[g * cpg + pc, :, pl.ds(pl0, LANES)]
                acc = acc + lax.population_count(x)
                ones.append(x)
            cnt_one = lane_sum(acc)
            take = (cnt_great + cnt_one) >= kf
            take_b = jnp.broadcast_to(take, (tq, LANES))
            for g in range(ng):
                a = alive_ref[g]
                alive_ref[g] = jnp.where(take_b, ones[g], a ^ ones[g])
                great_ref[g] = jnp.where(take_b, great_ref[g], great_ref[g] | ones[g])
            return jnp.where(take, cnt_great, cnt_great + cnt_one)

        return lambda: lax.fori_loop(0, WORD, bit_body, jnp.zeros((tq, 1), F32))

    cnt_great = lax.switch(n_pg - 1, [select_bits(ng) for ng in range(1, n_groups + 1)])
    need = kf - cnt_great

    def count_alive(bound):
        acc = jnp.zeros((tq, LANES), I32)
        for g in range(n_groups):
            acc = acc + lax.population_count(alive_ref[g] & index_mask(g, bound))
        return lane_sum(acc)

    tie_row = count_alive(jnp.full((tq, 1), seq, I32)) > need

    def tie_break(_):
        nbits = max(1, int(np.ceil(np.log2(seq))))

        def jbit(p, m):
            cand = m | jnp.left_shift(jnp.int32(1), nbits - 1 - p)
            return jnp.where(count_alive(cand) < need, cand, m)

        m = lax.fori_loop(0, nbits, jbit, jnp.zeros((tq, 1), I32))
        return jnp.where(tie_row, m + 1, jnp.int32(seq))

    any_tie = jnp.max(jnp.where(tie_row, 1.0, 0.0)) > 0.0
    bound = lax.cond(any_tie, tie_break, lambda _: jnp.full((tq, 1), seq, I32), 0)
    for g in range(n_groups):
        great_ref[g] = great_ref[g] | (alive_ref[g] & index_mask(g, bound))

    qt = qb_ref[...]
    head_of_lane = lax.broadcasted_iota(I32, (tq, gw), 1) >> 6
    for g in range(2):
        qg = qt[:, g * gw:(g + 1) * gw]
        for h in range(hg):
            qbd_ref[g, h * tq:(h + 1) * tq, :] = jnp.where(head_of_lane == h, qg, jnp.zeros_like(qg))
    m_ref[...] = jnp.full(m_ref.shape, MASKED, F32)
    l_ref[...] = jnp.zeros(l_ref.shape, F32)
    acc_ref[...] = jnp.zeros(acc_ref.shape, F32)

    def attn_body(c, carry):
        sel = great_ref[c // cpg]
        for jj in range(bpc):
            bit = (sel >> ((c % cpg) * bpc + jj)) & 1
            bias_ref[:, jj * LANES:(jj + 1) * LANES] = jnp.where(bit != 0, 0.0, MASKED)
        for g in range(2):
            s_ref[g] = _dot(qbd_ref[g], kt_ref[g * gw:(g + 1) * gw, chunk(c)])
        for g in range(2):
            for r in range(hg * tq // DSA_SLAB):
                rows = slice(r * DSA_SLAB, (r + 1) * DSA_SLAB)
                b0 = (r * DSA_SLAB) % tq
                s = s_ref[g, rows, :] + bias_ref[b0:b0 + DSA_SLAB, :]
                m_prev = m_ref[g, rows, :]
                m_new = jnp.maximum(m_prev, jnp.max(s, axis=1, keepdims=True))
                alpha = jnp.exp2(m_prev - m_new)
                p = jnp.exp2(s - tile_l(m_new, ch))
                l_ref[g, rows, :] = alpha * l_ref[g, rows, :] + jnp.sum(p, axis=1, keepdims=True)
                m_ref[g, rows, :] = m_new
                alpha_ref[g, rows, :] = alpha
                p_ref[g, rows, :] = p.astype(BF16)
            acc_ref[g] = (acc_ref[g] * tile_l(alpha_ref[g], gw)
                          + _dot(p_ref[g], v_ref[chunk(c), g * gw:(g + 1) * gw]))
        return carry

    lax.fori_loop(0, n_ch, attn_body, 0)

    outs = []
    for g in range(2):
        a = acc_ref[g] * tile_l(1.0 / l_ref[g], gw)
        og = jnp.zeros((tq, gw), F32)
        for h in range(hg):
            og = og + jnp.where(head_of_lane == h, a[h * tq:(h + 1) * tq], 0.0)
        outs.append(og)
    o_ref[...] = jnp.concatenate(outs, axis=1)


def _dsa(qi_r, kit, wi, qb, kt, v, topk):
    s = qb.shape[0]
    tq = DSA_TQ
    assert s % DSA_GROUP == 0 and DSA_GROUP % DSA_CH == 0
    rows4 = 4 * tq
    return pl.pallas_call(
        functools.partial(_dsa_kernel, seq=s, topk=topk),
        grid=(s // tq,),
        in_specs=[pl.BlockSpec((1, IDX_HEADS * tq, IDX_DIM), lambda i: (i, 0, 0)),
                  _resident(kit.shape),
                  pl.BlockSpec((tq, LANES), lambda i: (i, 0)),
                  pl.BlockSpec((tq, DSA_W), lambda i: (i, 0)),
                  _resident(kt.shape),
                  _resident(v.shape)],
        out_specs=pl.BlockSpec((tq, DSA_W), lambda i: (i, 0)),
        out_shape=jax.ShapeDtypeStruct((s, DSA_W), F32),
        scratch_shapes=[pltpu.VMEM((s // DSA_CH, tq, DSA_CH), I32),
                        pltpu.VMEM((s // DSA_GROUP, tq, LANES), I32),
                        pltpu.VMEM((s // DSA_GROUP, tq, LANES), I32),
                        pltpu.VMEM((IDX_HEADS, tq, LANES), F32),
                        pltpu.VMEM((2, rows4, 4 * DSA_HD), BF16),
                        pltpu.VMEM((2, rows4, DSA_CH), F32),
                        pltpu.VMEM((2, rows4, DSA_CH), BF16),
                        pltpu.VMEM((tq, DSA_CH), F32),
                        pltpu.VMEM((2, rows4, LANES), F32),
                        pltpu.VMEM((2, rows4, LANES), F32),
                        pltpu.VMEM((2, rows4, LANES), F32),
                        pltpu.VMEM((2, rows4, 4 * DSA_HD), F32)],
        compiler_params=_params(("arbitrary",)),
        name="dsa",
    )(qi_r, kit, wi, qb, kt, v)


def _merge_kernel(x_ref, mod_ref, gpre_ref, gpost_ref, oa_ref, ob_ref, wbg_ref, wpg_ref, wpd_ref,
                  wout_ref, o_ref):
    x = x_ref[...]
    h = _modulated(x, gpre_ref[...], mod_ref, 0, 1)
    gates = _sigmoid(_dot(h.astype(BF16), wbg_ref[...]))
    yg = _dot(oa_ref[...].astype(BF16), wpg_ref[...])
    yd = _dot(ob_ref[...].astype(BF16), wpd_ref[...])
    mix = gates[:, 0:D_MODEL] * yg + gates[:, D_MODEL:] * yd
    out = _dot(mix.astype(BF16), wout_ref[...])
    o_ref[...] = x + mod_ref[2:3, :] * _rms(out, gpost_ref[...])


def _merge(x, mod, gpre, gpost, oa, ob, wbg, wpg, wpd, wout):
    s = x.shape[0]
    tm = PROJ_ROWS
    row = lambda n: pl.BlockSpec((tm, n), lambda i: (i, 0))
    return pl.pallas_call(
        _merge_kernel,
        grid=(s // tm,),
        in_specs=[row(D_MODEL), _full(mod.shape), _full(gpre.shape), _full(gpost.shape),
                  row(GLA_V), row(DSA_W), _full(wbg.shape), _full(wpg.shape), _full(wpd.shape),
                  _full(wout.shape)],
        out_specs=row(D_MODEL),
        out_shape=jax.ShapeDtypeStruct((s, D_MODEL), F32),
        compiler_params=_params(("arbitrary",)),
        name="merge",
    )(x, mod, gpre, gpost, oa, ob, wbg, wpg, wpd, wout)


def _router(lg):
    t = lg.shape[0]
    lane = lax.broadcasted_iota(I32, (t, LANES), 1)
    lanef = lane.astype(F32)
    big = 1e9
    gm = lane < N_GROUPS
    gmax = jnp.max(jnp.where(gm, lg, -jnp.inf), axis=1, keepdims=True)
    gsum = jnp.sum(jnp.where(gm, jnp.exp(lg - gmax), 0.0), axis=1, keepdims=True)
    p_g = 1.0 / gsum
    g_sel = jnp.min(jnp.where(gm & (lg == gmax), lanef, big), axis=1, keepdims=True)
    lo = N_GROUPS + EXPERTS_PER_GROUP * g_sel
    em = (lanef >= lo) & (lanef < lo + EXPERTS_PER_GROUP)
    m1 = jnp.max(jnp.where(em, lg, -jnp.inf), axis=1, keepdims=True)
    i1 = jnp.min(jnp.where(em & (lg == m1), lanef, big), axis=1, keepdims=True)
    em2 = em & (lanef != i1)
    m2 = jnp.max(jnp.where(em2, lg, -jnp.inf), axis=1, keepdims=True)
    i2 = jnp.min(jnp.where(em2 & (lg == m2), lanef, big), axis=1, keepdims=True)
    e2 = jnp.exp(m2 - m1)
    inv = 1.0 / (1.0 + e2)
    return (jnp.where(lanef == i1, p_g * inv, 0.0) + jnp.where(lanef == i2, p_g * (e2 * inv), 0.0))


def _moe_kernel(x_ref, mod_ref, gpre_ref, gpost_ref, wr_ref, br_ref, wg_ref, wu_ref, wd_ref,
                o_ref, hb_ref, comb_ref, acc_ref):
    j = pl.program_id(1)
    tm = x_ref.shape[0]

    @pl.when(j == 0)
    def _():
        h = _modulated(x_ref[...], gpre_ref[...], mod_ref, 3, 4)
        hb = h.astype(BF16)
        hb_ref[...] = hb
        comb_ref[...] = _router(_dot(hb, wr_ref[...]) + br_ref[...])
        acc_ref[...] = jnp.zeros_like(acc_ref)

    hb = hb_ref[...]
    hgate = _dot(hb, wg_ref[...])
    hup = _dot(hb, wu_ref[...])
    act = hgate * _sigmoid(hgate) * hup
    comb = comb_ref[...]
    lane = lax.broadcasted_iota(I32, (tm, LANES), 1)
    parts = []
    for e in range(MOE_EB):
        sel = lane == (N_GROUPS + j * MOE_EB + e)
        cw = jnp.sum(jnp.where(sel, comb, 0.0), axis=1, keepdims=True)
        parts.append((act[:, e * D_EXPERT:(e + 1) * D_EXPERT] * cw).astype(BF16))
    acc_ref[...] += _dot(jnp.concatenate(parts, axis=1), wd_ref[...])

    @pl.when(j == pl.num_programs(1) - 1)
    def _():
        o_ref[...] = x_ref[...] + mod_ref[5:6, :] * _rms(acc_ref[...], gpost_ref[...])


def _moe(x, mod, gpre, gpost, wr, br, wg, wu, wd):
    s = x.shape[0]
    tm = MOE_ROWS
    bw = MOE_EB * D_EXPERT
    return pl.pallas_call(
        _moe_kernel,
        grid=(s // tm, N_EXPERTS // MOE_EB),
        in_specs=[pl.BlockSpec((tm, D_MODEL), lambda i, j: (i, 0)),
                  _full(mod.shape), _full(gpre.shape), _full(gpost.shape), _full(wr.shape),
                  _full(br.shape),
                  pl.BlockSpec((D_MODEL, bw), lambda i, j: (0, j)),
                  pl.BlockSpec((D_MODEL, bw), lambda i, j: (0, j)),
                  pl.BlockSpec((bw, D_MODEL), lambda i, j: (j, 0))],
        out_specs=pl.BlockSpec((tm, D_MODEL), lambda i, j: (i, 0)),
        out_shape=jax.ShapeDtypeStruct((s, D_MODEL), F32),
        scratch_shapes=[pltpu.VMEM((tm, D_MODEL), BF16), pltpu.VMEM((tm, LANES), F32),
                        pltpu.VMEM((tm, D_MODEL), F32)],
        compiler_params=_params(("arbitrary", "arbitrary")),
        name="moe",
    )(x, mod, gpre, gpost, wr, br, wg, wu, wd)


def _rope_lane_constants():
    rot = DSA_HD // ROT_FRAC
    half = rot // 2
    freqs = np.power(np.float32(ROPE_THETA), -np.arange(half, dtype=np.float32) * np.float32(2.0) / rot)
    j = np.arange(LANES) % DSA_HD
    fv = np.where(j < rot, freqs[j % half], 0.0).astype(np.float32)
    sg = np.where(j < half, -1.0, np.where(j < rot, 1.0, 0.0)).astype(np.float32)
    return jnp.asarray(fv)[None, :], jnp.asarray(sg)[None, :]


def _pad_cols(w, n):
    return jnp.pad(w, ((0, 0), (0, n - w.shape[1])))


def _layer(x, c, pos, w_ada, b_ada, g_pre_mix, g_post_mix, g_pre_ffn, g_post_ffn, w_in, w_gla_a2,
           b_gla_a, g_gla_out, g_idx_k, w_proj_gla, w_proj_dsa, w_out, w_router_g, b_router_g,
           w_router_e, b_router_e, w_e_gate, w_e_up, w_e_down):
    s = x.shape[0]
    mod = _ada(jnp.broadcast_to(c, (8, D_MODEL)), w_ada, b_ada[None, :])[0].reshape(N_MOD, D_MODEL)

    o = np.cumsum((GLA_QK, GLA_QK, GLA_V, GLA_V, GLA_GATE_RANK, DSA_W, DSA_W, DSA_W, IDX_Q, IDX_DIM,
                   IDX_HEADS, 2 * D_MODEL))
    wb = w_in.astype(BF16)
    w_gla = wb[:, 0:o[3]]
    w_ga = _pad_cols(wb[:, o[3]:o[4]], LANES)
    w_dsa = wb[:, o[4]:o[8]]
    w_sm = _pad_cols(jnp.concatenate([wb[:, o[8]:o[9]], wb[:, o[9]:o[10]]], axis=1), LANES)
    w_bg = wb[:, o[10]:o[11]]
    w_a2 = jnp.pad(w_gla_a2.astype(BF16), ((0, LANES - GLA_GATE_RANK), (0, 0)))

    q_a, k_a, v_a, gg, la = _gla_proj(x, mod, g_pre_mix[None, :], w_gla, w_ga, w_a2, b_gla_a[None, :])
    o_a = _gla(q_a, k_a, v_a, gg, la, jnp.tile(g_gla_out, GLA_HEADS)[None, :])

    fv, sg = _rope_lane_constants()
    gik = jnp.pad(g_idx_k, (0, LANES - IDX_DIM))[None, :]
    q_b, k_b, v_b, qi, ki, wi = _dsa_proj(x, mod, g_pre_mix[None, :], pos.astype(F32)[:, None], fv, sg,
                                          w_dsa, w_sm, gik)
    nqb = s // DSA_TQ
    qi_r = qi.reshape(nqb, DSA_TQ, IDX_HEADS, IDX_DIM).transpose(0, 2, 1, 3).reshape(
        nqb, IDX_HEADS * DSA_TQ, IDX_DIM)
    o_b = _dsa(qi_r, ki.T, wi, q_b, k_b.T, v_b, min(DSA_TOPK_MAX, s // 4))

    x1 = _merge(x, mod, g_pre_mix[None, :], g_post_mix[None, :], o_a, o_b, w_bg,
                w_proj_gla.astype(BF16), w_proj_dsa.astype(BF16), w_out.astype(BF16))

    wr = _pad_cols(jnp.concatenate([w_router_g, w_router_e], axis=1).astype(BF16), LANES)
    br = jnp.pad(jnp.concatenate([b_router_g, b_router_e]), (0, LANES - N_GROUPS - N_EXPERTS))[None, :]
    wg = w_e_gate.astype(BF16).transpose(1, 0, 2).reshape(D_MODEL, N_EXPERTS * D_EXPERT)
    wu = w_e_up.astype(BF16).transpose(1, 0, 2).reshape(D_MODEL, N_EXPERTS * D_EXPERT)
    wd = w_e_down.astype(BF16).reshape(N_EXPERTS * D_EXPERT, D_MODEL)
    return _moe(x1, mod, g_pre_ffn[None, :], g_post_ffn[None, :], wr, br, wg, wu, wd)


def kernel(x, c, positions, w_ada, b_ada, g_pre_mix, g_post_mix, g_pre_ffn, g_post_ffn, w_in, w_gla_a2,
           b_gla_a, g_gla_out, g_idx_k, w_proj_gla, w_proj_dsa, w_out, w_router_g, b_router_g,
           w_router_e, b_router_e, w_e_gate, w_e_up, w_e_down):
    batch, depth = x.shape[0], w_ada.shape[0]
    outs = []
    for bi in range(batch):
        xb = x[bi]
        for l in range(depth):
            xb = _layer(xb, c[bi:bi + 1], positions[bi], w_ada[l], b_ada[l], g_pre_mix[l], g_post_mix[l],
                        g_pre_ffn[l], g_post_ffn[l], w_in[l], w_gla_a2[l], b_gla_a[l], g_gla_out[l],
                        g_idx_k[l], w_proj_gla[l], w_proj_dsa[l], w_out[l], w_router_g[l],
                        b_router_g[l], w_router_e[l], b_router_e[l], w_e_gate[l], w_e_up[l],
                        w_e_down[l])
        outs.append(xb)
    return jnp.stack(outs, axis=0)
```

```python
import functools

import numpy as np
import jax
import jax.numpy as jnp
from jax import lax
from jax.experimental import pallas as pl
from jax.experimental.pallas import tpu as pltpu

F32 = jnp.float32
BF16 = jnp.bfloat16
I32 = jnp.int32

D_MODEL = 1024
EPS = 1e-6
ROPE_THETA = 500000.0
ROT_FRAC = 4
GLA_HEADS = 4
GLA_DK = 64
GLA_DV = 128
GLA_GATE_RANK = 16
GLA_TAU = 16.0
GLA_CHUNK = 64
DSA_HEADS = 8
DSA_HD = 64
IDX_HEADS = 8
IDX_DIM = 64
DSA_TOPK_MAX = 256
N_GROUPS = 4
EXPERTS_PER_GROUP = 8
N_EXPERTS = N_GROUPS * EXPERTS_PER_GROUP
D_EXPERT = D_MODEL // 4
N_MOD = 6

GLA_QK = GLA_HEADS * GLA_DK
GLA_V = GLA_HEADS * GLA_DV
DSA_W = DSA_HEADS * DSA_HD
IDX_Q = IDX_HEADS * IDX_DIM

LANES = 128
VMEM_LIMIT_BYTES = 56 * 1024 * 1024
DSA_VMEM_LIMIT_BYTES = 60 * 1024 * 1024

PROJ_ROWS = 512
GLA_ROWS = 512
DSA_TQ = 128
DSA_CH = 1024
MOE_ROWS = 512
MOE_EB = 8
MASKED = -1e30
LOG2E = float(np.log2(np.e))
GLA_LEVELS = 7


def _dot(a, b):
    return jnp.dot(a, b, preferred_element_type=F32)


def _dot_nt(a, b):
    return lax.dot_general(a, b, (((1,), (1,)), ((), ())), preferred_element_type=F32)


def _dot_tn(a, b):
    return lax.dot_general(a, b, (((0,), (0,)), ((), ())), preferred_element_type=F32)


def _sigmoid(x):
    return 1.0 / (1.0 + jnp.exp(-x))


def _rms(x, g):
    ms = jnp.mean(x * x, axis=-1, keepdims=True)
    return x * lax.rsqrt(ms + EPS) * g


def _modulated(x, g, mod_ref, shift_row, scale_row):
    return (_rms(x, g) * (1.0 + mod_ref[scale_row:scale_row + 1, :])
            + mod_ref[shift_row:shift_row + 1, :])


def _params(sem):
    return pltpu.CompilerParams(dimension_semantics=sem, vmem_limit_bytes=VMEM_LIMIT_BYTES)


def _full(shape):
    return pl.BlockSpec(shape, lambda *_: (0,) * len(shape))


def _resident(shape):
    return pl.BlockSpec(shape, lambda *_: (0,) * len(shape), pipeline_mode=pl.Buffered(1))


def _ada_kernel(c_ref, w_ref, b_ref, o_ref):
    c = c_ref[...]
    a = c * _sigmoid(c)
    o_ref[...] = _dot(a.astype(BF16), w_ref[...].astype(BF16)) + b_ref[...]


def _ada(c8, w, b):
    n = w.shape[1]
    bn = 1536
    return pl.pallas_call(
        _ada_kernel,
        grid=(n // bn,),
        in_specs=[_full(c8.shape),
                  pl.BlockSpec((D_MODEL, bn), lambda j: (0, j)),
                  pl.BlockSpec((1, bn), lambda j: (0, j))],
        out_specs=pl.BlockSpec((8, bn), lambda j: (0, j)),
        out_shape=jax.ShapeDtypeStruct((8, n), F32),
        compiler_params=_params(("arbitrary",)),
        name="ada",
    )(c8, w, b)


def _gla_proj_kernel(x_ref, mod_ref, g_ref, w_ref, wga_ref, wa2_ref, ba_ref,
                     q_ref, k_ref, v_ref, gg_ref, la_ref):
    h = _modulated(x_ref[...], g_ref[...], mod_ref, 0, 1)
    hb = h.astype(BF16)
    p = _dot(hb, w_ref[...])
    q_ref[...] = p[:, 0:GLA_QK] * (GLA_DK ** -0.5)
    k_ref[...] = p[:, GLA_QK:2 * GLA_QK]
    v_ref[...] = p[:, 2 * GLA_QK:2 * GLA_QK + GLA_V]
    gg_ref[...] = p[:, 2 * GLA_QK + GLA_V:]
    ga = _dot(hb, wga_ref[...])
    z = _dot(ga.astype(BF16), wa2_ref[...]) + ba_ref[...]
    log_sig = jnp.minimum(z, 0.0) - jnp.log1p(jnp.exp(-jnp.abs(z)))
    la_ref[...] = log_sig * (1.0 / GLA_TAU)


def _gla_proj(x, mod, g, w, wga, wa2, ba):
    s = x.shape[0]
    tm = PROJ_ROWS
    row = lambda n: pl.BlockSpec((tm, n), lambda i: (i, 0))
    return pl.pallas_call(
        _gla_proj_kernel,
        grid=(s // tm,),
        in_specs=[row(D_MODEL), _full(mod.shape), _full(g.shape), _full(w.shape),
                  _full(wga.shape), _full(wa2.shape), _full(ba.shape)],
        out_specs=[row(GLA_QK), row(GLA_QK), row(GLA_V), row(GLA_V), row(GLA_QK)],
        out_shape=[jax.ShapeDtypeStruct((s, GLA_QK), F32), jax.ShapeDtypeStruct((s, GLA_QK), F32),
                   jax.ShapeDtypeStruct((s, GLA_V), F32), jax.ShapeDtypeStruct((s, GLA_V), F32),
                   jax.ShapeDtypeStruct((s, GLA_QK), F32)],
        compiler_params=_params(("arbitrary",)),
        name="gla_proj",
    )(x, mod, g, w, wga, wa2, ba)


def _dsa_proj_kernel(x_ref, mod_ref, g_ref, pos_ref, fv_ref, sg_ref, w_ref, wsm_ref, gik_ref,
                     qb_ref, kb_ref, vb_ref, qi_ref, ki_ref, wi_ref):
    tm = x_ref.shape[0]
    h = _modulated(x_ref[...], g_ref[...], mod_ref, 0, 1)
    hb = h.astype(BF16)
    ang = pos_ref[...] * fv_ref[...]
    cs = jnp.cos(ang)
    sn = jnp.sin(ang) * sg_ref[...]
    lane = lax.broadcasted_iota(I32, (tm, LANES), 1)
    first = (lane & (DSA_HD - 1)) < (DSA_HD // ROT_FRAC // 2)

    def rope(t):
        width = t.shape[1]
        rep = width // LANES
        tile = (lambda a: jnp.concatenate([a] * rep, axis=1)) if rep > 1 else (lambda a: a)
        half = DSA_HD // ROT_FRAC // 2
        fwd = pltpu.roll(t, half, 1)
        bwd = pltpu.roll(t, width - half, 1)
        partner = jnp.where(tile(first), bwd, fwd)
        return t * tile(cs) + partner * tile(sn)

    p = _dot(hb, w_ref[...])
    qb_ref[...] = (rope(p[:, 0:DSA_W]) * (DSA_HD ** -0.5 * LOG2E)).astype(BF16)
    kb_ref[...] = rope(p[:, DSA_W:2 * DSA_W]).astype(BF16)
    vb_ref[...] = p[:, 2 * DSA_W:3 * DSA_W].astype(BF16)
    qi_ref[...] = (rope(p[:, 3 * DSA_W:]) * (IDX_DIM ** -0.5)).astype(BF16)

    sm = _dot(hb, wsm_ref[...])
    is_ik = lane < IDX_DIM
    mu = jnp.sum(jnp.where(is_ik, sm, 0.0), axis=-1, keepdims=True) * (1.0 / IDX_DIM)
    xc = jnp.where(is_ik, sm - mu, 0.0)
    var = jnp.sum(xc * xc, axis=-1, keepdims=True) * (1.0 / IDX_DIM)
    y = xc * lax.rsqrt(var + EPS) * gik_ref[...]
    ki_ref[...] = rope(y)[:, 0:IDX_DIM].astype(BF16)
    wi_ref[...] = sm * (IDX_HEADS ** -0.5)


def _dsa_proj(x, mod, g, pos, fv, sg, w, wsm, gik):
    s = x.shape[0]
    tm = PROJ_ROWS
    row = lambda n: pl.BlockSpec((tm, n), lambda i: (i, 0))
    return pl.pallas_call(
        _dsa_proj_kernel,
        grid=(s // tm,),
        in_specs=[row(D_MODEL), _full(mod.shape), _full(g.shape), row(1), _full(fv.shape),
                  _full(sg.shape), _full(w.shape), _full(wsm.shape), _full(gik.shape)],
        out_specs=[row(DSA_W), row(DSA_W), row(DSA_W), row(IDX_Q), row(IDX_DIM), row(LANES)],
        out_shape=[jax.ShapeDtypeStruct((s, DSA_W), BF16), jax.ShapeDtypeStruct((s, DSA_W), BF16),
                   jax.ShapeDtypeStruct((s, DSA_W), BF16), jax.ShapeDtypeStruct((s, IDX_Q), BF16),
                   jax.ShapeDtypeStruct((s, IDX_DIM), BF16), jax.ShapeDtypeStruct((s, LANES), F32)],
        compiler_params=_params(("arbitrary",)),
        name="dsa_proj",
    )(x, mod, g, pos, fv, sg, w, wsm, gik)


def _gla_constants():
    c = GLA_CHUNK
    tril = np.tril(np.ones((c, c), np.float32))
    t = np.arange(c)
    mats = [tril]
    masks = [(t[:, None] == t[None, :])]
    for hs in (32, 16, 8, 4, 2, 1):
        blk = 2 * hs
        r = (t // blk) * blk + hs - 1
        mats.append(tril - tril[r, :])
        same = (t[:, None] // blk) == (t[None, :] // blk)
        masks.append(same & ((t[:, None] % blk) >= hs) & ((t[None, :] % blk) < hs))
    m_all = np.concatenate(mats, axis=0)
    lvl = np.stack([np.tile(m.astype(np.float32), (1, GLA_HEADS)) for m in masks])
    hrow = np.arange(GLA_HEADS * c) // c
    wmask = (hrow[:, None] == (np.arange(GLA_QK) // GLA_DK)[None, :]).astype(np.float32)
    vmask = (hrow[:, None] == (np.arange(GLA_V) // GLA_DV)[None, :]).astype(np.float32)
    smask = ((np.arange(GLA_V) // GLA_DV)[:, None] == (np.arange(GLA_QK) // GLA_DK)[None, :])
    return (jnp.asarray(m_all, BF16), jnp.asarray(lvl), jnp.asarray(wmask), jnp.asarray(vmask),
            jnp.asarray(smask.astype(np.float32)))


def _gla_kernel(q_ref, k_ref, v_ref, gg_ref, la_ref, gout_ref, mall_ref, lvl_ref, wmask_ref,
                vmask_ref, smask_ref, o_ref, st_ref):
    c = GLA_CHUNK

    @pl.when(pl.program_id(0) == 0)
    def _():
        st_ref[...] = jnp.zeros_like(st_ref)

    m_all = mall_ref[...]
    wmask = wmask_ref[...]
    vmask = vmask_ref[...]
    smask = smask_ref[...]
    gout = gout_ref[...]

    def chunk(ci, carry):
        r0 = pl.multiple_of(ci * c, c)
        rows = pl.ds(r0, c)
        q = q_ref[rows, :]
        k = k_ref[rows, :]
        v = v_ref[rows, :]
        la = la_ref[rows, :]
        hi = la.astype(BF16)
        r1 = la - hi.astype(F32)
        mid = r1.astype(BF16)
        lo = (r1 - mid.astype(F32)).astype(BF16)
        dall = _dot(m_all, hi) + _dot(m_all, mid) + _dot(m_all, lo)
        b = dall[0:c]
        b_last = b[c - 1:c, :]
        qhat = q * jnp.exp(b)
        khat = k * jnp.exp(b_last - b)

        a = jnp.zeros((c, GLA_HEADS * c), F32)
        for lv in range(GLA_LEVELS):
            if lv == 0:
                qt, kt = q, k
            else:
                d = dall[lv * c:(lv + 1) * c]
                qt = q * jnp.exp(jnp.minimum(d, 0.0))
                kt = k * jnp.exp(jnp.minimum(-d, 0.0))
            w = (jnp.concatenate([kt] * GLA_HEADS, axis=0) * wmask).astype(BF16)
            a = a + lvl_ref[lv] * _dot_nt(qt.astype(BF16), w)

        st = st_ref[...]
        vbd = (jnp.concatenate([v] * GLA_HEADS, axis=0) * vmask).astype(BF16)
        o = _dot(a.astype(BF16), vbd) + _dot_nt(qhat.astype(BF16), st.astype(BF16))
        st_ref[...] = st * jnp.exp(b_last) + smask * _dot_tn(v.astype(BF16), khat.astype(BF16))

        parts = []
        for hh in range(GLA_HEADS):
            oh = o[:, hh * GLA_DV:(hh + 1) * GLA_DV]
            ms = jnp.mean(oh * oh, axis=-1, keepdims=True)
            parts.append(oh * lax.rsqrt(ms + EPS))
        gg = gg_ref[rows, :]
        o_ref[rows, :] = jnp.concatenate(parts, axis=1) * gout * (gg * _sigmoid(gg))
        return carry

    lax.fori_loop(0, q_ref.shape[0] // c, chunk, 0, unroll=4)


def _gla(q, k, v, gg, la, gout):
    s = q.shape[0]
    tb = GLA_ROWS
    consts = _gla_constants()
    row = lambda n: pl.BlockSpec((tb, n), lambda i: (i, 0))
    return pl.pallas_call(
        _gla_kernel,
        grid=(s // tb,),
        in_specs=[row(GLA_QK), row(GLA_QK), row(GLA_V), row(GLA_V), row(GLA_QK), _full(gout.shape)]
                 + [_full(a.shape) for a in consts],
        out_specs=row(GLA_V),
        out_shape=jax.ShapeDtypeStruct((s, GLA_V), F32),
        scratch_shapes=[pltpu.VMEM((GLA_V, GLA_QK), F32)],
        compiler_params=_params(("arbitrary",)),
        name="gla",
    )(q, k, v, gg, la, gout, *consts)


DSA_SLAB = 32
DSA_MAX_SHIFT = 48.0
WORD = 32
DSA_GROUP = WORD * LANES


def _float_to_ordered_bits(x):
    bits = lax.bitcast_convert_type(x, I32)
    key = bits ^ ((bits >> 31) & jnp.int32(0x7FFFFFFF))
    return key ^ jnp.int32(-2147483648)


def _transpose_bits(a):
    a = list(a)
    j, msk = 16, 0x0000FFFF
    while j:
        m32 = jnp.int32(np.array(msk, np.uint32).view(np.int32))
        k = 0
        while k < WORD:
            t = (a[k] ^ (a[k + j] >> j)) & m32
            a[k] = a[k] ^ t
            a[k + j] = a[k + j] ^ (t << j)
            k = (k + j + 1) & ~j
        j >>= 1
        msk = (msk ^ (msk << j)) & 0xFFFFFFFF
    return a


def _dsa_kernel(qi_ref, kit_ref, wi_ref, qb_ref, kt_ref, v_ref, o_ref,
                key_ref, alive_ref, great_ref, wb_ref, qbd_ref, s_ref, p_ref, bias_ref, m_ref, l_ref,
                alpha_ref, acc_ref, bnd_ref, kmax_ref, *, seq, topk):
    tq = DSA_TQ
    ch = DSA_CH
    hg = 4
    gw = hg * DSA_HD
    bpc = ch // LANES
    cpg = DSA_GROUP // ch
    n_groups = seq // DSA_GROUP
    i = pl.program_id(0)
    n_ch = ((i + 1) * tq + ch - 1) // ch
    lane_i = lax.broadcasted_iota(I32, (tq, LANES), 1)
    row_pos = lax.broadcasted_iota(I32, (tq, 1), 0) + i * tq
    kf = float(topk)

    def tile_l(a, width):
        return jnp.concatenate([a] * (width // a.shape[1]), axis=1)

    def chunk(c):
        return pl.ds(pl.multiple_of(c * ch, ch), ch)

    def lane_sum(a):
        return jnp.sum(a.astype(F32), axis=1, keepdims=True)

    @pl.when(i == 0)
    def _():
        key_ref[...] = jnp.zeros(key_ref.shape, I32)

        def norm_body(c, best):
            out = []
            for h in range(DSA_HEADS):
                kf32 = kt_ref[h * DSA_HD:(h + 1) * DSA_HD, chunk(c)].astype(F32)
                out.append(jnp.maximum(best[h], jnp.sum(kf32 * kf32, axis=0, keepdims=True)))
            return tuple(out)

        best = lax.fori_loop(0, seq // ch, norm_body,
                             tuple(jnp.zeros((1, ch), F32) for _ in range(DSA_HEADS)))
        for h in range(DSA_HEADS):
            kmax_ref[h] = jnp.sqrt(jnp.max(best[h]))

    wv = wi_ref[...]
    for h in range(IDX_HEADS):
        wb_ref[h] = jnp.broadcast_to(wv[:, IDX_DIM + h:IDX_DIM + h + 1], (tq, LANES))
    qi = qi_ref[0]

    def score_body(c, carry):
        lg = _dot(qi, kit_ref[:, chunk(c)])
        sc = jnp.maximum(lg[0:tq], 0.0) * tile_l(wb_ref[0], ch)
        for h in range(1, IDX_HEADS):
            sc = sc + jnp.maximum(lg[h * tq:(h + 1) * tq], 0.0) * tile_l(wb_ref[h], ch)
        sc = jnp.where(sc == 0.0, 0.0, sc)
        key_ref[c] = _float_to_ordered_bits(sc)
        return carry

    lax.fori_loop(0, n_ch, score_body, 0)

    n_pg = (n_ch + cpg - 1) // cpg

    def plane_body(t, carry):
        g = t // (tq // 8)
        rows = pl.ds(pl.multiple_of((t % (tq // 8)) * 8, 8), 8)

        def slot(b):
            return (g * cpg + b // bpc, rows, slice((b % bpc) * LANES, (b % bpc + 1) * LANES))

        planes = _transpose_bits([key_ref[slot(WORD - 1 - k)] for k in range(WORD)])
        for b in range(WORD):
            key_ref[slot(b)] = planes[b]
        return carry

    lax.fori_loop(0, n_pg * (tq // 8), plane_body, 0)

    def index_mask(g, bound):
        r = bound - g * DSA_GROUP - lane_i
        q = jnp.clip((r + (LANES - 1)) >> 7, 0, WORD)
        return jnp.where(q >= WORD, jnp.int32(-1), (jnp.int32(1) << jnp.minimum(q, WORD - 1)) - 1)

    for g in range(n_groups):
        alive_ref[g] = index_mask(g, row_pos + 1)
        great_ref[g] = jnp.zeros((tq, LANES), I32)

    def select_bits(ng):
        def bit_body(p, cnt_great):
            pc = p // bpc
            pl0 = pl.multiple_of((p % bpc) * LANES, LANES)
            ones = []
            acc = jnp.zeros((tq, LANES), I32)
            for g in range(ng):
                x = alive_ref[g] & key_ref[g * cpg + pc, :, pl.ds(pl0, LANES)]
                acc = acc + lax.population_count(x)
                ones.append(x)
            cnt_one = lane_sum(acc)
            take = (cnt_great + cnt_one) >= kf
            take_b = jnp.broadcast_to(take, (tq, LANES))
            for g in range(ng):
                a = alive_ref[g]
                alive_ref[g] = jnp.where(take_b, ones[g], a ^ ones[g])
                great_ref[g] = jnp.where(take_b, great_ref[g], great_ref[g] | ones[g])
            return jnp.where(take, cnt_great, cnt_great + cnt_one)

        return lambda: lax.fori_loop(0, WORD, bit_body, jnp.zeros((tq, 1), F32))

    cnt_great = lax.switch(n_pg - 1, [select_bits(ng) for ng in range(1, n_groups + 1)])
    need = kf - cnt_great

    def count_alive(bound):
        acc = jnp.zeros((tq, LANES), I32)
        for g in range(n_groups):
            acc = acc + lax.population_count(alive_ref[g] & index_mask(g, bound))
        return lane_sum(acc)

    tie_row = count_alive(jnp.full((tq, 1), seq, I32)) > need

    def tie_break(_):
        nbits = max(1, int(np.ceil(np.log2(seq))))

        def jbit(p, m):
            cand = m | jnp.left_shift(jnp.int32(1), nbits - 1 - p)
            return jnp.where(count_alive(cand) < need, cand, m)

        m = lax.fori_loop(0, nbits, jbit, jnp.zeros((tq, 1), I32))
        return jnp.where(tie_row, m + 1, jnp.int32(seq))

    any_tie = jnp.max(jnp.where(tie_row, 1.0, 0.0)) > 0.0
    bound = lax.cond(any_tie, tie_break, lambda _: jnp.full((tq, 1), seq, I32), 0)
    for g in range(n_groups):
        great_ref[g] = great_ref[g] | (alive_ref[g] & index_mask(g, bound))

    qt = qb_ref[...]
    head_of_lane = lax.broadcasted_iota(I32, (tq, gw), 1) >> 6
    for g in range(2):
        qg = qt[:, g * gw:(g + 1) * gw]
        for h in range(hg):
            qbd_ref[g, h * tq:(h + 1) * tq, :] = jnp.where(head_of_lane == h, qg, jnp.zeros_like(qg))
    l_ref[...] = jnp.zeros(l_ref.shape, F32)
    acc_ref[...] = jnp.zeros(acc_ref.shape, F32)

    for g in range(2):
        qf = qbd_ref[g].astype(F32)
        qn = jnp.sqrt(jnp.sum(qf * qf, axis=1, keepdims=True))
        kn = jnp.concatenate([jnp.full((tq, 1), kmax_ref[g * hg + h], F32) for h in range(hg)], axis=0)
        bnd_ref[g] = jnp.broadcast_to(qn * kn, (hg * tq, LANES))
    bound_ok = jnp.max(bnd_ref[...]) < DSA_MAX_SHIFT

    def bias_of(c):
        sel = great_ref[c // cpg]
        for jj in range(bpc):
            bit = (sel >> ((c % cpg) * bpc + jj)) & 1
            bias_ref[:, jj * LANES:(jj + 1) * LANES] = jnp.where(bit != 0, 0.0, MASKED)

    def bounded_body(c, carry):
        bias_of(c)
        for g in range(2):
            s_ref[g] = _dot(qbd_ref[g], kt_ref[g * gw:(g + 1) * gw, chunk(c)])
        for g in range(2):
            for r in range(hg * tq // DSA_SLAB):
                rows = slice(r * DSA_SLAB, (r + 1) * DSA_SLAB)
                b0 = (r * DSA_SLAB) % tq
                s = s_ref[g, rows, :] + bias_ref[b0:b0 + DSA_SLAB, :]
                p = jnp.exp2(s - tile_l(bnd_ref[g, rows, :], ch))
                l_ref[g, rows, :] = l_ref[g, rows, :] + jnp.sum(p, axis=1, keepdims=True)
                p_ref[g, rows, :] = p.astype(BF16)
            acc_ref[g] = acc_ref[g] + _dot(p_ref[g], v_ref[chunk(c), g * gw:(g + 1) * gw])
        return carry

    def running_max_body(c, carry):
        bias_of(c)
        for g in range(2):
            s_ref[g] = _dot(qbd_ref[g], kt_ref[g * gw:(g + 1) * gw, chunk(c)])
        for g in range(2):
            for r in range(hg * tq // DSA_SLAB):
                rows = slice(r * DSA_SLAB, (r + 1) * DSA_SLAB)
                b0 = (r * DSA_SLAB) % tq
                s = s_ref[g, rows, :] + bias_ref[b0:b0 + DSA_SLAB, :]
                m_prev = m_ref[g, rows, :]
                m_new = jnp.maximum(m_prev, jnp.max(s, axis=1, keepdims=True))
                alpha = jnp.exp2(m_prev - m_new)
                p = jnp.exp2(s - tile_l(m_new, ch))
                l_ref[g, rows, :] = alpha * l_ref[g, rows, :] + jnp.sum(p, axis=1, keepdims=True)
                m_ref[g, rows, :] = m_new
                alpha_ref[g, rows, :] = alpha
                p_ref[g, rows, :] = p.astype(BF16)
            acc_ref[g] = (acc_ref[g] * tile_l(alpha_ref[g], gw)
                          + _dot(p_ref[g], v_ref[chunk(c), g * gw:(g + 1) * gw]))
        return carry

    @pl.when(bound_ok)
    def _():
        lax.fori_loop(0, n_ch, bounded_body, 0)

    @pl.when(jnp.logical_not(bound_ok))
    def _():
        m_ref[...] = jnp.full(m_ref.shape, MASKED, F32)
        lax.fori_loop(0, n_ch, running_max_body, 0)

    outs = []
    for g in range(2):
        a = acc_ref[g] * tile_l(1.0 / l_ref[g], gw)
        og = jnp.zeros((tq, gw), F32)
        for h in range(hg):
            og = og + jnp.where(head_of_lane == h, a[h * tq:(h + 1) * tq], 0.0)
        outs.append(og)
    o_ref[...] = jnp.concatenate(outs, axis=1)


def _dsa(qi_r, kit, wi, qb, kt, v, topk):
    s = qb.shape[0]
    tq = DSA_TQ
    assert s % DSA_GROUP == 0 and DSA_GROUP % DSA_CH == 0
    rows4 = 4 * tq
    return pl.pallas_call(
        functools.partial(_dsa_kernel, seq=s, topk=topk),
        grid=(s // tq,),
        in_specs=[pl.BlockSpec((1, IDX_HEADS * tq, IDX_DIM), lambda i: (i, 0, 0)),
                  _resident(kit.shape),
                  pl.BlockSpec((tq, LANES), lambda i: (i, 0)),
                  pl.BlockSpec((tq, DSA_W), lambda i: (i, 0)),
                  _resident(kt.shape),
                  _resident(v.shape)],
        out_specs=pl.BlockSpec((tq, DSA_W), lambda i: (i, 0)),
        out_shape=jax.ShapeDtypeStruct((s, DSA_W), F32),
        scratch_shapes=[pltpu.VMEM((s // DSA_CH, tq, DSA_CH), I32),
                        pltpu.VMEM((s // DSA_GROUP, tq, LANES), I32),
                        pltpu.VMEM((s // DSA_GROUP, tq, LANES), I32),
                        pltpu.VMEM((IDX_HEADS, tq, LANES), F32),
                        pltpu.VMEM((2, rows4, 4 * DSA_HD), BF16),
                        pltpu.VMEM((2, rows4, DSA_CH), F32),
                        pltpu.VMEM((2, rows4, DSA_CH), BF16),
                        pltpu.VMEM((tq, DSA_CH), F32),
                        pltpu.VMEM((2, rows4, LANES), F32),
                        pltpu.VMEM((2, rows4, LANES), F32),
                        pltpu.VMEM((2, rows4, LANES), F32),
                        pltpu.VMEM((2, rows4, 4 * DSA_HD), F32),
                        pltpu.VMEM((2, rows4, LANES), F32),
                        pltpu.SMEM((DSA_HEADS,), F32)],
        compiler_params=pltpu.CompilerParams(dimension_semantics=("arbitrary",),
                                             vmem_limit_bytes=DSA_VMEM_LIMIT_BYTES),
        name="dsa",
    )(qi_r, kit, wi, qb, kt, v)


def _merge_kernel(x_ref, mod_ref, gpre_ref, gpost_ref, oa_ref, ob_ref, wbg_ref, wpg_ref, wpd_ref,
                  wout_ref, o_ref):
    x = x_ref[...]
    h = _modulated(x, gpre_ref[...], mod_ref, 0, 1)
    gates = _sigmoid(_dot(h.astype(BF16), wbg_ref[...]))
    yg = _dot(oa_ref[...].astype(BF16), wpg_ref[...])
    yd = _dot(ob_ref[...].astype(BF16), wpd_ref[...])
    mix = gates[:, 0:D_MODEL] * yg + gates[:, D_MODEL:] * yd
    out = _dot(mix.astype(BF16), wout_ref[...])
    o_ref[...] = x + mod_ref[2:3, :] * _rms(out, gpost_ref[...])


def _merge(x, mod, gpre, gpost, oa, ob, wbg, wpg, wpd, wout):
    s = x.shape[0]
    tm = PROJ_ROWS
    row = lambda n: pl.BlockSpec((tm, n), lambda i: (i, 0))
    return pl.pallas_call(
        _merge_kernel,
        grid=(s // tm,),
        in_specs=[row(D_MODEL), _full(mod.shape), _full(gpre.shape), _full(gpost.shape),
                  row(GLA_V), row(DSA_W), _full(wbg.shape), _full(wpg.shape), _full(wpd.shape),
                  _full(wout.shape)],
        out_specs=row(D_MODEL),
        out_shape=jax.ShapeDtypeStruct((s, D_MODEL), F32),
        compiler_params=_params(("arbitrary",)),
        name="merge",
    )(x, mod, gpre, gpost, oa, ob, wbg, wpg, wpd, wout)


def _router(lg):
    t = lg.shape[0]
    lane = lax.broadcasted_iota(I32, (t, LANES), 1)
    lanef = lane.astype(F32)
    big = 1e9
    gm = lane < N_GROUPS
    gmax = jnp.max(jnp.where(gm, lg, -jnp.inf), axis=1, keepdims=True)
    gsum = jnp.sum(jnp.where(gm, jnp.exp(lg - gmax), 0.0), axis=1, keepdims=True)
    p_g = 1.0 / gsum
    g_sel = jnp.min(jnp.where(gm & (lg == gmax), lanef, big), axis=1, keepdims=True)
    lo = N_GROUPS + EXPERTS_PER_GROUP * g_sel
    em = (lanef >= lo) & (lanef < lo + EXPERTS_PER_GROUP)
    m1 = jnp.max(jnp.where(em, lg, -jnp.inf), axis=1, keepdims=True)
    i1 = jnp.min(jnp.where(em & (lg == m1), lanef, big), axis=1, keepdims=True)
    em2 = em & (lanef != i1)
    m2 = jnp.max(jnp.where(em2, lg, -jnp.inf), axis=1, keepdims=True)
    i2 = jnp.min(jnp.where(em2 & (lg == m2), lanef, big), axis=1, keepdims=True)
    e2 = jnp.exp(m2 - m1)
    inv = 1.0 / (1.0 + e2)
    return (jnp.where(lanef == i1, p_g * inv, 0.0) + jnp.where(lanef == i2, p_g * (e2 * inv), 0.0))


def _moe_kernel(x_ref, mod_ref, gpre_ref, gpost_ref, wr_ref, br_ref, wg_ref, wu_ref, wd_ref,
                o_ref, hb_ref, comb_ref, acc_ref):
    j = pl.program_id(1)
    tm = x_ref.shape[0]

    @pl.when(j == 0)
    def _():
        h = _modulated(x_ref[...], gpre_ref[...], mod_ref, 3, 4)
        hb = h.astype(BF16)
        hb_ref[...] = hb
        comb_ref[...] = _router(_dot(hb, wr_ref[...]) + br_ref[...])
        acc_ref[...] = jnp.zeros_like(acc_ref)

    hb = hb_ref[...]
    hgate = _dot(hb, wg_ref[...])
    hup = _dot(hb, wu_ref[...])
    act = hgate * _sigmoid(hgate) * hup
    comb = comb_ref[...]
    lane = lax.broadcasted_iota(I32, (tm, LANES), 1)
    parts = []
    for e in range(MOE_EB):
        sel = lane == (N_GROUPS + j * MOE_EB + e)
        cw = jnp.sum(jnp.where(sel, comb, 0.0), axis=1, keepdims=True)
        parts.append((act[:, e * D_EXPERT:(e + 1) * D_EXPERT] * cw).astype(BF16))
    acc_ref[...] += _dot(jnp.concatenate(parts, axis=1), wd_ref[...])

    @pl.when(j == pl.num_programs(1) - 1)
    def _():
        o_ref[...] = x_ref[...] + mod_ref[5:6, :] * _rms(acc_ref[...], gpost_ref[...])


def _moe(x, mod, gpre, gpost, wr, br, wg, wu, wd):
    s = x.shape[0]
    tm = MOE_ROWS
    bw = MOE_EB * D_EXPERT
    return pl.pallas_call(
        _moe_kernel,
        grid=(s // tm, N_EXPERTS // MOE_EB),
        in_specs=[pl.BlockSpec((tm, D_MODEL), lambda i, j: (i, 0)),
                  _full(mod.shape), _full(gpre.shape), _full(gpost.shape), _full(wr.shape),
                  _full(br.shape),
                  pl.BlockSpec((D_MODEL, bw), lambda i, j: (0, j)),
                  pl.BlockSpec((D_MODEL, bw), lambda i, j: (0, j)),
                  pl.BlockSpec((bw, D_MODEL), lambda i, j: (j, 0))],
        out_specs=pl.BlockSpec((tm, D_MODEL), lambda i, j: (i, 0)),
        out_shape=jax.ShapeDtypeStruct((s, D_MODEL), F32),
        scratch_shapes=[pltpu.VMEM((tm, D_MODEL), BF16), pltpu.VMEM((tm, LANES), F32),
                        pltpu.VMEM((tm, D_MODEL), F32)],
        compiler_params=_params(("arbitrary", "arbitrary")),
        name="moe",
    )(x, mod, gpre, gpost, wr, br, wg, wu, wd)


def _rope_lane_constants():
    rot = DSA_HD // ROT_FRAC
    half = rot // 2
    freqs = np.power(np.float32(ROPE_THETA), -np.arange(half, dtype=np.float32) * np.float32(2.0) / rot)
    j = np.arange(LANES) % DSA_HD
    fv = np.where(j < rot, freqs[j % half], 0.0).astype(np.float32)
    sg = np.where(j < half, -1.0, np.where(j < rot, 1.0, 0.0)).astype(np.float32)
    return jnp.asarray(fv)[None, :], jnp.asarray(sg)[None, :]


def _pad_cols(w, n):
    return jnp.pad(w, ((0, 0), (0, n - w.shape[1])))


def _layer(x, c, pos, w_ada, b_ada, g_pre_mix, g_post_mix, g_pre_ffn, g_post_ffn, w_in, w_gla_a2,
           b_gla_a, g_gla_out, g_idx_k, w_proj_gla, w_proj_dsa, w_out, w_router_g, b_router_g,
           w_router_e, b_router_e, w_e_gate, w_e_up, w_e_down):
    s = x.shape[0]
    mod = _ada(jnp.broadcast_to(c, (8, D_MODEL)), w_ada, b_ada[None, :])[0].reshape(N_MOD, D_MODEL)

    o = np.cumsum((GLA_QK, GLA_QK, GLA_V, GLA_V, GLA_GATE_RANK, DSA_W, DSA_W, DSA_W, IDX_Q, IDX_DIM,
                   IDX_HEADS, 2 * D_MODEL))
    wb = w_in.astype(BF16)
    w_gla = wb[:, 0:o[3]]
    w_ga = _pad_cols(wb[:, o[3]:o[4]], LANES)
    w_dsa = wb[:, o[4]:o[8]]
    w_sm = _pad_cols(jnp.concatenate([wb[:, o[8]:o[9]], wb[:, o[9]:o[10]]], axis=1), LANES)
    w_bg = wb[:, o[10]:o[11]]
    w_a2 = jnp.pad(w_gla_a2.astype(BF16), ((0, LANES - GLA_GATE_RANK), (0, 0)))

    q_a, k_a, v_a, gg, la = _gla_proj(x, mod, g_pre_mix[None, :], w_gla, w_ga, w_a2, b_gla_a[None, :])
    o_a = _gla(q_a, k_a, v_a, gg, la, jnp.tile(g_gla_out, GLA_HEADS)[None, :])

    fv, sg = _rope_lane_constants()
    gik = jnp.pad(g_idx_k, (0, LANES - IDX_DIM))[None, :]
    q_b, k_b, v_b, qi, ki, wi = _dsa_proj(x, mod, g_pre_mix[None, :], pos.astype(F32)[:, None], fv, sg,
                                          w_dsa, w_sm, gik)
    nqb = s // DSA_TQ
    qi_r = qi.reshape(nqb, DSA_TQ, IDX_HEADS, IDX_DIM).transpose(0, 2, 1, 3).reshape(
        nqb, IDX_HEADS * DSA_TQ, IDX_DIM)
    o_b = _dsa(qi_r, ki.T, wi, q_b, k_b.T, v_b, min(DSA_TOPK_MAX, s // 4))

    x1 = _merge(x, mod, g_pre_mix[None, :], g_post_mix[None, :], o_a, o_b, w_bg,
                w_proj_gla.astype(BF16), w_proj_dsa.astype(BF16), w_out.astype(BF16))

    wr = _pad_cols(jnp.concatenate([w_router_g, w_router_e], axis=1).astype(BF16), LANES)
    br = jnp.pad(jnp.concatenate([b_router_g, b_router_e]), (0, LANES - N_GROUPS - N_EXPERTS))[None, :]
    wg = w_e_gate.astype(BF16).transpose(1, 0, 2).reshape(D_MODEL, N_EXPERTS * D_EXPERT)
    wu = w_e_up.astype(BF16).transpose(1, 0, 2).reshape(D_MODEL, N_EXPERTS * D_EXPERT)
    wd = w_e_down.astype(BF16).reshape(N_EXPERTS * D_EXPERT, D_MODEL)
    return _moe(x1, mod, g_pre_ffn[None, :], g_post_ffn[None, :], wr, br, wg, wu, wd)


def kernel(x, c, positions, w_ada, b_ada, g_pre_mix, g_post_mix, g_pre_ffn, g_post_ffn, w_in, w_gla_a2,
           b_gla_a, g_gla_out, g_idx_k, w_proj_gla, w_proj_dsa, w_out, w_router_g, b_router_g,
           w_router_e, b_router_e, w_e_gate, w_e_up, w_e_down):
    batch, depth = x.shape[0], w_ada.shape[0]
    outs = []
    for bi in range(batch):
        xb = x[bi]
        for l in range(depth):
            xb = _layer(xb, c[bi:bi + 1], positions[bi], w_ada[l], b_ada[l], g_pre_mix[l], g_post_mix[l],
                        g_pre_ffn[l], g_post_ffn[l], w_in[l], w_gla_a2[l], b_gla_a[l], g_gla_out[l],
                        g_idx_k[l], w_proj_gla[l], w_proj_dsa[l], w_out[l], w_router_g[l],
                        b_router_g[l], w_router_e[l], b_router_e[l], w_e_gate[l], w_e_up[l],
                        w_e_down[l])
        outs.append(xb)
    return jnp.stack(outs, axis=0)
```

```python
import functools

import numpy as np
import jax
import jax.numpy as jnp
from jax import lax
from jax.experimental import pallas as pl
from jax.experimental.pallas import tpu as pltpu

F32 = jnp.float32
BF16 = jnp.bfloat16
I32 = jnp.int32

D_MODEL = 1024
EPS = 1e-6
ROPE_THETA = 500000.0
ROT_FRAC = 4
GLA_HEADS = 4
GLA_DK = 64
GLA_DV = 128
GLA_GATE_RANK = 16
GLA_TAU = 16.0
GLA_CHUNK = 64
DSA_HEADS = 8
DSA_HD = 64
IDX_HEADS = 8
IDX_DIM = 64
DSA_TOPK_MAX = 256
N_GROUPS = 4
EXPERTS_PER_GROUP = 8
N_EXPERTS = N_GROUPS * EXPERTS_PER_GROUP
D_EXPERT = D_MODEL // 4
N_MOD = 6

GLA_QK = GLA_HEADS * GLA_DK
GLA_V = GLA_HEADS * GLA_DV
DSA_W = DSA_HEADS * DSA_HD
IDX_Q = IDX_HEADS * IDX_DIM

LANES = 128
VMEM_LIMIT_BYTES = 56 * 1024 * 1024
DSA_VMEM_LIMIT_BYTES = 60 * 1024 * 1024

PROJ_ROWS = 512
GLA_ROWS = 512
DSA_TQ = 128
DSA_CH = 1024
MOE_ROWS = 512
MOE_EB = 8
MASKED = -1e30
LOG2E = float(np.log2(np.e))
GLA_LEVELS = 7


def _dot(a, b):
    return jnp.dot(a, b, preferred_element_type=F32)


def _dot_nt(a, b):
    return lax.dot_general(a, b, (((1,), (1,)), ((), ())), preferred_element_type=F32)


def _dot_tn(a, b):
    return lax.dot_general(a, b, (((0,), (0,)), ((), ())), preferred_element_type=F32)


def _sigmoid(x):
    return 1.0 / (1.0 + jnp.exp(-x))


def _rms(x, g):
    ms = jnp.mean(x * x, axis=-1, keepdims=True)
    return x * lax.rsqrt(ms + EPS) * g


def _modulated(x, g, mod_ref, shift_row, scale_row):
    return (_rms(x, g) * (1.0 + mod_ref[scale_row:scale_row + 1, :])
            + mod_ref[shift_row:shift_row + 1, :])


def _params(sem):
    return pltpu.CompilerParams(dimension_semantics=sem, vmem_limit_bytes=VMEM_LIMIT_BYTES)


def _full(shape):
    return pl.BlockSpec(shape, lambda *_: (0,) * len(shape))


def _resident(shape):
    return pl.BlockSpec(shape, lambda *_: (0,) * len(shape), pipeline_mode=pl.Buffered(1))


def _ada_kernel(c_ref, w_ref, b_ref, o_ref):
    c = c_ref[...]
    a = c * _sigmoid(c)
    o_ref[...] = _dot(a.astype(BF16), w_ref[...].astype(BF16)) + b_ref[...]


def _ada(c8, w, b):
    n = w.shape[1]
    bn = 1536
    return pl.pallas_call(
        _ada_kernel,
        grid=(n // bn,),
        in_specs=[_full(c8.shape),
                  pl.BlockSpec((D_MODEL, bn), lambda j: (0, j)),
                  pl.BlockSpec((1, bn), lambda j: (0, j))],
        out_specs=pl.BlockSpec((8, bn), lambda j: (0, j)),
        out_shape=jax.ShapeDtypeStruct((8, n), F32),
        compiler_params=_params(("arbitrary",)),
        name="ada",
    )(c8, w, b)


def _gla_proj_kernel(x_ref, mod_ref, g_ref, w_ref, wga_ref, wa2_ref, ba_ref,
                     q_ref, k_ref, v_ref, gg_ref, la_ref):
    h = _modulated(x_ref[...], g_ref[...], mod_ref, 0, 1)
    hb = h.astype(BF16)
    p = _dot(hb, w_ref[...])
    q_ref[...] = p[:, 0:GLA_QK] * (GLA_DK ** -0.5)
    k_ref[...] = p[:, GLA_QK:2 * GLA_QK]
    v_ref[...] = p[:, 2 * GLA_QK:2 * GLA_QK + GLA_V]
    gg_ref[...] = p[:, 2 * GLA_QK + GLA_V:]
    ga = _dot(hb, wga_ref[...])
    z = _dot(ga.astype(BF16), wa2_ref[...]) + ba_ref[...]
    log_sig = jnp.minimum(z, 0.0) - jnp.log1p(jnp.exp(-jnp.abs(z)))
    la_ref[...] = log_sig * (1.0 / GLA_TAU)


def _gla_proj(x, mod, g, w, wga, wa2, ba):
    s = x.shape[0]
    tm = PROJ_ROWS
    row = lambda n: pl.BlockSpec((tm, n), lambda i: (i, 0))
    return pl.pallas_call(
        _gla_proj_kernel,
        grid=(s // tm,),
        in_specs=[row(D_MODEL), _full(mod.shape), _full(g.shape), _full(w.shape),
                  _full(wga.shape), _full(wa2.shape), _full(ba.shape)],
        out_specs=[row(GLA_QK), row(GLA_QK), row(GLA_V), row(GLA_V), row(GLA_QK)],
        out_shape=[jax.ShapeDtypeStruct((s, GLA_QK), F32), jax.ShapeDtypeStruct((s, GLA_QK), F32),
                   jax.ShapeDtypeStruct((s, GLA_V), F32), jax.ShapeDtypeStruct((s, GLA_V), F32),
                   jax.ShapeDtypeStruct((s, GLA_QK), F32)],
        compiler_params=_params(("arbitrary",)),
        name="gla_proj",
    )(x, mod, g, w, wga, wa2, ba)


def _dsa_proj_kernel(x_ref, mod_ref, g_ref, pos_ref, fv_ref, sg_ref, w_ref, wsm_ref, gik_ref,
                     qb_ref, kb_ref, vb_ref, qi_ref, ki_ref, wi_ref):
    tm = x_ref.shape[0]
    h = _modulated(x_ref[...], g_ref[...], mod_ref, 0, 1)
    hb = h.astype(BF16)
    ang = pos_ref[...] * fv_ref[...]
    cs = jnp.cos(ang)
    sn = jnp.sin(ang) * sg_ref[...]
    lane = lax.broadcasted_iota(I32, (tm, LANES), 1)
    first = (lane & (DSA_HD - 1)) < (DSA_HD // ROT_FRAC // 2)

    def rope(t):
        width = t.shape[1]
        rep = width // LANES
        tile = (lambda a: jnp.concatenate([a] * rep, axis=1)) if rep > 1 else (lambda a: a)
        half = DSA_HD // ROT_FRAC // 2
        fwd = pltpu.roll(t, half, 1)
        bwd = pltpu.roll(t, width - half, 1)
        partner = jnp.where(tile(first), bwd, fwd)
        return t * tile(cs) + partner * tile(sn)

    p = _dot(hb, w_ref[...])
    qb_ref[...] = (rope(p[:, 0:DSA_W]) * (DSA_HD ** -0.5 * LOG2E)).astype(BF16)
    kb_ref[...] = rope(p[:, DSA_W:2 * DSA_W]).astype(BF16)
    vb_ref[...] = p[:, 2 * DSA_W:3 * DSA_W].astype(BF16)
    qi_ref[...] = (rope(p[:, 3 * DSA_W:]) * (IDX_DIM ** -0.5)).astype(BF16)

    sm = _dot(hb, wsm_ref[...])
    is_ik = lane < IDX_DIM
    mu = jnp.sum(jnp.where(is_ik, sm, 0.0), axis=-1, keepdims=True) * (1.0 / IDX_DIM)
    xc = jnp.where(is_ik, sm - mu, 0.0)
    var = jnp.sum(xc * xc, axis=-1, keepdims=True) * (1.0 / IDX_DIM)
    y = xc * lax.rsqrt(var + EPS) * gik_ref[...]
    ki_ref[...] = rope(y)[:, 0:IDX_DIM].astype(BF16)
    wi_ref[...] = sm * (IDX_HEADS ** -0.5)


def _dsa_proj(x, mod, g, pos, fv, sg, w, wsm, gik):
    s = x.shape[0]
    tm = PROJ_ROWS
    row = lambda n: pl.BlockSpec((tm, n), lambda i: (i, 0))
    return pl.pallas_call(
        _dsa_proj_kernel,
        grid=(s // tm,),
        in_specs=[row(D_MODEL), _full(mod.shape), _full(g.shape), row(1), _full(fv.shape),
                  _full(sg.shape), _full(w.shape), _full(wsm.shape), _full(gik.shape)],
        out_specs=[row(DSA_W), row(DSA_W), row(DSA_W), row(IDX_Q), row(IDX_DIM), row(LANES)],
        out_shape=[jax.ShapeDtypeStruct((s, DSA_W), BF16), jax.ShapeDtypeStruct((s, DSA_W), BF16),
                   jax.ShapeDtypeStruct((s, DSA_W), BF16), jax.ShapeDtypeStruct((s, IDX_Q), BF16),
                   jax.ShapeDtypeStruct((s, IDX_DIM), BF16), jax.ShapeDtypeStruct((s, LANES), F32)],
        compiler_params=_params(("arbitrary",)),
        name="dsa_proj",
    )(x, mod, g, pos, fv, sg, w, wsm, gik)


def _gla_constants():
    c = GLA_CHUNK
    tril = np.tril(np.ones((c, c), np.float32))
    t = np.arange(c)
    mats = [tril]
    masks = [(t[:, None] == t[None, :])]
    for hs in (32, 16, 8, 4, 2, 1):
        blk = 2 * hs
        r = (t // blk) * blk + hs - 1
        mats.append(tril - tril[r, :])
        same = (t[:, None] // blk) == (t[None, :] // blk)
        masks.append(same & ((t[:, None] % blk) >= hs) & ((t[None, :] % blk) < hs))
    m_all = np.concatenate(mats, axis=0)
    lvl = np.stack([np.tile(m.astype(np.float32), (1, GLA_HEADS)) for m in masks])
    hrow = np.arange(GLA_HEADS * c) // c
    wmask = (hrow[:, None] == (np.arange(GLA_QK) // GLA_DK)[None, :]).astype(np.float32)
    vmask = (hrow[:, None] == (np.arange(GLA_V) // GLA_DV)[None, :]).astype(np.float32)
    smask = ((np.arange(GLA_V) // GLA_DV)[:, None] == (np.arange(GLA_QK) // GLA_DK)[None, :])
    return (jnp.asarray(m_all, BF16), jnp.asarray(lvl), jnp.asarray(wmask), jnp.asarray(vmask),
            jnp.asarray(smask.astype(np.float32)))


def _gla_kernel(q_ref, k_ref, v_ref, gg_ref, la_ref, gout_ref, mall_ref, lvl_ref, wmask_ref,
                vmask_ref, smask_ref, o_ref, st_ref):
    c = GLA_CHUNK

    @pl.when(pl.program_id(0) == 0)
    def _():
        st_ref[...] = jnp.zeros_like(st_ref)

    m_all = mall_ref[...]
    wmask = wmask_ref[...]
    vmask = vmask_ref[...]
    smask = smask_ref[...]
    gout = gout_ref[...]

    def chunk(ci, carry):
        r0 = pl.multiple_of(ci * c, c)
        rows = pl.ds(r0, c)
        q = q_ref[rows, :]
        k = k_ref[rows, :]
        v = v_ref[rows, :]
        la = la_ref[rows, :]
        hi = la.astype(BF16)
        r1 = la - hi.astype(F32)
        mid = r1.astype(BF16)
        lo = (r1 - mid.astype(F32)).astype(BF16)
        dall = _dot(m_all, hi) + _dot(m_all, mid) + _dot(m_all, lo)
        b = dall[0:c]
        b_last = b[c - 1:c, :]
        qhat = q * jnp.exp(b)
        khat = k * jnp.exp(b_last - b)

        a = jnp.zeros((c, GLA_HEADS * c), F32)
        for lv in range(GLA_LEVELS):
            if lv == 0:
                qt, kt = q, k
            else:
                d = dall[lv * c:(lv + 1) * c]
                qt = q * jnp.exp(jnp.minimum(d, 0.0))
                kt = k * jnp.exp(jnp.minimum(-d, 0.0))
            w = (jnp.concatenate([kt] * GLA_HEADS, axis=0) * wmask).astype(BF16)
            a = a + lvl_ref[lv] * _dot_nt(qt.astype(BF16), w)

        st = st_ref[...]
        vbd = (jnp.concatenate([v] * GLA_HEADS, axis=0) * vmask).astype(BF16)
        o = _dot(a.astype(BF16), vbd) + _dot_nt(qhat.astype(BF16), st.astype(BF16))
        st_ref[...] = st * jnp.exp(b_last) + smask * _dot_tn(v.astype(BF16), khat.astype(BF16))

        parts = []
        for hh in range(GLA_HEADS):
            oh = o[:, hh * GLA_DV:(hh + 1) * GLA_DV]
            ms = jnp.mean(oh * oh, axis=-1, keepdims=True)
            parts.append(oh * lax.rsqrt(ms + EPS))
        gg = gg_ref[rows, :]
        o_ref[rows, :] = jnp.concatenate(parts, axis=1) * gout * (gg * _sigmoid(gg))
        return carry

    lax.fori_loop(0, q_ref.shape[0] // c, chunk, 0, unroll=4)


def _gla(q, k, v, gg, la, gout):
    s = q.shape[0]
    tb = GLA_ROWS
    consts = _gla_constants()
    row = lambda n: pl.BlockSpec((tb, n), lambda i: (i, 0))
    return pl.pallas_call(
        _gla_kernel,
        grid=(s // tb,),
        in_specs=[row(GLA_QK), row(GLA_QK), row(GLA_V), row(GLA_V), row(GLA_QK), _full(gout.shape)]
                 + [_full(a.shape) for a in consts],
        out_specs=row(GLA_V),
        out_shape=jax.ShapeDtypeStruct((s, GLA_V), F32),
        scratch_shapes=[pltpu.VMEM((GLA_V, GLA_QK), F32)],
        compiler_params=_params(("arbitrary",)),
        name="gla",
    )(q, k, v, gg, la, gout, *consts)


DSA_SLAB = 32
DSA_HG = 2
DSA_MAX_SHIFT = 48.0
WORD = 32
DSA_GROUP = WORD * LANES
LANE_BITS = LANES.bit_length() - 1
HD_BITS = DSA_HD.bit_length() - 1


def _float_to_ordered_bits(x):
    bits = lax.bitcast_convert_type(x, I32)
    key = bits ^ ((bits >> 31) & jnp.int32(0x7FFFFFFF))
    return key ^ jnp.int32(-2147483648)


def _transpose_bits(a):
    a = list(a)
    j, msk = 16, 0x0000FFFF
    while j:
        m32 = jnp.int32(np.array(msk, np.uint32).view(np.int32))
        k = 0
        while k < WORD:
            t = (a[k] ^ (a[k + j] >> j)) & m32
            a[k] = a[k] ^ t
            a[k + j] = a[k + j] ^ (t << j)
            k = (k + j + 1) & ~j
        j >>= 1
        msk = (msk ^ (msk << j)) & 0xFFFFFFFF
    return a


def _dsa_kernel(qi_ref, kit_ref, wi_ref, qb_ref, kt_ref, v_ref, o_ref,
                key_ref, alive_ref, great_ref, wb_ref, qbd_ref, s_ref, p_ref, bias_ref, m_ref, l_ref,
                alpha_ref, acc_ref, bnd_ref, kmax_ref, *, seq, topk):
    tq = DSA_TQ
    ch = DSA_CH
    hg = DSA_HG
    n_hg = DSA_HEADS // hg
    gw = hg * DSA_HD
    bpc = ch // LANES
    cpg = DSA_GROUP // ch
    n_groups = seq // DSA_GROUP
    i = pl.program_id(0)
    n_ch = ((i + 1) * tq + ch - 1) // ch
    lane_i = lax.broadcasted_iota(I32, (tq, LANES), 1)
    row_pos = lax.broadcasted_iota(I32, (tq, 1), 0) + i * tq
    kf = float(topk)

    def tile_l(a, width):
        return jnp.concatenate([a] * (width // a.shape[1]), axis=1)

    def chunk(c):
        return pl.ds(pl.multiple_of(c * ch, ch), ch)

    def lane_sum(a):
        return jnp.sum(a.astype(F32), axis=1, keepdims=True)

    @pl.when(i == 0)
    def _():
        key_ref[...] = jnp.zeros(key_ref.shape, I32)

        def norm_body(c, best):
            out = []
            for h in range(DSA_HEADS):
                kf32 = kt_ref[h * DSA_HD:(h + 1) * DSA_HD, chunk(c)].astype(F32)
                out.append(jnp.maximum(best[h], jnp.sum(kf32 * kf32, axis=0, keepdims=True)))
            return tuple(out)

        best = lax.fori_loop(0, seq // ch, norm_body,
                             tuple(jnp.zeros((1, ch), F32) for _ in range(DSA_HEADS)))
        for h in range(DSA_HEADS):
            kmax_ref[h] = jnp.sqrt(jnp.max(best[h]))

    wv = wi_ref[...]
    for h in range(IDX_HEADS):
        wb_ref[h] = jnp.broadcast_to(wv[:, IDX_DIM + h:IDX_DIM + h + 1], (tq, LANES))
    qi = qi_ref[0]

    def score_body(c, carry):
        lg = _dot(qi, kit_ref[:, chunk(c)])
        sc = jnp.maximum(lg[0:tq], 0.0) * tile_l(wb_ref[0], ch)
        for h in range(1, IDX_HEADS):
            sc = sc + jnp.maximum(lg[h * tq:(h + 1) * tq], 0.0) * tile_l(wb_ref[h], ch)
        sc = jnp.where(sc == 0.0, 0.0, sc)
        key_ref[c] = _float_to_ordered_bits(sc)
        return carry

    lax.fori_loop(0, n_ch, score_body, 0)

    n_pg = (n_ch + cpg - 1) // cpg

    def plane_body(t, carry):
        g = t // (tq // 8)
        rows = pl.ds(pl.multiple_of((t % (tq // 8)) * 8, 8), 8)

        def slot(b):
            return (g * cpg + b // bpc, rows, slice((b % bpc) * LANES, (b % bpc + 1) * LANES))

        planes = _transpose_bits([key_ref[slot(WORD - 1 - k)] for k in range(WORD)])
        for b in range(WORD):
            key_ref[slot(b)] = planes[b]
        return carry

    lax.fori_loop(0, n_pg * (tq // 8), plane_body, 0)

    def index_mask(g, bound):
        r = bound - g * DSA_GROUP - lane_i
        q = jnp.clip((r + (LANES - 1)) >> LANE_BITS, 0, WORD)
        return jnp.where(q >= WORD, jnp.int32(-1), (jnp.int32(1) << jnp.minimum(q, WORD - 1)) - 1)

    for g in range(n_groups):
        alive_ref[g] = index_mask(g, row_pos + 1)
        great_ref[g] = jnp.zeros((tq, LANES), I32)

    def select_bits(ng):
        def bit_body(p, cnt_great):
            pc = p // bpc
            pl0 = pl.multiple_of((p % bpc) * LANES, LANES)
            ones = []
            acc = jnp.zeros((tq, LANES), I32)
            for g in range(ng):
                x = alive_ref[g] & key_ref[g * cpg + pc, :, pl.ds(pl0, LANES)]
                acc = acc + lax.population_count(x)
                ones.append(x)
            cnt_one = lane_sum(acc)
            take = (cnt_great + cnt_one) >= kf
            take_b = jnp.broadcast_to(take, (tq, LANES))
            for g in range(ng):
                a = alive_ref[g]
                alive_ref[g] = jnp.where(take_b, ones[g], a ^ ones[g])
                great_ref[g] = jnp.where(take_b, great_ref[g], great_ref[g] | ones[g])
            return jnp.where(take, cnt_great, cnt_great + cnt_one)

        return lambda: lax.fori_loop(0, WORD, bit_body, jnp.zeros((tq, 1), F32))

    cnt_great = lax.switch(n_pg - 1, [select_bits(ng) for ng in range(1, n_groups + 1)])
    need = kf - cnt_great

    def count_alive(bound):
        acc = jnp.zeros((tq, LANES), I32)
        for g in range(n_groups):
            acc = acc + lax.population_count(alive_ref[g] & index_mask(g, bound))
        return lane_sum(acc)

    tie_row = count_alive(jnp.full((tq, 1), seq, I32)) > need

    def tie_break(_):
        nbits = max(1, int(np.ceil(np.log2(seq))))

        def jbit(p, m):
            cand = m | jnp.left_shift(jnp.int32(1), nbits - 1 - p)
            return jnp.where(count_alive(cand) < need, cand, m)

        m = lax.fori_loop(0, nbits, jbit, jnp.zeros((tq, 1), I32))
        return jnp.where(tie_row, m + 1, jnp.int32(seq))

    any_tie = jnp.max(jnp.where(tie_row, 1.0, 0.0)) > 0.0
    bound = lax.cond(any_tie, tie_break, lambda _: jnp.full((tq, 1), seq, I32), 0)
    for g in range(n_groups):
        great_ref[g] = great_ref[g] | (alive_ref[g] & index_mask(g, bound))

    qt = qb_ref[...]
    head_of_lane = lax.broadcasted_iota(I32, (tq, gw), 1) >> HD_BITS
    for g in range(n_hg):
        qg = qt[:, g * gw:(g + 1) * gw]
        for h in range(hg):
            qbd_ref[g, h * tq:(h + 1) * tq, :] = jnp.where(head_of_lane == h, qg, jnp.zeros_like(qg))
    l_ref[...] = jnp.zeros(l_ref.shape, F32)
    acc_ref[...] = jnp.zeros(acc_ref.shape, F32)

    for g in range(n_hg):
        qf = qbd_ref[g].astype(F32)
        qn = jnp.sqrt(jnp.sum(qf * qf, axis=1, keepdims=True))
        kn = jnp.concatenate([jnp.full((tq, 1), kmax_ref[g * hg + h], F32) for h in range(hg)], axis=0)
        bnd_ref[g] = jnp.broadcast_to(qn * kn, (hg * tq, LANES))
    bound_ok = jnp.max(bnd_ref[...]) < DSA_MAX_SHIFT

    def bias_of(c):
        sel = great_ref[c // cpg]
        for jj in range(bpc):
            bit = (sel >> ((c % cpg) * bpc + jj)) & 1
            bias_ref[:, jj * LANES:(jj + 1) * LANES] = jnp.where(bit != 0, 0.0, MASKED)

    def bounded_body(c, carry):
        bias_of(c)
        for g in range(n_hg):
            s_ref[g] = _dot(qbd_ref[g], kt_ref[g * gw:(g + 1) * gw, chunk(c)])
        for g in range(n_hg):
            for r in range(hg * tq // DSA_SLAB):
                rows = slice(r * DSA_SLAB, (r + 1) * DSA_SLAB)
                b0 = (r * DSA_SLAB) % tq
                s = s_ref[g, rows, :] + bias_ref[b0:b0 + DSA_SLAB, :]
                p = jnp.exp2(s - tile_l(bnd_ref[g, rows, :], ch))
                l_ref[g, rows, :] = l_ref[g, rows, :] + jnp.sum(p, axis=1, keepdims=True)
                p_ref[g, rows, :] = p.astype(BF16)
            acc_ref[g] = acc_ref[g] + _dot(p_ref[g], v_ref[chunk(c), g * gw:(g + 1) * gw])
        return carry

    def running_max_body(c, carry):
        bias_of(c)
        for g in range(n_hg):
            s_ref[g] = _dot(qbd_ref[g], kt_ref[g * gw:(g + 1) * gw, chunk(c)])
        for g in range(n_hg):
            for r in range(hg * tq // DSA_SLAB):
                rows = slice(r * DSA_SLAB, (r + 1) * DSA_SLAB)
                b0 = (r * DSA_SLAB) % tq
                s = s_ref[g, rows, :] + bias_ref[b0:b0 + DSA_SLAB, :]
                m_prev = m_ref[g, rows, :]
                m_new = jnp.maximum(m_prev, jnp.max(s, axis=1, keepdims=True))
                alpha = jnp.exp2(m_prev - m_new)
                p = jnp.exp2(s - tile_l(m_new, ch))
                l_ref[g, rows, :] = alpha * l_ref[g, rows, :] + jnp.sum(p, axis=1, keepdims=True)
                m_ref[g, rows, :] = m_new
                alpha_ref[g, rows, :] = alpha
                p_ref[g, rows, :] = p.astype(BF16)
            acc_ref[g] = (acc_ref[g] * tile_l(alpha_ref[g], gw)
                          + _dot(p_ref[g], v_ref[chunk(c), g * gw:(g + 1) * gw]))
        return carry

    @pl.when(bound_ok)
    def _():
        lax.fori_loop(0, n_ch, bounded_body, 0)

    @pl.when(jnp.logical_not(bound_ok))
    def _():
        m_ref[...] = jnp.full(m_ref.shape, MASKED, F32)
        lax.fori_loop(0, n_ch, running_max_body, 0)

    outs = []
    for g in range(n_hg):
        a = acc_ref[g] * tile_l(1.0 / l_ref[g], gw)
        og = jnp.zeros((tq, gw), F32)
        for h in range(hg):
            og = og + jnp.where(head_of_lane == h, a[h * tq:(h + 1) * tq], 0.0)
        outs.append(og)
    o_ref[...] = jnp.concatenate(outs, axis=1)


def _dsa(qi_r, kit, wi, qb, kt, v, topk):
    s = qb.shape[0]
    tq = DSA_TQ
    assert s % DSA_GROUP == 0 and DSA_GROUP % DSA_CH == 0
    n_hg = DSA_HEADS // DSA_HG
    rows = DSA_HG * tq
    gw = DSA_HG * DSA_HD
    return pl.pallas_call(
        functools.partial(_dsa_kernel, seq=s, topk=topk),
        grid=(s // tq,),
        in_specs=[pl.BlockSpec((1, IDX_HEADS * tq, IDX_DIM), lambda i: (i, 0, 0)),
                  _resident(kit.shape),
                  pl.BlockSpec((tq, LANES), lambda i: (i, 0)),
                  pl.BlockSpec((tq, DSA_W), lambda i: (i, 0)),
                  _resident(kt.shape),
                  _resident(v.shape)],
        out_specs=pl.BlockSpec((tq, DSA_W), lambda i: (i, 0)),
        out_shape=jax.ShapeDtypeStruct((s, DSA_W), F32),
        scratch_shapes=[pltpu.VMEM((s // DSA_CH, tq, DSA_CH), I32),
                        pltpu.VMEM((s // DSA_GROUP, tq, LANES), I32),
                        pltpu.VMEM((s // DSA_GROUP, tq, LANES), I32),
                        pltpu.VMEM((IDX_HEADS, tq, LANES), F32),
                        pltpu.VMEM((n_hg, rows, gw), BF16),
                        pltpu.VMEM((n_hg, rows, DSA_CH), F32),
                        pltpu.VMEM((n_hg, rows, DSA_CH), BF16),
                        pltpu.VMEM((tq, DSA_CH), F32),
                        pltpu.VMEM((n_hg, rows, LANES), F32),
                        pltpu.VMEM((n_hg, rows, LANES), F32),
                        pltpu.VMEM((n_hg, rows, LANES), F32),
                        pltpu.VMEM((n_hg, rows, gw), F32),
                        pltpu.VMEM((n_hg, rows, LANES), F32),
                        pltpu.SMEM((DSA_HEADS,), F32)],
        compiler_params=pltpu.CompilerParams(dimension_semantics=("arbitrary",),
                                             vmem_limit_bytes=DSA_VMEM_LIMIT_BYTES),
        name="dsa",
    )(qi_r, kit, wi, qb, kt, v)


def _merge_kernel(x_ref, mod_ref, gpre_ref, gpost_ref, oa_ref, ob_ref, wbg_ref, wpg_ref, wpd_ref,
                  wout_ref, o_ref):
    x = x_ref[...]
    h = _modulated(x, gpre_ref[...], mod_ref, 0, 1)
    gates = _sigmoid(_dot(h.astype(BF16), wbg_ref[...]))
    yg = _dot(oa_ref[...].astype(BF16), wpg_ref[...])
    yd = _dot(ob_ref[...].astype(BF16), wpd_ref[...])
    mix = gates[:, 0:D_MODEL] * yg + gates[:, D_MODEL:] * yd
    out = _dot(mix.astype(BF16), wout_ref[...])
    o_ref[...] = x + mod_ref[2:3, :] * _rms(out, gpost_ref[...])


def _merge(x, mod, gpre, gpost, oa, ob, wbg, wpg, wpd, wout):
    s = x.shape[0]
    tm = PROJ_ROWS
    row = lambda n: pl.BlockSpec((tm, n), lambda i: (i, 0))
    return pl.pallas_call(
        _merge_kernel,
        grid=(s // tm,),
        in_specs=[row(D_MODEL), _full(mod.shape), _full(gpre.shape), _full(gpost.shape),
                  row(GLA_V), row(DSA_W), _full(wbg.shape), _full(wpg.shape), _full(wpd.shape),
                  _full(wout.shape)],
        out_specs=row(D_MODEL),
        out_shape=jax.ShapeDtypeStruct((s, D_MODEL), F32),
        compiler_params=_params(("arbitrary",)),
        name="merge",
    )(x, mod, gpre, gpost, oa, ob, wbg, wpg, wpd, wout)


def _router(lg):
    t = lg.shape[0]
    lane = lax.broadcasted_iota(I32, (t, LANES), 1)
    lanef = lane.astype(F32)
    big = 1e9
    gm = lane < N_GROUPS
    gmax = jnp.max(jnp.where(gm, lg, -jnp.inf), axis=1, keepdims=True)
    gsum = jnp.sum(jnp.where(gm, jnp.exp(lg - gmax), 0.0), axis=1, keepdims=True)
    p_g = 1.0 / gsum
    g_sel = jnp.min(jnp.where(gm & (lg == gmax), lanef, big), axis=1, keepdims=True)
    lo = N_GROUPS + EXPERTS_PER_GROUP * g_sel
    em = (lanef >= lo) & (lanef < lo + EXPERTS_PER_GROUP)
    m1 = jnp.max(jnp.where(em, lg, -jnp.inf), axis=1, keepdims=True)
    i1 = jnp.min(jnp.where(em & (lg == m1), lanef, big), axis=1, keepdims=True)
    em2 = em & (lanef != i1)
    m2 = jnp.max(jnp.where(em2, lg, -jnp.inf), axis=1, keepdims=True)
    i2 = jnp.min(jnp.where(em2 & (lg == m2), lanef, big), axis=1, keepdims=True)
    e2 = jnp.exp(m2 - m1)
    inv = 1.0 / (1.0 + e2)
    return (jnp.where(lanef == i1, p_g * inv, 0.0) + jnp.where(lanef == i2, p_g * (e2 * inv), 0.0))


def _moe_kernel(x_ref, mod_ref, gpre_ref, gpost_ref, wr_ref, br_ref, wg_ref, wu_ref, wd_ref,
                o_ref, hb_ref, comb_ref, acc_ref):
    j = pl.program_id(1)
    tm = x_ref.shape[0]

    @pl.when(j == 0)
    def _():
        h = _modulated(x_ref[...], gpre_ref[...], mod_ref, 3, 4)
        hb = h.astype(BF16)
        hb_ref[...] = hb
        comb_ref[...] = _router(_dot(hb, wr_ref[...]) + br_ref[...])
        acc_ref[...] = jnp.zeros_like(acc_ref)

    hb = hb_ref[...]
    hgate = _dot(hb, wg_ref[...])
    hup = _dot(hb, wu_ref[...])
    act = hgate * _sigmoid(hgate) * hup
    comb = comb_ref[...]
    lane = lax.broadcasted_iota(I32, (tm, LANES), 1)
    parts = []
    for e in range(MOE_EB):
        sel = lane == (N_GROUPS + j * MOE_EB + e)
        cw = jnp.sum(jnp.where(sel, comb, 0.0), axis=1, keepdims=True)
        parts.append((act[:, e * D_EXPERT:(e + 1) * D_EXPERT] * cw).astype(BF16))
    acc_ref[...] += _dot(jnp.concatenate(parts, axis=1), wd_ref[...])

    @pl.when(j == pl.num_programs(1) - 1)
    def _():
        o_ref[...] = x_ref[...] + mod_ref[5:6, :] * _rms(acc_ref[...], gpost_ref[...])


def _moe(x, mod, gpre, gpost, wr, br, wg, wu, wd):
    s = x.shape[0]
    tm = MOE_ROWS
    bw = MOE_EB * D_EXPERT
    return pl.pallas_call(
        _moe_kernel,
        grid=(s // tm, N_EXPERTS // MOE_EB),
        in_specs=[pl.BlockSpec((tm, D_MODEL), lambda i, j: (i, 0)),
                  _full(mod.shape), _full(gpre.shape), _full(gpost.shape), _full(wr.shape),
                  _full(br.shape),
                  pl.BlockSpec((D_MODEL, bw), lambda i, j: (0, j)),
                  pl.BlockSpec((D_MODEL, bw), lambda i, j: (0, j)),
                  pl.BlockSpec((bw, D_MODEL), lambda i, j: (j, 0))],
        out_specs=pl.BlockSpec((tm, D_MODEL), lambda i, j: (i, 0)),
        out_shape=jax.ShapeDtypeStruct((s, D_MODEL), F32),
        scratch_shapes=[pltpu.VMEM((tm, D_MODEL), BF16), pltpu.VMEM((tm, LANES), F32),
                        pltpu.VMEM((tm, D_MODEL), F32)],
        compiler_params=_params(("arbitrary", "arbitrary")),
        name="moe",
    )(x, mod, gpre, gpost, wr, br, wg, wu, wd)


def _rope_lane_constants():
    rot = DSA_HD // ROT_FRAC
    half = rot // 2
    freqs = np.power(np.float32(ROPE_THETA), -np.arange(half, dtype=np.float32) * np.float32(2.0) / rot)
    j = np.arange(LANES) % DSA_HD
    fv = np.where(j < rot, freqs[j % half], 0.0).astype(np.float32)
    sg = np.where(j < half, -1.0, np.where(j < rot, 1.0, 0.0)).astype(np.float32)
    return jnp.asarray(fv)[None, :], jnp.asarray(sg)[None, :]


def _pad_cols(w, n):
    return jnp.pad(w, ((0, 0), (0, n - w.shape[1])))


def _layer(x, c, pos, w_ada, b_ada, g_pre_mix, g_post_mix, g_pre_ffn, g_post_ffn, w_in, w_gla_a2,
           b_gla_a, g_gla_out, g_idx_k, w_proj_gla, w_proj_dsa, w_out, w_router_g, b_router_g,
           w_router_e, b_router_e, w_e_gate, w_e_up, w_e_down):
    s = x.shape[0]
    mod = _ada(jnp.broadcast_to(c, (8, D_MODEL)), w_ada, b_ada[None, :])[0].reshape(N_MOD, D_MODEL)

    o = np.cumsum((GLA_QK, GLA_QK, GLA_V, GLA_V, GLA_GATE_RANK, DSA_W, DSA_W, DSA_W, IDX_Q, IDX_DIM,
                   IDX_HEADS, 2 * D_MODEL))
    wb = w_in.astype(BF16)
    w_gla = wb[:, 0:o[3]]
    w_ga = _pad_cols(wb[:, o[3]:o[4]], LANES)
    w_dsa = wb[:, o[4]:o[8]]
    w_sm = _pad_cols(jnp.concatenate([wb[:, o[8]:o[9]], wb[:, o[9]:o[10]]], axis=1), LANES)
    w_bg = wb[:, o[10]:o[11]]
    w_a2 = jnp.pad(w_gla_a2.astype(BF16), ((0, LANES - GLA_GATE_RANK), (0, 0)))

    q_a, k_a, v_a, gg, la = _gla_proj(x, mod, g_pre_mix[None, :], w_gla, w_ga, w_a2, b_gla_a[None, :])
    o_a = _gla(q_a, k_a, v_a, gg, la, jnp.tile(g_gla_out, GLA_HEADS)[None, :])

    fv, sg = _rope_lane_constants()
    gik = jnp.pad(g_idx_k, (0, LANES - IDX_DIM))[None, :]
    q_b, k_b, v_b, qi, ki, wi = _dsa_proj(x, mod, g_pre_mix[None, :], pos.astype(F32)[:, None], fv, sg,
                                          w_dsa, w_sm, gik)
    nqb = s // DSA_TQ
    qi_r = qi.reshape(nqb, DSA_TQ, IDX_HEADS, IDX_DIM).transpose(0, 2, 1, 3).reshape(
        nqb, IDX_HEADS * DSA_TQ, IDX_DIM)
    o_b = _dsa(qi_r, ki.T, wi, q_b, k_b.T, v_b, min(DSA_TOPK_MAX, s // 4))

    x1 = _merge(x, mod, g_pre_mix[None, :], g_post_mix[None, :], o_a, o_b, w_bg,
                w_proj_gla.astype(BF16), w_proj_dsa.astype(BF16), w_out.astype(BF16))

    wr = _pad_cols(jnp.concatenate([w_router_g, w_router_e], axis=1).astype(BF16), LANES)
    br = jnp.pad(jnp.concatenate([b_router_g, b_router_e]), (0, LANES - N_GROUPS - N_EXPERTS))[None, :]
    wg = w_e_gate.astype(BF16).transpose(1, 0, 2).reshape(D_MODEL, N_EXPERTS * D_EXPERT)
    wu = w_e_up.astype(BF16).transpose(1, 0, 2).reshape(D_MODEL, N_EXPERTS * D_EXPERT)
    wd = w_e_down.astype(BF16).reshape(N_EXPERTS * D_EXPERT, D_MODEL)
    return _moe(x1, mod, g_pre_ffn[None, :], g_post_ffn[None, :], wr, br, wg, wu, wd)


def kernel(x, c, positions, w_ada, b_ada, g_pre_mix, g_post_mix, g_pre_ffn, g_post_ffn, w_in, w_gla_a2,
           b_gla_a, g_gla_out, g_idx_k, w_proj_gla, w_proj_dsa, w_out, w_router_g, b_router_g,
           w_router_e, b_router_e, w_e_gate, w_e_up, w_e_down):
    batch, depth = x.shape[0], w_ada.shape[0]
    outs = []
    for bi in range(batch):
        xb = x[bi]
        for l in range(depth):
            xb = _layer(xb, c[bi:bi + 1], positions[bi], w_ada[l], b_ada[l], g_pre_mix[l], g_post_mix[l],
                        g_pre_ffn[l], g_post_ffn[l], w_in[l], w_gla_a2[l], b_gla_a[l], g_gla_out[l],
                        g_idx_k[l], w_proj_gla[l], w_proj_dsa[l], w_out[l], w_router_g[l],
                        b_router_g[l], w_router_e[l], b_router_e[l], w_e_gate[l], w_e_up[l],
                        w_e_down[l])
        outs.append(xb)
    return jnp.stack(outs, axis=0)
```

```python
import functools

import numpy as np
import jax
import jax.numpy as jnp
from jax import lax
from jax.experimental import pallas as pl
from jax.experimental.pallas import tpu as pltpu

F32 = jnp.float32
BF16 = jnp.bfloat16
I32 = jnp.int32

D_MODEL = 1024
EPS = 1e-6
ROPE_THETA = 500000.0
ROT_FRAC = 4
GLA_HEADS = 4
GLA_DK = 64
GLA_DV = 128
GLA_GATE_RANK = 16
GLA_TAU = 16.0
GLA_CHUNK = 64
DSA_HEADS = 8
DSA_HD = 64
IDX_HEADS = 8
IDX_DIM = 64
DSA_TOPK_MAX = 256
N_GROUPS = 4
EXPERTS_PER_GROUP = 8
N_EXPERTS = N_GROUPS * EXPERTS_PER_GROUP
D_EXPERT = D_MODEL // 4
N_MOD = 6

GLA_QK = GLA_HEADS * GLA_DK
GLA_V = GLA_HEADS * GLA_DV
DSA_W = DSA_HEADS * DSA_HD
IDX_Q = IDX_HEADS * IDX_DIM

LANES = 128
VMEM_LIMIT_BYTES = 56 * 1024 * 1024
DSA_VMEM_LIMIT_BYTES = 60 * 1024 * 1024

PROJ_ROWS = 512
GLA_ROWS = 512
DSA_TQ = 256
DSA_CH = 1024
DSA_CH_ATT = 512
MOE_ROWS = 512
MOE_EB = 8
MASKED = -1e30
LOG2E = float(np.log2(np.e))
GLA_LEVELS = 7


def _dot(a, b):
    return jnp.dot(a, b, preferred_element_type=F32)


def _dot_nt(a, b):
    return lax.dot_general(a, b, (((1,), (1,)), ((), ())), preferred_element_type=F32)


def _dot_tn(a, b):
    return lax.dot_general(a, b, (((0,), (0,)), ((), ())), preferred_element_type=F32)


def _sigmoid(x):
    return 1.0 / (1.0 + jnp.exp(-x))


def _rms(x, g):
    ms = jnp.mean(x * x, axis=-1, keepdims=True)
    return x * lax.rsqrt(ms + EPS) * g


def _modulated(x, g, mod_ref, shift_row, scale_row):
    return (_rms(x, g) * (1.0 + mod_ref[scale_row:scale_row + 1, :])
            + mod_ref[shift_row:shift_row + 1, :])


def _params(sem):
    return pltpu.CompilerParams(dimension_semantics=sem, vmem_limit_bytes=VMEM_LIMIT_BYTES)


def _full(shape):
    return pl.BlockSpec(shape, lambda *_: (0,) * len(shape))


def _resident(shape):
    return pl.BlockSpec(shape, lambda *_: (0,) * len(shape), pipeline_mode=pl.Buffered(1))


def _ada_kernel(c_ref, w_ref, b_ref, o_ref):
    c = c_ref[...]
    a = c * _sigmoid(c)
    o_ref[...] = _dot(a.astype(BF16), w_ref[...].astype(BF16)) + b_ref[...]


def _ada(c8, w, b):
    n = w.shape[1]
    bn = 1536
    return pl.pallas_call(
        _ada_kernel,
        grid=(n // bn,),
        in_specs=[_full(c8.shape),
                  pl.BlockSpec((D_MODEL, bn), lambda j: (0, j)),
                  pl.BlockSpec((1, bn), lambda j: (0, j))],
        out_specs=pl.BlockSpec((8, bn), lambda j: (0, j)),
        out_shape=jax.ShapeDtypeStruct((8, n), F32),
        compiler_params=_params(("arbitrary",)),
        name="ada",
    )(c8, w, b)


def _gla_proj_kernel(x_ref, mod_ref, g_ref, w_ref, wga_ref, wa2_ref, ba_ref,
                     q_ref, k_ref, v_ref, gg_ref, la_ref):
    h = _modulated(x_ref[...], g_ref[...], mod_ref, 0, 1)
    hb = h.astype(BF16)
    p = _dot(hb, w_ref[...])
    q_ref[...] = p[:, 0:GLA_QK] * (GLA_DK ** -0.5)
    k_ref[...] = p[:, GLA_QK:2 * GLA_QK]
    v_ref[...] = p[:, 2 * GLA_QK:2 * GLA_QK + GLA_V]
    gg_ref[...] = p[:, 2 * GLA_QK + GLA_V:]
    ga = _dot(hb, wga_ref[...])
    z = _dot(ga.astype(BF16), wa2_ref[...]) + ba_ref[...]
    log_sig = jnp.minimum(z, 0.0) - jnp.log1p(jnp.exp(-jnp.abs(z)))
    la_ref[...] = log_sig * (1.0 / GLA_TAU)


def _gla_proj(x, mod, g, w, wga, wa2, ba):
    s = x.shape[0]
    tm = PROJ_ROWS
    row = lambda n: pl.BlockSpec((tm, n), lambda i: (i, 0))
    return pl.pallas_call(
        _gla_proj_kernel,
        grid=(s // tm,),
        in_specs=[row(D_MODEL), _full(mod.shape), _full(g.shape), _full(w.shape),
                  _full(wga.shape), _full(wa2.shape), _full(ba.shape)],
        out_specs=[row(GLA_QK), row(GLA_QK), row(GLA_V), row(GLA_V), row(GLA_QK)],
        out_shape=[jax.ShapeDtypeStruct((s, GLA_QK), F32), jax.ShapeDtypeStruct((s, GLA_QK), F32),
                   jax.ShapeDtypeStruct((s, GLA_V), F32), jax.ShapeDtypeStruct((s, GLA_V), F32),
                   jax.ShapeDtypeStruct((s, GLA_QK), F32)],
        compiler_params=_params(("arbitrary",)),
        name="gla_proj",
    )(x, mod, g, w, wga, wa2, ba)


def _dsa_proj_kernel(x_ref, mod_ref, g_ref, pos_ref, fv_ref, sg_ref, w_ref, wsm_ref, gik_ref,
                     qb_ref, kb_ref, vb_ref, qi_ref, ki_ref, wi_ref):
    tm = x_ref.shape[0]
    h = _modulated(x_ref[...], g_ref[...], mod_ref, 0, 1)
    hb = h.astype(BF16)
    ang = pos_ref[...] * fv_ref[...]
    cs = jnp.cos(ang)
    sn = jnp.sin(ang) * sg_ref[...]
    lane = lax.broadcasted_iota(I32, (tm, LANES), 1)
    first = (lane & (DSA_HD - 1)) < (DSA_HD // ROT_FRAC // 2)

    def rope(t):
        width = t.shape[1]
        rep = width // LANES
        tile = (lambda a: jnp.concatenate([a] * rep, axis=1)) if rep > 1 else (lambda a: a)
        half = DSA_HD // ROT_FRAC // 2
        fwd = pltpu.roll(t, half, 1)
        bwd = pltpu.roll(t, width - half, 1)
        partner = jnp.where(tile(first), bwd, fwd)
        return t * tile(cs) + partner * tile(sn)

    p = _dot(hb, w_ref[...])
    qb_ref[...] = (rope(p[:, 0:DSA_W]) * (DSA_HD ** -0.5 * LOG2E)).astype(BF16)
    kb_ref[...] = rope(p[:, DSA_W:2 * DSA_W]).astype(BF16)
    vb_ref[...] = p[:, 2 * DSA_W:3 * DSA_W].astype(BF16)
    qi_ref[...] = (rope(p[:, 3 * DSA_W:]) * (IDX_DIM ** -0.5)).astype(BF16)

    sm = _dot(hb, wsm_ref[...])
    is_ik = lane < IDX_DIM
    mu = jnp.sum(jnp.where(is_ik, sm, 0.0), axis=-1, keepdims=True) * (1.0 / IDX_DIM)
    xc = jnp.where(is_ik, sm - mu, 0.0)
    var = jnp.sum(xc * xc, axis=-1, keepdims=True) * (1.0 / IDX_DIM)
    y = xc * lax.rsqrt(var + EPS) * gik_ref[...]
    ki_ref[...] = rope(y)[:, 0:IDX_DIM].astype(BF16)
    wi_ref[...] = sm * (IDX_HEADS ** -0.5)


def _dsa_proj(x, mod, g, pos, fv, sg, w, wsm, gik):
    s = x.shape[0]
    tm = PROJ_ROWS
    row = lambda n: pl.BlockSpec((tm, n), lambda i: (i, 0))
    return pl.pallas_call(
        _dsa_proj_kernel,
        grid=(s // tm,),
        in_specs=[row(D_MODEL), _full(mod.shape), _full(g.shape), row(1), _full(fv.shape),
                  _full(sg.shape), _full(w.shape), _full(wsm.shape), _full(gik.shape)],
        out_specs=[row(DSA_W), row(DSA_W), row(DSA_W), row(IDX_Q), row(IDX_DIM), row(LANES)],
        out_shape=[jax.ShapeDtypeStruct((s, DSA_W), BF16), jax.ShapeDtypeStruct((s, DSA_W), BF16),
                   jax.ShapeDtypeStruct((s, DSA_W), BF16), jax.ShapeDtypeStruct((s, IDX_Q), BF16),
                   jax.ShapeDtypeStruct((s, IDX_DIM), BF16), jax.ShapeDtypeStruct((s, LANES), F32)],
        compiler_params=_params(("arbitrary",)),
        name="dsa_proj",
    )(x, mod, g, pos, fv, sg, w, wsm, gik)


def _gla_constants():
    c = GLA_CHUNK
    tril = np.tril(np.ones((c, c), np.float32))
    t = np.arange(c)
    mats = [tril]
    masks = [(t[:, None] == t[None, :])]
    for hs in (32, 16, 8, 4, 2, 1):
        blk = 2 * hs
        r = (t // blk) * blk + hs - 1
        mats.append(tril - tril[r, :])
        same = (t[:, None] // blk) == (t[None, :] // blk)
        masks.append(same & ((t[:, None] % blk) >= hs) & ((t[None, :] % blk) < hs))
    m_all = np.concatenate(mats, axis=0)
    lvl = np.stack([np.tile(m.astype(np.float32), (1, GLA_HEADS)) for m in masks])
    hrow = np.arange(GLA_HEADS * c) // c
    wmask = (hrow[:, None] == (np.arange(GLA_QK) // GLA_DK)[None, :]).astype(np.float32)
    vmask = (hrow[:, None] == (np.arange(GLA_V) // GLA_DV)[None, :]).astype(np.float32)
    smask = ((np.arange(GLA_V) // GLA_DV)[:, None] == (np.arange(GLA_QK) // GLA_DK)[None, :])
    return (jnp.asarray(m_all, BF16), jnp.asarray(lvl), jnp.asarray(wmask), jnp.asarray(vmask),
            jnp.asarray(smask.astype(np.float32)))


def _gla_kernel(q_ref, k_ref, v_ref, gg_ref, la_ref, gout_ref, mall_ref, lvl_ref, wmask_ref,
                vmask_ref, smask_ref, o_ref, st_ref):
    c = GLA_CHUNK

    @pl.when(pl.program_id(0) == 0)
    def _():
        st_ref[...] = jnp.zeros_like(st_ref)

    m_all = mall_ref[...]
    wmask = wmask_ref[...]
    vmask = vmask_ref[...]
    smask = smask_ref[...]
    gout = gout_ref[...]

    def chunk(ci, carry):
        r0 = pl.multiple_of(ci * c, c)
        rows = pl.ds(r0, c)
        q = q_ref[rows, :]
        k = k_ref[rows, :]
        v = v_ref[rows, :]
        la = la_ref[rows, :]
        hi = la.astype(BF16)
        r1 = la - hi.astype(F32)
        mid = r1.astype(BF16)
        lo = (r1 - mid.astype(F32)).astype(BF16)
        dall = _dot(m_all, hi) + _dot(m_all, mid) + _dot(m_all, lo)
        b = dall[0:c]
        b_last = b[c - 1:c, :]
        qhat = q * jnp.exp(b)
        khat = k * jnp.exp(b_last - b)

        a = jnp.zeros((c, GLA_HEADS * c), F32)
        for lv in range(GLA_LEVELS):
            if lv == 0:
                qt, kt = q, k
            else:
                d = dall[lv * c:(lv + 1) * c]
                qt = q * jnp.exp(jnp.minimum(d, 0.0))
                kt = k * jnp.exp(jnp.minimum(-d, 0.0))
            w = (jnp.concatenate([kt] * GLA_HEADS, axis=0) * wmask).astype(BF16)
            a = a + lvl_ref[lv] * _dot_nt(qt.astype(BF16), w)

        st = st_ref[...]
        vbd = (jnp.concatenate([v] * GLA_HEADS, axis=0) * vmask).astype(BF16)
        o = _dot(a.astype(BF16), vbd) + _dot_nt(qhat.astype(BF16), st.astype(BF16))
        st_ref[...] = st * jnp.exp(b_last) + smask * _dot_tn(v.astype(BF16), khat.astype(BF16))

        parts = []
        for hh in range(GLA_HEADS):
            oh = o[:, hh * GLA_DV:(hh + 1) * GLA_DV]
            ms = jnp.mean(oh * oh, axis=-1, keepdims=True)
            parts.append(oh * lax.rsqrt(ms + EPS))
        gg = gg_ref[rows, :]
        o_ref[rows, :] = jnp.concatenate(parts, axis=1) * gout * (gg * _sigmoid(gg))
        return carry

    lax.fori_loop(0, q_ref.shape[0] // c, chunk, 0, unroll=4)


def _gla(q, k, v, gg, la, gout):
    s = q.shape[0]
    tb = GLA_ROWS
    consts = _gla_constants()
    row = lambda n: pl.BlockSpec((tb, n), lambda i: (i, 0))
    return pl.pallas_call(
        _gla_kernel,
        grid=(s // tb,),
        in_specs=[row(GLA_QK), row(GLA_QK), row(GLA_V), row(GLA_V), row(GLA_QK), _full(gout.shape)]
                 + [_full(a.shape) for a in consts],
        out_specs=row(GLA_V),
        out_shape=jax.ShapeDtypeStruct((s, GLA_V), F32),
        scratch_shapes=[pltpu.VMEM((GLA_V, GLA_QK), F32)],
        compiler_params=_params(("arbitrary",)),
        name="gla",
    )(q, k, v, gg, la, gout, *consts)


DSA_SLAB = 32
DSA_HG = 2
DSA_MAX_SHIFT = 48.0
WORD = 32
DSA_GROUP = WORD * LANES
LANE_BITS = LANES.bit_length() - 1
HD_BITS = DSA_HD.bit_length() - 1


def _float_to_ordered_bits(x):
    bits = lax.bitcast_convert_type(x, I32)
    key = bits ^ ((bits >> 31) & jnp.int32(0x7FFFFFFF))
    return key ^ jnp.int32(-2147483648)


def _transpose_bits(a):
    a = list(a)
    j, msk = 16, 0x0000FFFF
    while j:
        m32 = jnp.int32(np.array(msk, np.uint32).view(np.int32))
        k = 0
        while k < WORD:
            t = (a[k] ^ (a[k + j] >> j)) & m32
            a[k] = a[k] ^ t
            a[k + j] = a[k + j] ^ (t << j)
            k = (k + j + 1) & ~j
        j >>= 1
        msk = (msk ^ (msk << j)) & 0xFFFFFFFF
    return a


def _dsa_kernel(qi_ref, kit_ref, wi_ref, qb_ref, kt_hbm, v_hbm, o_ref,
                key_ref, alive_ref, great_ref, wb_ref, qbd_ref, s_ref, p_ref, bias_ref, m_ref, l_ref,
                alpha_ref, acc_ref, bnd_ref, kmax_ref, kbuf_ref, vbuf_ref, sem_ref, *, seq, topk):
    tq = DSA_TQ
    ch = DSA_CH
    hg = DSA_HG
    n_hg = DSA_HEADS // hg
    gw = hg * DSA_HD
    bpc = ch // LANES
    ca = DSA_CH_ATT
    bpa = ca // LANES
    cga = DSA_GROUP // ca
    n_ca = ((pl.program_id(0) + 1) * tq + ca - 1) // ca
    cpg = DSA_GROUP // ch
    n_groups = seq // DSA_GROUP
    i = pl.program_id(0)
    n_ch = ((i + 1) * tq + ch - 1) // ch
    lane_i = lax.broadcasted_iota(I32, (tq, LANES), 1)
    row_pos = lax.broadcasted_iota(I32, (tq, 1), 0) + i * tq
    kf = float(topk)

    def tile_l(a, width):
        return jnp.concatenate([a] * (width // a.shape[1]), axis=1)

    def chunk(c):
        return pl.ds(pl.multiple_of(c * ch, ch), ch)

    def key_copy(c, slot):
        cols = pl.ds(pl.multiple_of(c * ca, ca), ca)
        return pltpu.make_async_copy(kt_hbm.at[:, cols], kbuf_ref.at[slot], sem_ref.at[0, slot])

    def value_copy(c, slot):
        cols = pl.ds(pl.multiple_of(c * ca, ca), ca)
        return pltpu.make_async_copy(v_hbm.at[cols, :], vbuf_ref.at[slot], sem_ref.at[1, slot])

    def kv_copies(c, slot):
        return key_copy(c, slot), value_copy(c, slot)

    def lane_sum(a):
        return jnp.sum(a.astype(F32), axis=1, keepdims=True)

    @pl.when(i == 0)
    def _():
        key_ref[...] = jnp.zeros(key_ref.shape, I32)

        def norm_body(c, best):
            copy = key_copy(c, 0)
            copy.start()
            copy.wait()
            out = []
            for h in range(DSA_HEADS):
                kf32 = kbuf_ref[0, h * DSA_HD:(h + 1) * DSA_HD, :].astype(F32)
                out.append(jnp.maximum(best[h], jnp.sum(kf32 * kf32, axis=0, keepdims=True)))
            return tuple(out)

        best = lax.fori_loop(0, seq // ca, norm_body,
                             tuple(jnp.zeros((1, ca), F32) for _ in range(DSA_HEADS)))
        for h in range(DSA_HEADS):
            kmax_ref[h] = jnp.sqrt(jnp.max(best[h]))

    wv = wi_ref[...]
    for h in range(IDX_HEADS):
        wb_ref[h] = jnp.broadcast_to(wv[:, IDX_DIM + h:IDX_DIM + h + 1], (tq, LANES))

    def score_body(c, carry):
        sc = jnp.zeros((tq, ch), F32)
        hh = IDX_HEADS // 2
        for half in range(2):
            lg = _dot(qi_ref[0, half * hh * tq:(half + 1) * hh * tq, :], kit_ref[:, chunk(c)])
            for h in range(hh):
                sc = sc + jnp.maximum(lg[h * tq:(h + 1) * tq], 0.0) * tile_l(wb_ref[half * hh + h], ch)
        sc = jnp.where(sc == 0.0, 0.0, sc)
        key_ref[c] = _float_to_ordered_bits(sc)
        return carry

    lax.fori_loop(0, n_ch, score_body, 0)

    n_pg = (n_ch + cpg - 1) // cpg

    def plane_body(t, carry):
        g = t // (tq // 8)
        rows = pl.ds(pl.multiple_of((t % (tq // 8)) * 8, 8), 8)

        def slot(b):
            return (g * cpg + b // bpc, rows, slice((b % bpc) * LANES, (b % bpc + 1) * LANES))

        planes = _transpose_bits([key_ref[slot(WORD - 1 - k)] for k in range(WORD)])
        for b in range(WORD):
            key_ref[slot(b)] = planes[b]
        return carry

    lax.fori_loop(0, n_pg * (tq // 8), plane_body, 0)

    def index_mask(g, bound):
        r = bound - g * DSA_GROUP - lane_i
        q = jnp.clip((r + (LANES - 1)) >> LANE_BITS, 0, WORD)
        return jnp.where(q >= WORD, jnp.int32(-1), (jnp.int32(1) << jnp.minimum(q, WORD - 1)) - 1)

    for g in range(n_groups):
        alive_ref[g] = index_mask(g, row_pos + 1)
        great_ref[g] = jnp.zeros((tq, LANES), I32)

    def select_bits(ng):
        def bit_body(p, cnt_great):
            pc = p // bpc
            pl0 = pl.multiple_of((p % bpc) * LANES, LANES)
            ones = []
            acc = jnp.zeros((tq, LANES), I32)
            for g in range(ng):
                x = alive_ref[g] & key_ref[g * cpg + pc, :, pl.ds(pl0, LANES)]
                acc = acc + lax.population_count(x)
                ones.append(x)
            cnt_one = lane_sum(acc)
            take = (cnt_great + cnt_one) >= kf
            take_b = jnp.broadcast_to(take, (tq, LANES))
            for g in range(ng):
                a = alive_ref[g]
                alive_ref[g] = jnp.where(take_b, ones[g], a ^ ones[g])
                great_ref[g] = jnp.where(take_b, great_ref[g], great_ref[g] | ones[g])
            return jnp.where(take, cnt_great, cnt_great + cnt_one)

        return lambda: lax.fori_loop(0, WORD, bit_body, jnp.zeros((tq, 1), F32))

    cnt_great = lax.switch(n_pg - 1, [select_bits(ng) for ng in range(1, n_groups + 1)])
    need = kf - cnt_great

    def count_alive(bound):
        acc = jnp.zeros((tq, LANES), I32)
        for g in range(n_groups):
            acc = acc + lax.population_count(alive_ref[g] & index_mask(g, bound))
        return lane_sum(acc)

    tie_row = count_alive(jnp.full((tq, 1), seq, I32)) > need

    def tie_break(_):
        nbits = max(1, int(np.ceil(np.log2(seq))))

        def jbit(p, m):
            cand = m | jnp.left_shift(jnp.int32(1), nbits - 1 - p)
            return jnp.where(count_alive(cand) < need, cand, m)

        m = lax.fori_loop(0, nbits, jbit, jnp.zeros((tq, 1), I32))
        return jnp.where(tie_row, m + 1, jnp.int32(seq))

    any_tie = jnp.max(jnp.where(tie_row, 1.0, 0.0)) > 0.0
    bound = lax.cond(any_tie, tie_break, lambda _: jnp.full((tq, 1), seq, I32), 0)
    for g in range(n_groups):
        great_ref[g] = great_ref[g] | (alive_ref[g] & index_mask(g, bound))

    qt = qb_ref[...]
    head_of_lane = lax.broadcasted_iota(I32, (tq, gw), 1) >> HD_BITS
    for g in range(n_hg):
        qg = qt[:, g * gw:(g + 1) * gw]
        for h in range(hg):
            qbd_ref[g, h * tq:(h + 1) * tq, :] = jnp.where(head_of_lane == h, qg, jnp.zeros_like(qg))
    l_ref[...] = jnp.zeros(l_ref.shape, F32)
    acc_ref[...] = jnp.zeros(acc_ref.shape, F32)

    for g in range(n_hg):
        qf = qbd_ref[g].astype(F32)
        qn = jnp.sqrt(jnp.sum(qf * qf, axis=1, keepdims=True))
        kn = jnp.concatenate([jnp.full((tq, 1), kmax_ref[g * hg + h], F32) for h in range(hg)], axis=0)
        bnd_ref[g] = jnp.broadcast_to(qn * kn, (hg * tq, LANES))
    bound_ok = jnp.max(bnd_ref[...]) < DSA_MAX_SHIFT

    def bias_of(c):
        sel = great_ref[c // cga]
        for jj in range(bpa):
            bit = (sel >> ((c % cga) * bpa + jj)) & 1
            bias_ref[:, jj * LANES:(jj + 1) * LANES] = jnp.where(bit != 0, 0.0, MASKED)

    def over_key_chunks(body):
        for cp in kv_copies(0, 0):
            cp.start()

        def trip(c, carry):
            slot = c & 1
            for cp in kv_copies(c, slot):
                cp.wait()

            @pl.when(c + 1 < n_ca)
            def _():
                for cp in kv_copies(c + 1, 1 - slot):
                    cp.start()

            body(c, slot)
            return carry

        lax.fori_loop(0, n_ca, trip, 0)

    def bounded_body(c, slot):
        bias_of(c)
        for g in range(n_hg):
            s_ref[g] = _dot(qbd_ref[g], kbuf_ref[slot, g * gw:(g + 1) * gw, :])
        for g in range(n_hg):
            for r in range(hg * tq // DSA_SLAB):
                rows = slice(r * DSA_SLAB, (r + 1) * DSA_SLAB)
                b0 = (r * DSA_SLAB) % tq
                s = s_ref[g, rows, :] + bias_ref[b0:b0 + DSA_SLAB, :]
                p = jnp.exp2(s - tile_l(bnd_ref[g, rows, :], ca))
                l_ref[g, rows, :] = l_ref[g, rows, :] + jnp.sum(p, axis=1, keepdims=True)
                p_ref[g, rows, :] = p.astype(BF16)
            acc_ref[g] = acc_ref[g] + _dot(p_ref[g], vbuf_ref[slot, :, g * gw:(g + 1) * gw])

    def running_max_body(c, slot):
        bias_of(c)
        for g in range(n_hg):
            s_ref[g] = _dot(qbd_ref[g], kbuf_ref[slot, g * gw:(g + 1) * gw, :])
        for g in range(n_hg):
            for r in range(hg * tq // DSA_SLAB):
                rows = slice(r * DSA_SLAB, (r + 1) * DSA_SLAB)
                b0 = (r * DSA_SLAB) % tq
                s = s_ref[g, rows, :] + bias_ref[b0:b0 + DSA_SLAB, :]
                m_prev = m_ref[g, rows, :]
                m_new = jnp.maximum(m_prev, jnp.max(s, axis=1, keepdims=True))
                alpha = jnp.exp2(m_prev - m_new)
                p = jnp.exp2(s - tile_l(m_new, ca))
                l_ref[g, rows, :] = alpha * l_ref[g, rows, :] + jnp.sum(p, axis=1, keepdims=True)
                m_ref[g, rows, :] = m_new
                alpha_ref[g, rows, :] = alpha
                p_ref[g, rows, :] = p.astype(BF16)
            acc_ref[g] = (acc_ref[g] * tile_l(alpha_ref[g], gw)
                          + _dot(p_ref[g], vbuf_ref[slot, :, g * gw:(g + 1) * gw]))

    @pl.when(bound_ok)
    def _():
        over_key_chunks(bounded_body)

    @pl.when(jnp.logical_not(bound_ok))
    def _():
        m_ref[...] = jnp.full(m_ref.shape, MASKED, F32)
        over_key_chunks(running_max_body)

    outs = []
    for g in range(n_hg):
        a = acc_ref[g] * tile_l(1.0 / l_ref[g], gw)
        og = jnp.zeros((tq, gw), F32)
        for h in range(hg):
            og = og + jnp.where(head_of_lane == h, a[h * tq:(h + 1) * tq], 0.0)
        outs.append(og)
    o_ref[...] = jnp.concatenate(outs, axis=1)


def _dsa(qi_r, kit, wi, qb, kt, v, topk):
    s = qb.shape[0]
    tq = DSA_TQ
    assert s % DSA_GROUP == 0 and DSA_GROUP % DSA_CH == 0
    n_hg = DSA_HEADS // DSA_HG
    rows = DSA_HG * tq
    gw = DSA_HG * DSA_HD
    return pl.pallas_call(
        functools.partial(_dsa_kernel, seq=s, topk=topk),
        grid=(s // tq,),
        in_specs=[pl.BlockSpec((1, IDX_HEADS * tq, IDX_DIM), lambda i: (i, 0, 0)),
                  _resident(kit.shape),
                  pl.BlockSpec((tq, LANES), lambda i: (i, 0)),
                  pl.BlockSpec((tq, DSA_W), lambda i: (i, 0)),
                  pl.BlockSpec(memory_space=pl.ANY),
                  pl.BlockSpec(memory_space=pl.ANY)],
        out_specs=pl.BlockSpec((tq, DSA_W), lambda i: (i, 0)),
        out_shape=jax.ShapeDtypeStruct((s, DSA_W), F32),
        scratch_shapes=[pltpu.VMEM((s // DSA_CH, tq, DSA_CH), I32),
                        pltpu.VMEM((s // DSA_GROUP, tq, LANES), I32),
                        pltpu.VMEM((s // DSA_GROUP, tq, LANES), I32),
                        pltpu.VMEM((IDX_HEADS, tq, LANES), F32),
                        pltpu.VMEM((n_hg, rows, gw), BF16),
                        pltpu.VMEM((n_hg, rows, DSA_CH_ATT), F32),
                        pltpu.VMEM((n_hg, rows, DSA_CH_ATT), BF16),
                        pltpu.VMEM((tq, DSA_CH_ATT), F32),
                        pltpu.VMEM((n_hg, rows, LANES), F32),
                        pltpu.VMEM((n_hg, rows, LANES), F32),
                        pltpu.VMEM((n_hg, rows, LANES), F32),
                        pltpu.VMEM((n_hg, rows, gw), F32),
                        pltpu.VMEM((n_hg, rows, LANES), F32),
                        pltpu.SMEM((DSA_HEADS,), F32),
                        pltpu.VMEM((2, DSA_W, DSA_CH_ATT), BF16),
                        pltpu.VMEM((2, DSA_CH_ATT, DSA_W), BF16),
                        pltpu.SemaphoreType.DMA((2, 2))],
        compiler_params=pltpu.CompilerParams(dimension_semantics=("arbitrary",),
                                             vmem_limit_bytes=DSA_VMEM_LIMIT_BYTES),
        name="dsa",
    )(qi_r, kit, wi, qb, kt, v)


def _merge_kernel(x_ref, mod_ref, gpre_ref, gpost_ref, oa_ref, ob_ref, wbg_ref, wpg_ref, wpd_ref,
                  wout_ref, o_ref):
    x = x_ref[...]
    h = _modulated(x, gpre_ref[...], mod_ref, 0, 1)
    gates = _sigmoid(_dot(h.astype(BF16), wbg_ref[...]))
    yg = _dot(oa_ref[...].astype(BF16), wpg_ref[...])
    yd = _dot(ob_ref[...].astype(BF16), wpd_ref[...])
    mix = gates[:, 0:D_MODEL] * yg + gates[:, D_MODEL:] * yd
    out = _dot(mix.astype(BF16), wout_ref[...])
    o_ref[...] = x + mod_ref[2:3, :] * _rms(out, gpost_ref[...])


def _merge(x, mod, gpre, gpost, oa, ob, wbg, wpg, wpd, wout):
    s = x.shape[0]
    tm = PROJ_ROWS
    row = lambda n: pl.BlockSpec((tm, n), lambda i: (i, 0))
    return pl.pallas_call(
        _merge_kernel,
        grid=(s // tm,),
        in_specs=[row(D_MODEL), _full(mod.shape), _full(gpre.shape), _full(gpost.shape),
                  row(GLA_V), row(DSA_W), _full(wbg.shape), _full(wpg.shape), _full(wpd.shape),
                  _full(wout.shape)],
        out_specs=row(D_MODEL),
        out_shape=jax.ShapeDtypeStruct((s, D_MODEL), F32),
        compiler_params=_params(("arbitrary",)),
        name="merge",
    )(x, mod, gpre, gpost, oa, ob, wbg, wpg, wpd, wout)


def _router(lg):
    t = lg.shape[0]
    lane = lax.broadcasted_iota(I32, (t, LANES), 1)
    lanef = lane.astype(F32)
    big = 1e9
    gm = lane < N_GROUPS
    gmax = jnp.max(jnp.where(gm, lg, -jnp.inf), axis=1, keepdims=True)
    gsum = jnp.sum(jnp.where(gm, jnp.exp(lg - gmax), 0.0), axis=1, keepdims=True)
    p_g = 1.0 / gsum
    g_sel = jnp.min(jnp.where(gm & (lg == gmax), lanef, big), axis=1, keepdims=True)
    lo = N_GROUPS + EXPERTS_PER_GROUP * g_sel
    em = (lanef >= lo) & (lanef < lo + EXPERTS_PER_GROUP)
    m1 = jnp.max(jnp.where(em, lg, -jnp.inf), axis=1, keepdims=True)
    i1 = jnp.min(jnp.where(em & (lg == m1), lanef, big), axis=1, keepdims=True)
    em2 = em & (lanef != i1)
    m2 = jnp.max(jnp.where(em2, lg, -jnp.inf), axis=1, keepdims=True)
    i2 = jnp.min(jnp.where(em2 & (lg == m2), lanef, big), axis=1, keepdims=True)
    e2 = jnp.exp(m2 - m1)
    inv = 1.0 / (1.0 + e2)
    return (jnp.where(lanef == i1, p_g * inv, 0.0) + jnp.where(lanef == i2, p_g * (e2 * inv), 0.0))


def _moe_kernel(x_ref, mod_ref, gpre_ref, gpost_ref, wr_ref, br_ref, wg_ref, wu_ref, wd_ref,
                o_ref, hb_ref, comb_ref, acc_ref):
    j = pl.program_id(1)
    tm = x_ref.shape[0]

    @pl.when(j == 0)
    def _():
        h = _modulated(x_ref[...], gpre_ref[...], mod_ref, 3, 4)
        hb = h.astype(BF16)
        hb_ref[...] = hb
        comb_ref[...] = _router(_dot(hb, wr_ref[...]) + br_ref[...])
        acc_ref[...] = jnp.zeros_like(acc_ref)

    hb = hb_ref[...]
    hgate = _dot(hb, wg_ref[...])
    hup = _dot(hb, wu_ref[...])
    act = hgate * _sigmoid(hgate) * hup
    comb = comb_ref[...]
    lane = lax.broadcasted_iota(I32, (tm, LANES), 1)
    parts = []
    for e in range(MOE_EB):
        sel = lane == (N_GROUPS + j * MOE_EB + e)
        cw = jnp.sum(jnp.where(sel, comb, 0.0), axis=1, keepdims=True)
        parts.append((act[:, e * D_EXPERT:(e + 1) * D_EXPERT] * cw).astype(BF16))
    acc_ref[...] += _dot(jnp.concatenate(parts, axis=1), wd_ref[...])

    @pl.when(j == pl.num_programs(1) - 1)
    def _():
        o_ref[...] = x_ref[...] + mod_ref[5:6, :] * _rms(acc_ref[...], gpost_ref[...])


def _moe(x, mod, gpre, gpost, wr, br, wg, wu, wd):
    s = x.shape[0]
    tm = MOE_ROWS
    bw = MOE_EB * D_EXPERT
    return pl.pallas_call(
        _moe_kernel,
        grid=(s // tm, N_EXPERTS // MOE_EB),
        in_specs=[pl.BlockSpec((tm, D_MODEL), lambda i, j: (i, 0)),
                  _full(mod.shape), _full(gpre.shape), _full(gpost.shape), _full(wr.shape),
                  _full(br.shape),
                  pl.BlockSpec((D_MODEL, bw), lambda i, j: (0, j)),
                  pl.BlockSpec((D_MODEL, bw), lambda i, j: (0, j)),
                  pl.BlockSpec((bw, D_MODEL), lambda i, j: (j, 0))],
        out_specs=pl.BlockSpec((tm, D_MODEL), lambda i, j: (i, 0)),
        out_shape=jax.ShapeDtypeStruct((s, D_MODEL), F32),
        scratch_shapes=[pltpu.VMEM((tm, D_MODEL), BF16), pltpu.VMEM((tm, LANES), F32),
                        pltpu.VMEM((tm, D_MODEL), F32)],
        compiler_params=_params(("arbitrary", "arbitrary")),
        name="moe",
    )(x, mod, gpre, gpost, wr, br, wg, wu, wd)


def _rope_lane_constants():
    rot = DSA_HD // ROT_FRAC
    half = rot // 2
    freqs = np.power(np.float32(ROPE_THETA), -np.arange(half, dtype=np.float32) * np.float32(2.0) / rot)
    j = np.arange(LANES) % DSA_HD
    fv = np.where(j < rot, freqs[j % half], 0.0).astype(np.float32)
    sg = np.where(j < half, -1.0, np.where(j < rot, 1.0, 0.0)).astype(np.float32)
    return jnp.asarray(fv)[None, :], jnp.asarray(sg)[None, :]


def _pad_cols(w, n):
    return jnp.pad(w, ((0, 0), (0, n - w.shape[1])))


def _layer(x, c, pos, w_ada, b_ada, g_pre_mix, g_post_mix, g_pre_ffn, g_post_ffn, w_in, w_gla_a2,
           b_gla_a, g_gla_out, g_idx_k, w_proj_gla, w_proj_dsa, w_out, w_router_g, b_router_g,
           w_router_e, b_router_e, w_e_gate, w_e_up, w_e_down):
    s = x.shape[0]
    mod = _ada(jnp.broadcast_to(c, (8, D_MODEL)), w_ada, b_ada[None, :])[0].reshape(N_MOD, D_MODEL)

    o = np.cumsum((GLA_QK, GLA_QK, GLA_V, GLA_V, GLA_GATE_RANK, DSA_W, DSA_W, DSA_W, IDX_Q, IDX_DIM,
                   IDX_HEADS, 2 * D_MODEL))
    wb = w_in.astype(BF16)
    w_gla = wb[:, 0:o[3]]
    w_ga = _pad_cols(wb[:, o[3]:o[4]], LANES)
    w_dsa = wb[:, o[4]:o[8]]
    w_sm = _pad_cols(jnp.concatenate([wb[:, o[8]:o[9]], wb[:, o[9]:o[10]]], axis=1), LANES)
    w_bg = wb[:, o[10]:o[11]]
    w_a2 = jnp.pad(w_gla_a2.astype(BF16), ((0, LANES - GLA_GATE_RANK), (0, 0)))

    q_a, k_a, v_a, gg, la = _gla_proj(x, mod, g_pre_mix[None, :], w_gla, w_ga, w_a2, b_gla_a[None, :])
    o_a = _gla(q_a, k_a, v_a, gg, la, jnp.tile(g_gla_out, GLA_HEADS)[None, :])

    fv, sg = _rope_lane_constants()
    gik = jnp.pad(g_idx_k, (0, LANES - IDX_DIM))[None, :]
    q_b, k_b, v_b, qi, ki, wi = _dsa_proj(x, mod, g_pre_mix[None, :], pos.astype(F32)[:, None], fv, sg,
                                          w_dsa, w_sm, gik)
    nqb = s // DSA_TQ
    qi_r = qi.reshape(nqb, DSA_TQ, IDX_HEADS, IDX_DIM).transpose(0, 2, 1, 3).reshape(
        nqb, IDX_HEADS * DSA_TQ, IDX_DIM)
    o_b = _dsa(qi_r, ki.T, wi, q_b, k_b.T, v_b, min(DSA_TOPK_MAX, s // 4))

    x1 = _merge(x, mod, g_pre_mix[None, :], g_post_mix[None, :], o_a, o_b, w_bg,
                w_proj_gla.astype(BF16), w_proj_dsa.astype(BF16), w_out.astype(BF16))

    wr = _pad_cols(jnp.concatenate([w_router_g, w_router_e], axis=1).astype(BF16), LANES)
    br = jnp.pad(jnp.concatenate([b_router_g, b_router_e]), (0, LANES - N_GROUPS - N_EXPERTS))[None, :]
    wg = w_e_gate.astype(BF16).transpose(1, 0, 2).reshape(D_MODEL, N_EXPERTS * D_EXPERT)
    wu = w_e_up.astype(BF16).transpose(1, 0, 2).reshape(D_MODEL, N_EXPERTS * D_EXPERT)
    wd = w_e_down.astype(BF16).reshape(N_EXPERTS * D_EXPERT, D_MODEL)
    return _moe(x1, mod, g_pre_ffn[None, :], g_post_ffn[None, :], wr, br, wg, wu, wd)


def kernel(x, c, positions, w_ada, b_ada, g_pre_mix, g_post_mix, g_pre_ffn, g_post_ffn, w_in, w_gla_a2,
           b_gla_a, g_gla_out, g_idx_k, w_proj_gla, w_proj_dsa, w_out, w_router_g, b_router_g,
           w_router_e, b_router_e, w_e_gate, w_e_up, w_e_down):
    batch, depth = x.shape[0], w_ada.shape[0]
    outs = []
    for bi in range(batch):
        xb = x[bi]
        for l in range(depth):
            xb = _layer(xb, c[bi:bi + 1], positions[bi], w_ada[l], b_ada[l], g_pre_mix[l], g_post_mix[l],
                        g_pre_ffn[l], g_post_ffn[l], w_in[l], w_gla_a2[l], b_gla_a[l], g_gla_out[l],
                        g_idx_k[l], w_proj_gla[l], w_proj_dsa[l], w_out[l], w_router_g[l],
                        b_router_g[l], w_router_e[l], b_router_e[l], w_e_gate[l], w_e_up[l],
                        w_e_down[l])
        outs.append(xb)
    return jnp.stack(outs, axis=0)
```

```python
import functools

import numpy as np
import jax
import jax.numpy as jnp
from jax import lax
from jax.experimental import pallas as pl
from jax.experimental.pallas import tpu as pltpu

F32 = jnp.float32
BF16 = jnp.bfloat16
I32 = jnp.int32

D_MODEL = 1024
EPS = 1e-6
ROPE_THETA = 500000.0
ROT_FRAC = 4
GLA_HEADS = 4
GLA_DK = 64
GLA_DV = 128
GLA_GATE_RANK = 16
GLA_TAU = 16.0
GLA_CHUNK = 64
DSA_HEADS = 8
DSA_HD = 64
IDX_HEADS = 8
IDX_DIM = 64
DSA_TOPK_MAX = 256
N_GROUPS = 4
EXPERTS_PER_GROUP = 8
N_EXPERTS = N_GROUPS * EXPERTS_PER_GROUP
D_EXPERT = D_MODEL // 4
N_MOD = 6

GLA_QK = GLA_HEADS * GLA_DK
GLA_V = GLA_HEADS * GLA_DV
DSA_W = DSA_HEADS * DSA_HD
IDX_Q = IDX_HEADS * IDX_DIM

LANES = 128
VMEM_LIMIT_BYTES = 56 * 1024 * 1024

PROJ_ROWS = 512
GLA_ROWS = 512
DSA_TQ = 256
DSA_CH = 1024
DSA_CH_ATT = 512
MOE_ROWS = 512
MOE_EB = 8
MASKED = -1e30
LOG2E = float(np.log2(np.e))
GLA_LEVELS = 7


def _dot(a, b):
    return jnp.dot(a, b, preferred_element_type=F32)


def _dot_nt(a, b):
    return lax.dot_general(a, b, (((1,), (1,)), ((), ())), preferred_element_type=F32)


def _dot_tn(a, b):
    return lax.dot_general(a, b, (((0,), (0,)), ((), ())), preferred_element_type=F32)


def _sigmoid(x):
    return 1.0 / (1.0 + jnp.exp(-x))


def _rms(x, g):
    ms = jnp.mean(x * x, axis=-1, keepdims=True)
    return x * lax.rsqrt(ms + EPS) * g


def _modulated(x, g, mod_ref, shift_row, scale_row):
    return (_rms(x, g) * (1.0 + mod_ref[scale_row:scale_row + 1, :])
            + mod_ref[shift_row:shift_row + 1, :])


def _params(sem):
    return pltpu.CompilerParams(dimension_semantics=sem, vmem_limit_bytes=VMEM_LIMIT_BYTES)


def _full(shape):
    return pl.BlockSpec(shape, lambda *_: (0,) * len(shape))


def _resident(shape):
    return pl.BlockSpec(shape, lambda *_: (0,) * len(shape), pipeline_mode=pl.Buffered(1))


def _ada_kernel(c_ref, w_ref, b_ref, o_ref):
    c = c_ref[...]
    a = c * _sigmoid(c)
    o_ref[...] = _dot(a.astype(BF16), w_ref[...].astype(BF16)) + b_ref[...]


def _ada(c8, w, b):
    n = w.shape[1]
    bn = 1536
    return pl.pallas_call(
        _ada_kernel,
        grid=(n // bn,),
        in_specs=[_full(c8.shape),
                  pl.BlockSpec((D_MODEL, bn), lambda j: (0, j)),
                  pl.BlockSpec((1, bn), lambda j: (0, j))],
        out_specs=pl.BlockSpec((8, bn), lambda j: (0, j)),
        out_shape=jax.ShapeDtypeStruct((8, n), F32),
        compiler_params=_params(("arbitrary",)),
        name="ada",
    )(c8, w, b)


def _gla_proj_kernel(x_ref, mod_ref, g_ref, w_ref, wga_ref, wa2_ref, ba_ref,
                     q_ref, k_ref, v_ref, gg_ref, la_ref):
    h = _modulated(x_ref[...], g_ref[...], mod_ref, 0, 1)
    hb = h.astype(BF16)
    p = _dot(hb, w_ref[...])
    q_ref[...] = p[:, 0:GLA_QK] * (GLA_DK ** -0.5)
    k_ref[...] = p[:, GLA_QK:2 * GLA_QK]
    v_ref[...] = p[:, 2 * GLA_QK:2 * GLA_QK + GLA_V]
    gg_ref[...] = p[:, 2 * GLA_QK + GLA_V:]
    ga = _dot(hb, wga_ref[...])
    z = _dot(ga.astype(BF16), wa2_ref[...]) + ba_ref[...]
    log_sig = jnp.minimum(z, 0.0) - jnp.log1p(jnp.exp(-jnp.abs(z)))
    la_ref[...] = log_sig * (1.0 / GLA_TAU)


def _gla_proj(x, mod, g, w, wga, wa2, ba):
    s = x.shape[0]
    tm = PROJ_ROWS
    row = lambda n: pl.BlockSpec((tm, n), lambda i: (i, 0))
    return pl.pallas_call(
        _gla_proj_kernel,
        grid=(s // tm,),
        in_specs=[row(D_MODEL), _full(mod.shape), _full(g.shape), _full(w.shape),
                  _full(wga.shape), _full(wa2.shape), _full(ba.shape)],
        out_specs=[row(GLA_QK), row(GLA_QK), row(GLA_V), row(GLA_V), row(GLA_QK)],
        out_shape=[jax.ShapeDtypeStruct((s, GLA_QK), F32), jax.ShapeDtypeStruct((s, GLA_QK), F32),
                   jax.ShapeDtypeStruct((s, GLA_V), F32), jax.ShapeDtypeStruct((s, GLA_V), F32),
                   jax.ShapeDtypeStruct((s, GLA_QK), F32)],
        compiler_params=_params(("arbitrary",)),
        name="gla_proj",
    )(x, mod, g, w, wga, wa2, ba)


def _dsa_proj_kernel(x_ref, mod_ref, g_ref, pos_ref, fv_ref, sg_ref, w_ref, wsm_ref, gik_ref,
                     qb_ref, kb_ref, vb_ref, qi_ref, ki_ref, wi_ref):
    tm = x_ref.shape[0]
    h = _modulated(x_ref[...], g_ref[...], mod_ref, 0, 1)
    hb = h.astype(BF16)
    ang = pos_ref[...] * fv_ref[...]
    cs = jnp.cos(ang)
    sn = jnp.sin(ang) * sg_ref[...]
    lane = lax.broadcasted_iota(I32, (tm, LANES), 1)
    first = (lane & (DSA_HD - 1)) < (DSA_HD // ROT_FRAC // 2)

    def rope(t):
        width = t.shape[1]
        rep = width // LANES
        tile = (lambda a: jnp.concatenate([a] * rep, axis=1)) if rep > 1 else (lambda a: a)
        half = DSA_HD // ROT_FRAC // 2
        fwd = pltpu.roll(t, half, 1)
        bwd = pltpu.roll(t, width - half, 1)
        partner = jnp.where(tile(first), bwd, fwd)
        return t * tile(cs) + partner * tile(sn)

    p = _dot(hb, w_ref[...])
    qb_ref[...] = (rope(p[:, 0:DSA_W]) * (DSA_HD ** -0.5 * LOG2E)).astype(BF16)
    kb_ref[...] = rope(p[:, DSA_W:2 * DSA_W]).astype(BF16)
    vb_ref[...] = p[:, 2 * DSA_W:3 * DSA_W].astype(BF16)
    qi_ref[...] = (rope(p[:, 3 * DSA_W:]) * (IDX_DIM ** -0.5)).astype(BF16)

    sm = _dot(hb, wsm_ref[...])
    is_ik = lane < IDX_DIM
    mu = jnp.sum(jnp.where(is_ik, sm, 0.0), axis=-1, keepdims=True) * (1.0 / IDX_DIM)
    xc = jnp.where(is_ik, sm - mu, 0.0)
    var = jnp.sum(xc * xc, axis=-1, keepdims=True) * (1.0 / IDX_DIM)
    y = xc * lax.rsqrt(var + EPS) * gik_ref[...]
    ki_ref[...] = rope(y)[:, 0:IDX_DIM].astype(BF16)
    wi_ref[...] = sm * (IDX_HEADS ** -0.5)


def _dsa_proj(x, mod, g, pos, fv, sg, w, wsm, gik):
    s = x.shape[0]
    tm = PROJ_ROWS
    row = lambda n: pl.BlockSpec((tm, n), lambda i: (i, 0))
    return pl.pallas_call(
        _dsa_proj_kernel,
        grid=(s // tm,),
        in_specs=[row(D_MODEL), _full(mod.shape), _full(g.shape), row(1), _full(fv.shape),
                  _full(sg.shape), _full(w.shape), _full(wsm.shape), _full(gik.shape)],
        out_specs=[row(DSA_W), row(DSA_W), row(DSA_W), row(IDX_Q), row(IDX_DIM), row(LANES)],
        out_shape=[jax.ShapeDtypeStruct((s, DSA_W), BF16), jax.ShapeDtypeStruct((s, DSA_W), BF16),
                   jax.ShapeDtypeStruct((s, DSA_W), BF16), jax.ShapeDtypeStruct((s, IDX_Q), BF16),
                   jax.ShapeDtypeStruct((s, IDX_DIM), BF16), jax.ShapeDtypeStruct((s, LANES), F32)],
        compiler_params=_params(("arbitrary",)),
        name="dsa_proj",
    )(x, mod, g, pos, fv, sg, w, wsm, gik)


def _gla_constants():
    c = GLA_CHUNK
    tril = np.tril(np.ones((c, c), np.float32))
    t = np.arange(c)
    mats = [tril]
    masks = [(t[:, None] == t[None, :])]
    for hs in (32, 16, 8, 4, 2, 1):
        blk = 2 * hs
        r = (t // blk) * blk + hs - 1
        mats.append(tril - tril[r, :])
        same = (t[:, None] // blk) == (t[None, :] // blk)
        masks.append(same & ((t[:, None] % blk) >= hs) & ((t[None, :] % blk) < hs))
    m_all = np.concatenate(mats, axis=0)
    lvl = np.stack([np.tile(m.astype(np.float32), (1, GLA_HEADS)) for m in masks])
    hrow = np.arange(GLA_HEADS * c) // c
    wmask = (hrow[:, None] == (np.arange(GLA_QK) // GLA_DK)[None, :]).astype(np.float32)
    vmask = (hrow[:, None] == (np.arange(GLA_V) // GLA_DV)[None, :]).astype(np.float32)
    smask = ((np.arange(GLA_V) // GLA_DV)[:, None] == (np.arange(GLA_QK) // GLA_DK)[None, :])
    return (jnp.asarray(m_all, BF16), jnp.asarray(lvl), jnp.asarray(wmask), jnp.asarray(vmask),
            jnp.asarray(smask.astype(np.float32)))


def _gla_kernel(q_ref, k_ref, v_ref, gg_ref, la_ref, gout_ref, mall_ref, lvl_ref, wmask_ref,
                vmask_ref, smask_ref, o_ref, st_ref):
    c = GLA_CHUNK

    @pl.when(pl.program_id(0) == 0)
    def _():
        st_ref[...] = jnp.zeros_like(st_ref)

    m_all = mall_ref[...]
    wmask = wmask_ref[...]
    vmask = vmask_ref[...]
    smask = smask_ref[...]
    gout = gout_ref[...]

    def chunk(ci, carry):
        r0 = pl.multiple_of(ci * c, c)
        rows = pl.ds(r0, c)
        q = q_ref[rows, :]
        k = k_ref[rows, :]
        v = v_ref[rows, :]
        la = la_ref[rows, :]
        hi = la.astype(BF16)
        r1 = la - hi.astype(F32)
        mid = r1.astype(BF16)
        lo = (r1 - mid.astype(F32)).astype(BF16)
        dall = _dot(m_all, hi) + _dot(m_all, mid) + _dot(m_all, lo)
        b = dall[0:c]
        b_last = b[c - 1:c, :]
        qhat = q * jnp.exp(b)
        khat = k * jnp.exp(b_last - b)

        a = jnp.zeros((c, GLA_HEADS * c), F32)
        for lv in range(GLA_LEVELS):
            if lv == 0:
                qt, kt = q, k
            else:
                d = dall[lv * c:(lv + 1) * c]
                qt = q * jnp.exp(jnp.minimum(d, 0.0))
                kt = k * jnp.exp(jnp.minimum(-d, 0.0))
            w = (jnp.concatenate([kt] * GLA_HEADS, axis=0) * wmask).astype(BF16)
            a = a + lvl_ref[lv] * _dot_nt(qt.astype(BF16), w)

        st = st_ref[...]
        vbd = (jnp.concatenate([v] * GLA_HEADS, axis=0) * vmask).astype(BF16)
        o = _dot(a.astype(BF16), vbd) + _dot_nt(qhat.astype(BF16), st.astype(BF16))
        st_ref[...] = st * jnp.exp(b_last) + smask * _dot_tn(v.astype(BF16), khat.astype(BF16))

        parts = []
        for hh in range(GLA_HEADS):
            oh = o[:, hh * GLA_DV:(hh + 1) * GLA_DV]
            ms = jnp.mean(oh * oh, axis=-1, keepdims=True)
            parts.append(oh * lax.rsqrt(ms + EPS))
        gg = gg_ref[rows, :]
        o_ref[rows, :] = jnp.concatenate(parts, axis=1) * gout * (gg * _sigmoid(gg))
        return carry

    lax.fori_loop(0, q_ref.shape[0] // c, chunk, 0, unroll=4)


def _gla(q, k, v, gg, la, gout):
    s = q.shape[0]
    tb = GLA_ROWS
    consts = _gla_constants()
    row = lambda n: pl.BlockSpec((tb, n), lambda i: (i, 0))
    return pl.pallas_call(
        _gla_kernel,
        grid=(s // tb,),
        in_specs=[row(GLA_QK), row(GLA_QK), row(GLA_V), row(GLA_V), row(GLA_QK), _full(gout.shape)]
                 + [_full(a.shape) for a in consts],
        out_specs=row(GLA_V),
        out_shape=jax.ShapeDtypeStruct((s, GLA_V), F32),
        scratch_shapes=[pltpu.VMEM((GLA_V, GLA_QK), F32)],
        compiler_params=_params(("arbitrary",)),
        name="gla",
    )(q, k, v, gg, la, gout, *consts)


DSA_SLAB = 32
DSA_HG = 2
DSA_MAX_SHIFT = 48.0
WORD = 32
DSA_GROUP = WORD * LANES
LANE_BITS = LANES.bit_length() - 1
HD_BITS = DSA_HD.bit_length() - 1


def _float_to_ordered_bits(x):
    bits = lax.bitcast_convert_type(x, I32)
    key = bits ^ ((bits >> 31) & jnp.int32(0x7FFFFFFF))
    return key ^ jnp.int32(-2147483648)


def _transpose_bits(a):
    a = list(a)
    j, msk = 16, 0x0000FFFF
    while j:
        m32 = jnp.int32(np.array(msk, np.uint32).view(np.int32))
        k = 0
        while k < WORD:
            t = (a[k] ^ (a[k + j] >> j)) & m32
            a[k] = a[k] ^ t
            a[k + j] = a[k + j] ^ (t << j)
            k = (k + j + 1) & ~j
        j >>= 1
        msk = (msk ^ (msk << j)) & 0xFFFFFFFF
    return a


def _dsa_kernel(qi_ref, kit_ref, wi_ref, qb_ref, kt_hbm, v_hbm, o_ref,
                key_ref, alive_ref, great_ref, wb_ref, qbd_ref, s_ref, p_ref, bias_ref, m_ref, l_ref,
                alpha_ref, acc_ref, bnd_ref, kmax_ref, kbuf_ref, vbuf_ref, sem_ref, *, seq, topk):
    tq = DSA_TQ
    ch = DSA_CH
    hg = DSA_HG
    n_hg = DSA_HEADS // hg
    gw = hg * DSA_HD
    bpc = ch // LANES
    ca = DSA_CH_ATT
    bpa = ca // LANES
    cga = DSA_GROUP // ca
    n_ca = ((pl.program_id(0) + 1) * tq + ca - 1) // ca
    cpg = DSA_GROUP // ch
    n_groups = seq // DSA_GROUP
    i = pl.program_id(0)
    n_ch = ((i + 1) * tq + ch - 1) // ch
    lane_i = lax.broadcasted_iota(I32, (tq, LANES), 1)
    row_pos = lax.broadcasted_iota(I32, (tq, 1), 0) + i * tq
    kf = float(topk)

    def tile_l(a, width):
        return jnp.concatenate([a] * (width // a.shape[1]), axis=1)

    def chunk(c):
        return pl.ds(pl.multiple_of(c * ch, ch), ch)

    def key_copy(c, slot):
        cols = pl.ds(pl.multiple_of(c * ca, ca), ca)
        return pltpu.make_async_copy(kt_hbm.at[:, cols], kbuf_ref.at[slot], sem_ref.at[0, slot])

    def value_copy(c, slot):
        cols = pl.ds(pl.multiple_of(c * ca, ca), ca)
        return pltpu.make_async_copy(v_hbm.at[cols, :], vbuf_ref.at[slot], sem_ref.at[1, slot])

    def kv_copies(c, slot):
        return key_copy(c, slot), value_copy(c, slot)

    def lane_sum(a):
        return jnp.sum(a.astype(F32), axis=1, keepdims=True)

    @pl.when(i == 0)
    def _():
        key_ref[...] = jnp.zeros(key_ref.shape, I32)

        def norm_body(c, best):
            copy = key_copy(c, 0)
            copy.start()
            copy.wait()
            out = []
            for h in range(DSA_HEADS):
                kf32 = kbuf_ref[0, h * DSA_HD:(h + 1) * DSA_HD, :].astype(F32)
                out.append(jnp.maximum(best[h], jnp.sum(kf32 * kf32, axis=0, keepdims=True)))
            return tuple(out)

        best = lax.fori_loop(0, seq // ca, norm_body,
                             tuple(jnp.zeros((1, ca), F32) for _ in range(DSA_HEADS)))
        for h in range(DSA_HEADS):
            kmax_ref[h] = jnp.sqrt(jnp.max(best[h]))

    wv = wi_ref[...]
    for h in range(IDX_HEADS):
        wb_ref[h] = jnp.broadcast_to(wv[:, IDX_DIM + h:IDX_DIM + h + 1], (tq, LANES))

    def score_body(c, carry):
        lg = _dot(qi_ref[0], kit_ref[:, chunk(c)])
        sc = jnp.maximum(lg[0:tq], 0.0) * tile_l(wb_ref[0], ch)
        for h in range(1, IDX_HEADS):
            sc = sc + jnp.maximum(lg[h * tq:(h + 1) * tq], 0.0) * tile_l(wb_ref[h], ch)
        sc = jnp.where(sc == 0.0, 0.0, sc)
        key_ref[c] = _float_to_ordered_bits(sc)
        return carry

    lax.fori_loop(0, n_ch, score_body, 0)

    n_pg = (n_ch + cpg - 1) // cpg

    def plane_body(t, carry):
        g = t // (tq // 8)
        rows = pl.ds(pl.multiple_of((t % (tq // 8)) * 8, 8), 8)

        def slot(b):
            return (g * cpg + b // bpc, rows, slice((b % bpc) * LANES, (b % bpc + 1) * LANES))

        planes = _transpose_bits([key_ref[slot(WORD - 1 - k)] for k in range(WORD)])
        for b in range(WORD):
            key_ref[slot(b)] = planes[b]
        return carry

    lax.fori_loop(0, n_pg * (tq // 8), plane_body, 0)

    def index_mask(g, bound):
        r = bound - g * DSA_GROUP - lane_i
        q = jnp.clip((r + (LANES - 1)) >> LANE_BITS, 0, WORD)
        return jnp.where(q >= WORD, jnp.int32(-1), (jnp.int32(1) << jnp.minimum(q, WORD - 1)) - 1)

    for g in range(n_groups):
        alive_ref[g] = index_mask(g, row_pos + 1)
        great_ref[g] = jnp.zeros((tq, LANES), I32)

    def select_bits(ng):
        def bit_body(p, cnt_great):
            pc = p // bpc
            pl0 = pl.multiple_of((p % bpc) * LANES, LANES)
            ones = []
            acc = jnp.zeros((tq, LANES), I32)
            for g in range(ng):
                x = alive_ref[g] & key_ref[g * cpg + pc, :, pl.ds(pl0, LANES)]
                acc = acc + lax.population_count(x)
                ones.append(x)
            cnt_one = lane_sum(acc)
            take = (cnt_great + cnt_one) >= kf
            take_b = jnp.broadcast_to(take, (tq, LANES))
            for g in range(ng):
                a = alive_ref[g]
                alive_ref[g] = jnp.where(take_b, ones[g], a ^ ones[g])
                great_ref[g] = jnp.where(take_b, great_ref[g], great_ref[g] | ones[g])
            return jnp.where(take, cnt_great, cnt_great + cnt_one)

        return lambda: lax.fori_loop(0, WORD, bit_body, jnp.zeros((tq, 1), F32))

    cnt_great = lax.switch(n_pg - 1, [select_bits(ng) for ng in range(1, n_groups + 1)])
    need = kf - cnt_great

    def count_alive(bound):
        acc = jnp.zeros((tq, LANES), I32)
        for g in range(n_groups):
            acc = acc + lax.population_count(alive_ref[g] & index_mask(g, bound))
        return lane_sum(acc)

    tie_row = count_alive(jnp.full((tq, 1), seq, I32)) > need

    def tie_break(_):
        nbits = max(1, int(np.ceil(np.log2(seq))))

        def jbit(p, m):
            cand = m | jnp.left_shift(jnp.int32(1), nbits - 1 - p)
            return jnp.where(count_alive(cand) < need, cand, m)

        m = lax.fori_loop(0, nbits, jbit, jnp.zeros((tq, 1), I32))
        return jnp.where(tie_row, m + 1, jnp.int32(seq))

    any_tie = jnp.max(jnp.where(tie_row, 1.0, 0.0)) > 0.0
    bound = lax.cond(any_tie, tie_break, lambda _: jnp.full((tq, 1), seq, I32), 0)
    for g in range(n_groups):
        great_ref[g] = great_ref[g] | (alive_ref[g] & index_mask(g, bound))

    qt = qb_ref[...]
    head_of_lane = lax.broadcasted_iota(I32, (tq, gw), 1) >> HD_BITS
    for g in range(n_hg):
        qg = qt[:, g * gw:(g + 1) * gw]
        for h in range(hg):
            qbd_ref[g, h * tq:(h + 1) * tq, :] = jnp.where(head_of_lane == h, qg, jnp.zeros_like(qg))
    l_ref[...] = jnp.zeros(l_ref.shape, F32)
    acc_ref[...] = jnp.zeros(acc_ref.shape, F32)

    for g in range(n_hg):
        qf = qbd_ref[g].astype(F32)
        qn = jnp.sqrt(jnp.sum(qf * qf, axis=1, keepdims=True))
        kn = jnp.concatenate([jnp.full((tq, 1), kmax_ref[g * hg + h], F32) for h in range(hg)], axis=0)
        bnd_ref[g] = jnp.broadcast_to(qn * kn, (hg * tq, LANES))
    bound_ok = jnp.max(bnd_ref[...]) < DSA_MAX_SHIFT

    def bias_of(c):
        sel = great_ref[c // cga]
        for jj in range(bpa):
            bit = (sel >> ((c % cga) * bpa + jj)) & 1
            bias_ref[:, jj * LANES:(jj + 1) * LANES] = jnp.where(bit != 0, 0.0, MASKED)

    def over_key_chunks(body):
        for cp in kv_copies(0, 0):
            cp.start()

        def trip(c, carry):
            slot = c & 1
            for cp in kv_copies(c, slot):
                cp.wait()

            @pl.when(c + 1 < n_ca)
            def _():
                for cp in kv_copies(c + 1, 1 - slot):
                    cp.start()

            body(c, slot)
            return carry

        lax.fori_loop(0, n_ca, trip, 0)

    def bounded_body(c, slot):
        bias_of(c)
        for g in range(n_hg):
            s_ref[g] = _dot(qbd_ref[g], kbuf_ref[slot, g * gw:(g + 1) * gw, :])
        for g in range(n_hg):
            for r in range(hg * tq // DSA_SLAB):
                rows = slice(r * DSA_SLAB, (r + 1) * DSA_SLAB)
                b0 = (r * DSA_SLAB) % tq
                s = s_ref[g, rows, :] + bias_ref[b0:b0 + DSA_SLAB, :]
                p = jnp.exp2(s - tile_l(bnd_ref[g, rows, :], ca))
                l_ref[g, rows, :] = l_ref[g, rows, :] + jnp.sum(p, axis=1, keepdims=True)
                p_ref[g, rows, :] = p.astype(BF16)
            acc_ref[g] = acc_ref[g] + _dot(p_ref[g], vbuf_ref[slot, :, g * gw:(g + 1) * gw])

    def running_max_body(c, slot):
        bias_of(c)
        for g in range(n_hg):
            s_ref[g] = _dot(qbd_ref[g], kbuf_ref[slot, g * gw:(g + 1) * gw, :])
        for g in range(n_hg):
            for r in range(hg * tq // DSA_SLAB):
                rows = slice(r * DSA_SLAB, (r + 1) * DSA_SLAB)
                b0 = (r * DSA_SLAB) % tq
                s = s_ref[g, rows, :] + bias_ref[b0:b0 + DSA_SLAB, :]
                m_prev = m_ref[g, rows, :]
                m_new = jnp.maximum(m_prev, jnp.max(s, axis=1, keepdims=True))
                alpha = jnp.exp2(m_prev - m_new)
                p = jnp.exp2(s - tile_l(m_new, ca))
                l_ref[g, rows, :] = alpha * l_ref[g, rows, :] + jnp.sum(p, axis=1, keepdims=True)
                m_ref[g, rows, :] = m_new
                alpha_ref[g, rows, :] = alpha
                p_ref[g, rows, :] = p.astype(BF16)
            acc_ref[g] = (acc_ref[g] * tile_l(alpha_ref[g], gw)
                          + _dot(p_ref[g], vbuf_ref[slot, :, g * gw:(g + 1) * gw]))

    @pl.when(bound_ok)
    def _():
        over_key_chunks(bounded_body)

    @pl.when(jnp.logical_not(bound_ok))
    def _():
        m_ref[...] = jnp.full(m_ref.shape, MASKED, F32)
        over_key_chunks(running_max_body)

    outs = []
    for g in range(n_hg):
        a = acc_ref[g] * tile_l(1.0 / l_ref[g], gw)
        og = jnp.zeros((tq, gw), F32)
        for h in range(hg):
            og = og + jnp.where(head_of_lane == h, a[h * tq:(h + 1) * tq], 0.0)
        outs.append(og)
    o_ref[...] = jnp.concatenate(outs, axis=1)


def _dsa(qi_r, kit, wi, qb, kt, v, topk):
    s = qb.shape[0]
    tq = DSA_TQ
    assert s % DSA_GROUP == 0 and DSA_GROUP % DSA_CH == 0
    n_hg = DSA_HEADS // DSA_HG
    rows = DSA_HG * tq
    gw = DSA_HG * DSA_HD
    return pl.pallas_call(
        functools.partial(_dsa_kernel, seq=s, topk=topk),
        grid=(s // tq,),
        in_specs=[pl.BlockSpec((1, IDX_HEADS * tq, IDX_DIM), lambda i: (i, 0, 0)),
                  _resident(kit.shape),
                  pl.BlockSpec((tq, LANES), lambda i: (i, 0)),
                  pl.BlockSpec((tq, DSA_W), lambda i: (i, 0)),
                  pl.BlockSpec(memory_space=pl.ANY),
                  pl.BlockSpec(memory_space=pl.ANY)],
        out_specs=pl.BlockSpec((tq, DSA_W), lambda i: (i, 0)),
        out_shape=jax.ShapeDtypeStruct((s, DSA_W), F32),
        scratch_shapes=[pltpu.VMEM((s // DSA_CH, tq, DSA_CH), I32),
                        pltpu.VMEM((s // DSA_GROUP, tq, LANES), I32),
                        pltpu.VMEM((s // DSA_GROUP, tq, LANES), I32),
                        pltpu.VMEM((IDX_HEADS, tq, LANES), F32),
                        pltpu.VMEM((n_hg, rows, gw), BF16),
                        pltpu.VMEM((n_hg, rows, DSA_CH_ATT), F32),
                        pltpu.VMEM((n_hg, rows, DSA_CH_ATT), BF16),
                        pltpu.VMEM((tq, DSA_CH_ATT), F32),
                        pltpu.VMEM((n_hg, rows, LANES), F32),
                        pltpu.VMEM((n_hg, rows, LANES), F32),
                        pltpu.VMEM((n_hg, rows, LANES), F32),
                        pltpu.VMEM((n_hg, rows, gw), F32),
                        pltpu.VMEM((n_hg, rows, LANES), F32),
                        pltpu.SMEM((DSA_HEADS,), F32),
                        pltpu.VMEM((2, DSA_W, DSA_CH_ATT), BF16),
                        pltpu.VMEM((2, DSA_CH_ATT, DSA_W), BF16),
                        pltpu.SemaphoreType.DMA((2, 2))],
        compiler_params=_params(("arbitrary",)),
        name="dsa",
    )(qi_r, kit, wi, qb, kt, v)


def _merge_kernel(x_ref, mod_ref, gpre_ref, gpost_ref, oa_ref, ob_ref, wbg_ref, wpg_ref, wpd_ref,
                  wout_ref, o_ref):
    x = x_ref[...]
    h = _modulated(x, gpre_ref[...], mod_ref, 0, 1)
    gates = _sigmoid(_dot(h.astype(BF16), wbg_ref[...]))
    yg = _dot(oa_ref[...].astype(BF16), wpg_ref[...])
    yd = _dot(ob_ref[...].astype(BF16), wpd_ref[...])
    mix = gates[:, 0:D_MODEL] * yg + gates[:, D_MODEL:] * yd
    out = _dot(mix.astype(BF16), wout_ref[...])
    o_ref[...] = x + mod_ref[2:3, :] * _rms(out, gpost_ref[...])


def _merge(x, mod, gpre, gpost, oa, ob, wbg, wpg, wpd, wout):
    s = x.shape[0]
    tm = PROJ_ROWS
    row = lambda n: pl.BlockSpec((tm, n), lambda i: (i, 0))
    return pl.pallas_call(
        _merge_kernel,
        grid=(s // tm,),
        in_specs=[row(D_MODEL), _full(mod.shape), _full(gpre.shape), _full(gpost.shape),
                  row(GLA_V), row(DSA_W), _full(wbg.shape), _full(wpg.shape), _full(wpd.shape),
                  _full(wout.shape)],
        out_specs=row(D_MODEL),
        out_shape=jax.ShapeDtypeStruct((s, D_MODEL), F32),
        compiler_params=_params(("arbitrary",)),
        name="merge",
    )(x, mod, gpre, gpost, oa, ob, wbg, wpg, wpd, wout)


def _router(lg):
    t = lg.shape[0]
    lane = lax.broadcasted_iota(I32, (t, LANES), 1)
    lanef = lane.astype(F32)
    big = 1e9
    gm = lane < N_GROUPS
    gmax = jnp.max(jnp.where(gm, lg, -jnp.inf), axis=1, keepdims=True)
    gsum = jnp.sum(jnp.where(gm, jnp.exp(lg - gmax), 0.0), axis=1, keepdims=True)
    p_g = 1.0 / gsum
    g_sel = jnp.min(jnp.where(gm & (lg == gmax), lanef, big), axis=1, keepdims=True)
    lo = N_GROUPS + EXPERTS_PER_GROUP * g_sel
    em = (lanef >= lo) & (lanef < lo + EXPERTS_PER_GROUP)
    m1 = jnp.max(jnp.where(em, lg, -jnp.inf), axis=1, keepdims=True)
    i1 = jnp.min(jnp.where(em & (lg == m1), lanef, big), axis=1, keepdims=True)
    em2 = em & (lanef != i1)
    m2 = jnp.max(jnp.where(em2, lg, -jnp.inf), axis=1, keepdims=True)
    i2 = jnp.min(jnp.where(em2 & (lg == m2), lanef, big), axis=1, keepdims=True)
    e2 = jnp.exp(m2 - m1)
    inv = 1.0 / (1.0 + e2)
    return (jnp.where(lanef == i1, p_g * inv, 0.0) + jnp.where(lanef == i2, p_g * (e2 * inv), 0.0))


def _moe_kernel(x_ref, mod_ref, gpre_ref, gpost_ref, wr_ref, br_ref, wg_ref, wu_ref, wd_ref,
                o_ref, hb_ref, comb_ref, acc_ref):
    j = pl.program_id(1)
    tm = x_ref.shape[0]

    @pl.when(j == 0)
    def _():
        h = _modulated(x_ref[...], gpre_ref[...], mod_ref, 3, 4)
        hb = h.astype(BF16)
        hb_ref[...] = hb
        comb_ref[...] = _router(_dot(hb, wr_ref[...]) + br_ref[...])
        acc_ref[...] = jnp.zeros_like(acc_ref)

    hb = hb_ref[...]
    hgate = _dot(hb, wg_ref[...])
    hup = _dot(hb, wu_ref[...])
    act = hgate * _sigmoid(hgate) * hup
    comb = comb_ref[...]
    lane = lax.broadcasted_iota(I32, (tm, LANES), 1)
    parts = []
    for e in range(MOE_EB):
        sel = lane == (N_GROUPS + j * MOE_EB + e)
        cw = jnp.sum(jnp.where(sel, comb, 0.0), axis=1, keepdims=True)
        parts.append((act[:, e * D_EXPERT:(e + 1) * D_EXPERT] * cw).astype(BF16))
    acc_ref[...] += _dot(jnp.concatenate(parts, axis=1), wd_ref[...])

    @pl.when(j == pl.num_programs(1) - 1)
    def _():
        o_ref[...] = x_ref[...] + mod_ref[5:6, :] * _rms(acc_ref[...], gpost_ref[...])


def _moe(x, mod, gpre, gpost, wr, br, wg, wu, wd):
    s = x.shape[0]
    tm = MOE_ROWS
    bw = MOE_EB * D_EXPERT
    return pl.pallas_call(
        _moe_kernel,
        grid=(s // tm, N_EXPERTS // MOE_EB),
        in_specs=[pl.BlockSpec((tm, D_MODEL), lambda i, j: (i, 0)),
                  _full(mod.shape), _full(gpre.shape), _full(gpost.shape), _full(wr.shape),
                  _full(br.shape),
                  pl.BlockSpec((D_MODEL, bw), lambda i, j: (0, j)),
                  pl.BlockSpec((D_MODEL, bw), lambda i, j: (0, j)),
                  pl.BlockSpec((bw, D_MODEL), lambda i, j: (j, 0))],
        out_specs=pl.BlockSpec((tm, D_MODEL), lambda i, j: (i, 0)),
        out_shape=jax.ShapeDtypeStruct((s, D_MODEL), F32),
        scratch_shapes=[pltpu.VMEM((tm, D_MODEL), BF16), pltpu.VMEM((tm, LANES), F32),
                        pltpu.VMEM((tm, D_MODEL), F32)],
        compiler_params=_params(("arbitrary", "arbitrary")),
        name="moe",
    )(x, mod, gpre, gpost, wr, br, wg, wu, wd)


def _rope_lane_constants():
    rot = DSA_HD // ROT_FRAC
    half = rot // 2
    freqs = np.power(np.float32(ROPE_THETA), -np.arange(half, dtype=np.float32) * np.float32(2.0) / rot)
    j = np.arange(LANES) % DSA_HD
    fv = np.where(j < rot, freqs[j % half], 0.0).astype(np.float32)
    sg = np.where(j < half, -1.0, np.where(j < rot, 1.0, 0.0)).astype(np.float32)
    return jnp.asarray(fv)[None, :], jnp.asarray(sg)[None, :]


def _pad_cols(w, n):
    return jnp.pad(w, ((0, 0), (0, n - w.shape[1])))


def _layer(x, c, pos, w_ada, b_ada, g_pre_mix, g_post_mix, g_pre_ffn, g_post_ffn, w_in, w_gla_a2,
           b_gla_a, g_gla_out, g_idx_k, w_proj_gla, w_proj_dsa, w_out, w_router_g, b_router_g,
           w_router_e, b_router_e, w_e_gate, w_e_up, w_e_down):
    s = x.shape[0]
    mod = _ada(jnp.broadcast_to(c, (8, D_MODEL)), w_ada, b_ada[None, :])[0].reshape(N_MOD, D_MODEL)

    o = np.cumsum((GLA_QK, GLA_QK, GLA_V, GLA_V, GLA_GATE_RANK, DSA_W, DSA_W, DSA_W, IDX_Q, IDX_DIM,
                   IDX_HEADS, 2 * D_MODEL))
    wb = w_in.astype(BF16)
    w_gla = wb[:, 0:o[3]]
    w_ga = _pad_cols(wb[:, o[3]:o[4]], LANES)
    w_dsa = wb[:, o[4]:o[8]]
    w_sm = _pad_cols(jnp.concatenate([wb[:, o[8]:o[9]], wb[:, o[9]:o[10]]], axis=1), LANES)
    w_bg = wb[:, o[10]:o[11]]
    w_a2 = jnp.pad(w_gla_a2.astype(BF16), ((0, LANES - GLA_GATE_RANK), (0, 0)))

    q_a, k_a, v_a, gg, la = _gla_proj(x, mod, g_pre_mix[None, :], w_gla, w_ga, w_a2, b_gla_a[None, :])
    o_a = _gla(q_a, k_a, v_a, gg, la, jnp.tile(g_gla_out, GLA_HEADS)[None, :])

    fv, sg = _rope_lane_constants()
    gik = jnp.pad(g_idx_k, (0, LANES - IDX_DIM))[None, :]
    q_b, k_b, v_b, qi, ki, wi = _dsa_proj(x, mod, g_pre_mix[None, :], pos.astype(F32)[:, None], fv, sg,
                                          w_dsa, w_sm, gik)
    nqb = s // DSA_TQ
    qi_r = qi.reshape(nqb, DSA_TQ, IDX_HEADS, IDX_DIM).transpose(0, 2, 1, 3).reshape(
        nqb, IDX_HEADS * DSA_TQ, IDX_DIM)
    o_b = _dsa(qi_r, ki.T, wi, q_b, k_b.T, v_b, min(DSA_TOPK_MAX, s // 4))

    x1 = _merge(x, mod, g_pre_mix[None, :], g_post_mix[None, :], o_a, o_b, w_bg,
                w_proj_gla.astype(BF16), w_proj_dsa.astype(BF16), w_out.astype(BF16))

    wr = _pad_cols(jnp.concatenate([w_router_g, w_router_e], axis=1).astype(BF16), LANES)
    br = jnp.pad(jnp.concatenate([b_router_g, b_router_e]), (0, LANES - N_GROUPS - N_EXPERTS))[None, :]
    wg = w_e_gate.astype(BF16).transpose(1, 0, 2).reshape(D_MODEL, N_EXPERTS * D_EXPERT)
    wu = w_e_up.astype(BF16).transpose(1, 0, 2).reshape(D_MODEL, N_EXPERTS * D_EXPERT)
    wd = w_e_down.astype(BF16).reshape(N_EXPERTS * D_EXPERT, D_MODEL)
    return _moe(x1, mod, g_pre_ffn[None, :], g_post_ffn[None, :], wr, br, wg, wu, wd)


def kernel(x, c, positions, w_ada, b_ada, g_pre_mix, g_post_mix, g_pre_ffn, g_post_ffn, w_in, w_gla_a2,
           b_gla_a, g_gla_out, g_idx_k, w_proj_gla, w_proj_dsa, w_out, w_router_g, b_router_g,
           w_router_e, b_router_e, w_e_gate, w_e_up, w_e_down):
    batch, depth = x.shape[0], w_ada.shape[0]
    outs = []
    for bi in range(batch):
        xb = x[bi]
        for l in range(depth):
            xb = _layer(xb, c[bi:bi + 1], positions[bi], w_ada[l], b_ada[l], g_pre_mix[l], g_post_mix[l],
                        g_pre_ffn[l], g_post_ffn[l], w_in[l], w_gla_a2[l], b_gla_a[l], g_gla_out[l],
                        g_idx_k[l], w_proj_gla[l], w_proj_dsa[l], w_out[l], w_router_g[l],
                        b_router_g[l], w_router_e[l], b_router_e[l], w_e_gate[l], w_e_up[l],
                        w_e_down[l])
        outs.append(xb)
    return jnp.stack(outs, axis=0)
```

```python
import functools

import numpy as np
import jax
import jax.numpy as jnp
from jax import lax
from jax.experimental import pallas as pl
from jax.experimental.pallas import tpu as pltpu

F32 = jnp.float32
BF16 = jnp.bfloat16
I32 = jnp.int32

D_MODEL = 1024
EPS = 1e-6
ROPE_THETA = 500000.0
ROT_FRAC = 4
GLA_HEADS = 4
GLA_DK = 64
GLA_DV = 128
GLA_GATE_RANK = 16
GLA_TAU = 16.0
GLA_CHUNK = 64
DSA_HEADS = 8
DSA_HD = 64
IDX_HEADS = 8
IDX_DIM = 64
DSA_TOPK_MAX = 256
N_GROUPS = 4
EXPERTS_PER_GROUP = 8
N_EXPERTS = N_GROUPS * EXPERTS_PER_GROUP
D_EXPERT = D_MODEL // 4
N_MOD = 6

GLA_QK = GLA_HEADS * GLA_DK
GLA_V = GLA_HEADS * GLA_DV
DSA_W = DSA_HEADS * DSA_HD
IDX_Q = IDX_HEADS * IDX_DIM

LANES = 128
VMEM_LIMIT_BYTES = 56 * 1024 * 1024

PROJ_ROWS = 512
GLA_ROWS = 512
DSA_TQ = 256
DSA_CH = 1024
DSA_CH_ATT = 512
MOE_ROWS = 512
MOE_EB = 8
MOE_CAST_EB = 4
MASKED = -1e30
LOG2E = float(np.log2(np.e))
GLA_LEVELS = 7


def _dot(a, b):
    return jnp.dot(a, b, preferred_element_type=F32)


def _dot_nt(a, b):
    return lax.dot_general(a, b, (((1,), (1,)), ((), ())), preferred_element_type=F32)


def _dot_tn(a, b):
    return lax.dot_general(a, b, (((0,), (0,)), ((), ())), preferred_element_type=F32)


def _sigmoid(x):
    return 1.0 / (1.0 + jnp.exp(-x))


def _rms(x, g):
    ms = jnp.mean(x * x, axis=-1, keepdims=True)
    return x * lax.rsqrt(ms + EPS) * g


def _modulated(x, g, mod_ref, shift_row, scale_row):
    return (_rms(x, g) * (1.0 + mod_ref[scale_row:scale_row + 1, :])
            + mod_ref[shift_row:shift_row + 1, :])


def _params(sem):
    return pltpu.CompilerParams(dimension_semantics=sem, vmem_limit_bytes=VMEM_LIMIT_BYTES)


def _full(shape):
    return pl.BlockSpec(shape, lambda *_: (0,) * len(shape))


def _resident(shape):
    return pl.BlockSpec(shape, lambda *_: (0,) * len(shape), pipeline_mode=pl.Buffered(1))


def _ada_kernel(c_ref, w_ref, b_ref, o_ref):
    c = c_ref[...]
    a = c * _sigmoid(c)
    o_ref[...] = _dot(a.astype(BF16), w_ref[...].astype(BF16)) + b_ref[...]


def _ada(c8, w, b):
    n = w.shape[1]
    bn = 1536
    return pl.pallas_call(
        _ada_kernel,
        grid=(n // bn,),
        in_specs=[_full(c8.shape),
                  pl.BlockSpec((D_MODEL, bn), lambda j: (0, j)),
                  pl.BlockSpec((1, bn), lambda j: (0, j))],
        out_specs=pl.BlockSpec((8, bn), lambda j: (0, j)),
        out_shape=jax.ShapeDtypeStruct((8, n), F32),
        compiler_params=_params(("arbitrary",)),
        name="ada",
    )(c8, w, b)


def _gla_proj_kernel(x_ref, mod_ref, g_ref, w_ref, wga_ref, wa2_ref, ba_ref,
                     q_ref, k_ref, v_ref, gg_ref, la_ref):
    h = _modulated(x_ref[...], g_ref[...], mod_ref, 0, 1)
    hb = h.astype(BF16)
    p = _dot(hb, w_ref[...])
    q_ref[...] = p[:, 0:GLA_QK] * (GLA_DK ** -0.5)
    k_ref[...] = p[:, GLA_QK:2 * GLA_QK]
    v_ref[...] = p[:, 2 * GLA_QK:2 * GLA_QK + GLA_V]
    gg_ref[...] = p[:, 2 * GLA_QK + GLA_V:]
    ga = _dot(hb, wga_ref[...])
    z = _dot(ga.astype(BF16), wa2_ref[...]) + ba_ref[...]
    log_sig = jnp.minimum(z, 0.0) - jnp.log1p(jnp.exp(-jnp.abs(z)))
    la_ref[...] = log_sig * (1.0 / GLA_TAU)


def _gla_proj(x, mod, g, w, wga, wa2, ba):
    s = x.shape[0]
    tm = PROJ_ROWS
    row = lambda n: pl.BlockSpec((tm, n), lambda i: (i, 0))
    return pl.pallas_call(
        _gla_proj_kernel,
        grid=(s // tm,),
        in_specs=[row(D_MODEL), _full(mod.shape), _full(g.shape), _full(w.shape),
                  _full(wga.shape), _full(wa2.shape), _full(ba.shape)],
        out_specs=[row(GLA_QK), row(GLA_QK), row(GLA_V), row(GLA_V), row(GLA_QK)],
        out_shape=[jax.ShapeDtypeStruct((s, GLA_QK), F32), jax.ShapeDtypeStruct((s, GLA_QK), F32),
                   jax.ShapeDtypeStruct((s, GLA_V), F32), jax.ShapeDtypeStruct((s, GLA_V), F32),
                   jax.ShapeDtypeStruct((s, GLA_QK), F32)],
        compiler_params=_params(("arbitrary",)),
        name="gla_proj",
    )(x, mod, g, w, wga, wa2, ba)


def _dsa_proj_kernel(x_ref, mod_ref, g_ref, pos_ref, fv_ref, sg_ref, w_ref, wsm_ref, gik_ref,
                     qb_ref, kb_ref, vb_ref, qi_ref, ki_ref, wi_ref):
    tm = x_ref.shape[0]
    h = _modulated(x_ref[...], g_ref[...], mod_ref, 0, 1)
    hb = h.astype(BF16)
    ang = pos_ref[...] * fv_ref[...]
    cs = jnp.cos(ang)
    sn = jnp.sin(ang) * sg_ref[...]
    lane = lax.broadcasted_iota(I32, (tm, LANES), 1)
    first = (lane & (DSA_HD - 1)) < (DSA_HD // ROT_FRAC // 2)

    def rope(t):
        width = t.shape[1]
        rep = width // LANES
        tile = (lambda a: jnp.concatenate([a] * rep, axis=1)) if rep > 1 else (lambda a: a)
        half = DSA_HD // ROT_FRAC // 2
        fwd = pltpu.roll(t, half, 1)
        bwd = pltpu.roll(t, width - half, 1)
        partner = jnp.where(tile(first), bwd, fwd)
        return t * tile(cs) + partner * tile(sn)

    p = _dot(hb, w_ref[...])
    qb_ref[...] = (rope(p[:, 0:DSA_W]) * (DSA_HD ** -0.5 * LOG2E)).astype(BF16)
    kb_ref[...] = rope(p[:, DSA_W:2 * DSA_W]).astype(BF16)
    vb_ref[...] = p[:, 2 * DSA_W:3 * DSA_W].astype(BF16)
    qi_ref[...] = (rope(p[:, 3 * DSA_W:]) * (IDX_DIM ** -0.5)).astype(BF16)

    sm = _dot(hb, wsm_ref[...])
    is_ik = lane < IDX_DIM
    mu = jnp.sum(jnp.where(is_ik, sm, 0.0), axis=-1, keepdims=True) * (1.0 / IDX_DIM)
    xc = jnp.where(is_ik, sm - mu, 0.0)
    var = jnp.sum(xc * xc, axis=-1, keepdims=True) * (1.0 / IDX_DIM)
    y = xc * lax.rsqrt(var + EPS) * gik_ref[...]
    ki_ref[...] = rope(y)[:, 0:IDX_DIM].astype(BF16)
    wi_ref[...] = sm * (IDX_HEADS ** -0.5)


def _dsa_proj(x, mod, g, pos, fv, sg, w, wsm, gik):
    s = x.shape[0]
    tm = PROJ_ROWS
    row = lambda n: pl.BlockSpec((tm, n), lambda i: (i, 0))
    return pl.pallas_call(
        _dsa_proj_kernel,
        grid=(s // tm,),
        in_specs=[row(D_MODEL), _full(mod.shape), _full(g.shape), row(1), _full(fv.shape),
                  _full(sg.shape), _full(w.shape), _full(wsm.shape), _full(gik.shape)],
        out_specs=[row(DSA_W), row(DSA_W), row(DSA_W), row(IDX_Q), row(IDX_DIM), row(LANES)],
        out_shape=[jax.ShapeDtypeStruct((s, DSA_W), BF16), jax.ShapeDtypeStruct((s, DSA_W), BF16),
                   jax.ShapeDtypeStruct((s, DSA_W), BF16), jax.ShapeDtypeStruct((s, IDX_Q), BF16),
                   jax.ShapeDtypeStruct((s, IDX_DIM), BF16), jax.ShapeDtypeStruct((s, LANES), F32)],
        compiler_params=_params(("arbitrary",)),
        name="dsa_proj",
    )(x, mod, g, pos, fv, sg, w, wsm, gik)


def _gla_constants():
    c = GLA_CHUNK
    tril = np.tril(np.ones((c, c), np.float32))
    t = np.arange(c)
    mats = [tril]
    masks = [(t[:, None] == t[None, :])]
    for hs in (32, 16, 8, 4, 2, 1):
        blk = 2 * hs
        r = (t // blk) * blk + hs - 1
        mats.append(tril - tril[r, :])
        same = (t[:, None] // blk) == (t[None, :] // blk)
        masks.append(same & ((t[:, None] % blk) >= hs) & ((t[None, :] % blk) < hs))
    m_all = np.concatenate(mats, axis=0)
    lvl = np.stack([np.tile(m.astype(np.float32), (1, GLA_HEADS)) for m in masks])
    hrow = np.arange(GLA_HEADS * c) // c
    wmask = (hrow[:, None] == (np.arange(GLA_QK) // GLA_DK)[None, :]).astype(np.float32)
    vmask = (hrow[:, None] == (np.arange(GLA_V) // GLA_DV)[None, :]).astype(np.float32)
    smask = ((np.arange(GLA_V) // GLA_DV)[:, None] == (np.arange(GLA_QK) // GLA_DK)[None, :])
    return (jnp.asarray(m_all, BF16), jnp.asarray(lvl), jnp.asarray(wmask), jnp.asarray(vmask),
            jnp.asarray(smask.astype(np.float32)))


def _gla_kernel(q_ref, k_ref, v_ref, gg_ref, la_ref, gout_ref, mall_ref, lvl_ref, wmask_ref,
                vmask_ref, smask_ref, o_ref, st_ref):
    c = GLA_CHUNK

    @pl.when(pl.program_id(0) == 0)
    def _():
        st_ref[...] = jnp.zeros_like(st_ref)

    m_all = mall_ref[...]
    wmask = wmask_ref[...]
    vmask = vmask_ref[...]
    smask = smask_ref[...]
    gout = gout_ref[...]

    def chunk(ci, carry):
        r0 = pl.multiple_of(ci * c, c)
        rows = pl.ds(r0, c)
        q = q_ref[rows, :]
        k = k_ref[rows, :]
        v = v_ref[rows, :]
        la = la_ref[rows, :]
        hi = la.astype(BF16)
        r1 = la - hi.astype(F32)
        mid = r1.astype(BF16)
        lo = (r1 - mid.astype(F32)).astype(BF16)
        dall = _dot(m_all, hi) + _dot(m_all, mid) + _dot(m_all, lo)
        b = dall[0:c]
        b_last = b[c - 1:c, :]
        qhat = q * jnp.exp(b)
        khat = k * jnp.exp(b_last - b)

        a = jnp.zeros((c, GLA_HEADS * c), F32)
        for lv in range(GLA_LEVELS):
            if lv == 0:
                qt, kt = q, k
            else:
                d = dall[lv * c:(lv + 1) * c]
                qt = q * jnp.exp(jnp.minimum(d, 0.0))
                kt = k * jnp.exp(jnp.minimum(-d, 0.0))
            w = (jnp.concatenate([kt] * GLA_HEADS, axis=0) * wmask).astype(BF16)
            a = a + lvl_ref[lv] * _dot_nt(qt.astype(BF16), w)

        st = st_ref[...]
        vbd = (jnp.concatenate([v] * GLA_HEADS, axis=0) * vmask).astype(BF16)
        o = _dot(a.astype(BF16), vbd) + _dot_nt(qhat.astype(BF16), st.astype(BF16))
        st_ref[...] = st * jnp.exp(b_last) + smask * _dot_tn(v.astype(BF16), khat.astype(BF16))

        parts = []
        for hh in range(GLA_HEADS):
            oh = o[:, hh * GLA_DV:(hh + 1) * GLA_DV]
            ms = jnp.mean(oh * oh, axis=-1, keepdims=True)
            parts.append(oh * lax.rsqrt(ms + EPS))
        gg = gg_ref[rows, :]
        o_ref[rows, :] = jnp.concatenate(parts, axis=1) * gout * (gg * _sigmoid(gg))
        return carry

    lax.fori_loop(0, q_ref.shape[0] // c, chunk, 0, unroll=4)


def _gla(q, k, v, gg, la, gout):
    s = q.shape[0]
    tb = GLA_ROWS
    consts = _gla_constants()
    row = lambda n: pl.BlockSpec((tb, n), lambda i: (i, 0))
    return pl.pallas_call(
        _gla_kernel,
        grid=(s // tb,),
        in_specs=[row(GLA_QK), row(GLA_QK), row(GLA_V), row(GLA_V), row(GLA_QK), _full(gout.shape)]
                 + [_full(a.shape) for a in consts],
        out_specs=row(GLA_V),
        out_shape=jax.ShapeDtypeStruct((s, GLA_V), F32),
        scratch_shapes=[pltpu.VMEM((GLA_V, GLA_QK), F32)],
        compiler_params=_params(("arbitrary",)),
        name="gla",
    )(q, k, v, gg, la, gout, *consts)


DSA_SLAB = 32
DSA_HG = 2
DSA_MAX_SHIFT = 48.0
WORD = 32
DSA_GROUP = WORD * LANES
LANE_BITS = LANES.bit_length() - 1
HD_BITS = DSA_HD.bit_length() - 1


def _float_to_ordered_bits(x):
    bits = lax.bitcast_convert_type(x, I32)
    key = bits ^ ((bits >> 31) & jnp.int32(0x7FFFFFFF))
    return key ^ jnp.int32(-2147483648)


def _transpose_bits(a):
    a = list(a)
    j, msk = 16, 0x0000FFFF
    while j:
        m32 = jnp.int32(np.array(msk, np.uint32).view(np.int32))
        k = 0
        while k < WORD:
            t = (a[k] ^ (a[k + j] >> j)) & m32
            a[k] = a[k] ^ t
            a[k + j] = a[k + j] ^ (t << j)
            k = (k + j + 1) & ~j
        j >>= 1
        msk = (msk ^ (msk << j)) & 0xFFFFFFFF
    return a


def _dsa_kernel(qi_ref, kit_ref, wi_ref, qb_ref, kt_hbm, v_hbm, o_ref,
                key_ref, alive_ref, great_ref, wb_ref, qbd_ref, s_ref, p_ref, bias_ref, m_ref, l_ref,
                alpha_ref, acc_ref, bnd_ref, kmax_ref, kbuf_ref, vbuf_ref, sem_ref, *, seq, topk):
    tq = DSA_TQ
    ch = DSA_CH
    hg = DSA_HG
    n_hg = DSA_HEADS // hg
    gw = hg * DSA_HD
    bpc = ch // LANES
    ca = DSA_CH_ATT
    bpa = ca // LANES
    cga = DSA_GROUP // ca
    n_ca = ((pl.program_id(0) + 1) * tq + ca - 1) // ca
    cpg = DSA_GROUP // ch
    n_groups = seq // DSA_GROUP
    i = pl.program_id(0)
    n_ch = ((i + 1) * tq + ch - 1) // ch
    lane_i = lax.broadcasted_iota(I32, (tq, LANES), 1)
    row_pos = lax.broadcasted_iota(I32, (tq, 1), 0) + i * tq
    kf = float(topk)

    def tile_l(a, width):
        return jnp.concatenate([a] * (width // a.shape[1]), axis=1)

    def chunk(c):
        return pl.ds(pl.multiple_of(c * ch, ch), ch)

    def key_copy(c, slot):
        cols = pl.ds(pl.multiple_of(c * ca, ca), ca)
        return pltpu.make_async_copy(kt_hbm.at[:, cols], kbuf_ref.at[slot], sem_ref.at[0, slot])

    def value_copy(c, slot):
        cols = pl.ds(pl.multiple_of(c * ca, ca), ca)
        return pltpu.make_async_copy(v_hbm.at[cols, :], vbuf_ref.at[slot], sem_ref.at[1, slot])

    def kv_copies(c, slot):
        return key_copy(c, slot), value_copy(c, slot)

    def lane_sum(a):
        return jnp.sum(a.astype(F32), axis=1, keepdims=True)

    @pl.when(i == 0)
    def _():
        key_ref[...] = jnp.zeros(key_ref.shape, I32)

        def norm_body(c, best):
            copy = key_copy(c, 0)
            copy.start()
            copy.wait()
            out = []
            for h in range(DSA_HEADS):
                kf32 = kbuf_ref[0, h * DSA_HD:(h + 1) * DSA_HD, :].astype(F32)
                out.append(jnp.maximum(best[h], jnp.sum(kf32 * kf32, axis=0, keepdims=True)))
            return tuple(out)

        best = lax.fori_loop(0, seq // ca, norm_body,
                             tuple(jnp.zeros((1, ca), F32) for _ in range(DSA_HEADS)))
        for h in range(DSA_HEADS):
            kmax_ref[h] = jnp.sqrt(jnp.max(best[h]))

    wv = wi_ref[...]
    for h in range(IDX_HEADS):
        wb_ref[h] = jnp.broadcast_to(wv[:, IDX_DIM + h:IDX_DIM + h + 1], (tq, LANES))

    def score_body(c, carry):
        lg = _dot(qi_ref[0], kit_ref[:, chunk(c)])
        sc = jnp.maximum(lg[0:tq], 0.0) * tile_l(wb_ref[0], ch)
        for h in range(1, IDX_HEADS):
            sc = sc + jnp.maximum(lg[h * tq:(h + 1) * tq], 0.0) * tile_l(wb_ref[h], ch)
        sc = jnp.where(sc == 0.0, 0.0, sc)
        key_ref[c] = _float_to_ordered_bits(sc)
        return carry

    lax.fori_loop(0, n_ch, score_body, 0)

    n_pg = (n_ch + cpg - 1) // cpg

    def plane_body(t, carry):
        g = t // (tq // 8)
        rows = pl.ds(pl.multiple_of((t % (tq // 8)) * 8, 8), 8)

        def slot(b):
            return (g * cpg + b // bpc, rows, slice((b % bpc) * LANES, (b % bpc + 1) * LANES))

        planes = _transpose_bits([key_ref[slot(WORD - 1 - k)] for k in range(WORD)])
        for b in range(WORD):
            key_ref[slot(b)] = planes[b]
        return carry

    lax.fori_loop(0, n_pg * (tq // 8), plane_body, 0)

    def index_mask(g, bound):
        r = bound - g * DSA_GROUP - lane_i
        q = jnp.clip((r + (LANES - 1)) >> LANE_BITS, 0, WORD)
        return jnp.where(q >= WORD, jnp.int32(-1), (jnp.int32(1) << jnp.minimum(q, WORD - 1)) - 1)

    for g in range(n_groups):
        alive_ref[g] = index_mask(g, row_pos + 1)
        great_ref[g] = jnp.zeros((tq, LANES), I32)

    def select_bits(ng):
        def bit_body(p, cnt_great):
            pc = p // bpc
            pl0 = pl.multiple_of((p % bpc) * LANES, LANES)
            ones = []
            acc = jnp.zeros((tq, LANES), I32)
            for g in range(ng):
                x = alive_ref[g] & key_ref[g * cpg + pc, :, pl.ds(pl0, LANES)]
                acc = acc + lax.population_count(x)
                ones.append(x)
            cnt_one = lane_sum(acc)
            take = (cnt_great + cnt_one) >= kf
            take_b = jnp.broadcast_to(take, (tq, LANES))
            for g in range(ng):
                a = alive_ref[g]
                alive_ref[g] = jnp.where(take_b, ones[g], a ^ ones[g])
                great_ref[g] = jnp.where(take_b, great_ref[g], great_ref[g] | ones[g])
            return jnp.where(take, cnt_great, cnt_great + cnt_one)

        return lambda: lax.fori_loop(0, WORD, bit_body, jnp.zeros((tq, 1), F32))

    cnt_great = lax.switch(n_pg - 1, [select_bits(ng) for ng in range(1, n_groups + 1)])
    need = kf - cnt_great

    def count_alive(bound):
        acc = jnp.zeros((tq, LANES), I32)
        for g in range(n_groups):
            acc = acc + lax.population_count(alive_ref[g] & index_mask(g, bound))
        return lane_sum(acc)

    tie_row = count_alive(jnp.full((tq, 1), seq, I32)) > need

    def tie_break(_):
        nbits = max(1, int(np.ceil(np.log2(seq))))

        def jbit(p, m):
            cand = m | jnp.left_shift(jnp.int32(1), nbits - 1 - p)
            return jnp.where(count_alive(cand) < need, cand, m)

        m = lax.fori_loop(0, nbits, jbit, jnp.zeros((tq, 1), I32))
        return jnp.where(tie_row, m + 1, jnp.int32(seq))

    any_tie = jnp.max(jnp.where(tie_row, 1.0, 0.0)) > 0.0
    bound = lax.cond(any_tie, tie_break, lambda _: jnp.full((tq, 1), seq, I32), 0)
    for g in range(n_groups):
        great_ref[g] = great_ref[g] | (alive_ref[g] & index_mask(g, bound))

    qt = qb_ref[...]
    head_of_lane = lax.broadcasted_iota(I32, (tq, gw), 1) >> HD_BITS
    for g in range(n_hg):
        qg = qt[:, g * gw:(g + 1) * gw]
        for h in range(hg):
            qbd_ref[g, h * tq:(h + 1) * tq, :] = jnp.where(head_of_lane == h, qg, jnp.zeros_like(qg))
    l_ref[...] = jnp.zeros(l_ref.shape, F32)
    acc_ref[...] = jnp.zeros(acc_ref.shape, F32)

    for g in range(n_hg):
        qf = qbd_ref[g].astype(F32)
        qn = jnp.sqrt(jnp.sum(qf * qf, axis=1, keepdims=True))
        kn = jnp.concatenate([jnp.full((tq, 1), kmax_ref[g * hg + h], F32) for h in range(hg)], axis=0)
        bnd_ref[g] = jnp.broadcast_to(qn * kn, (hg * tq, LANES))
    bound_ok = jnp.max(bnd_ref[...]) < DSA_MAX_SHIFT

    def bias_of(c):
        sel = great_ref[c // cga]
        for jj in range(bpa):
            bit = (sel >> ((c % cga) * bpa + jj)) & 1
            bias_ref[:, jj * LANES:(jj + 1) * LANES] = jnp.where(bit != 0, 0.0, MASKED)

    def over_key_chunks(body):
        for cp in kv_copies(0, 0):
            cp.start()

        def trip(c, carry):
            slot = c & 1
            for cp in kv_copies(c, slot):
                cp.wait()

            @pl.when(c + 1 < n_ca)
            def _():
                for cp in kv_copies(c + 1, 1 - slot):
                    cp.start()

            body(c, slot)
            return carry

        lax.fori_loop(0, n_ca, trip, 0)

    def bounded_body(c, slot):
        bias_of(c)
        for g in range(n_hg):
            s_ref[g] = _dot(qbd_ref[g], kbuf_ref[slot, g * gw:(g + 1) * gw, :])
        for g in range(n_hg):
            for r in range(hg * tq // DSA_SLAB):
                rows = slice(r * DSA_SLAB, (r + 1) * DSA_SLAB)
                b0 = (r * DSA_SLAB) % tq
                s = s_ref[g, rows, :] + bias_ref[b0:b0 + DSA_SLAB, :]
                p = jnp.exp2(s - tile_l(bnd_ref[g, rows, :], ca))
                l_ref[g, rows, :] = l_ref[g, rows, :] + jnp.sum(p, axis=1, keepdims=True)
                p_ref[g, rows, :] = p.astype(BF16)
            acc_ref[g] = acc_ref[g] + _dot(p_ref[g], vbuf_ref[slot, :, g * gw:(g + 1) * gw])

    def running_max_body(c, slot):
        bias_of(c)
        for g in range(n_hg):
            s_ref[g] = _dot(qbd_ref[g], kbuf_ref[slot, g * gw:(g + 1) * gw, :])
        for g in range(n_hg):
            for r in range(hg * tq // DSA_SLAB):
                rows = slice(r * DSA_SLAB, (r + 1) * DSA_SLAB)
                b0 = (r * DSA_SLAB) % tq
                s = s_ref[g, rows, :] + bias_ref[b0:b0 + DSA_SLAB, :]
                m_prev = m_ref[g, rows, :]
                m_new = jnp.maximum(m_prev, jnp.max(s, axis=1, keepdims=True))
                alpha = jnp.exp2(m_prev - m_new)
                p = jnp.exp2(s - tile_l(m_new, ca))
                l_ref[g, rows, :] = alpha * l_ref[g, rows, :] + jnp.sum(p, axis=1, keepdims=True)
                m_ref[g, rows, :] = m_new
                alpha_ref[g, rows, :] = alpha
                p_ref[g, rows, :] = p.astype(BF16)
            acc_ref[g] = (acc_ref[g] * tile_l(alpha_ref[g], gw)
                          + _dot(p_ref[g], vbuf_ref[slot, :, g * gw:(g + 1) * gw]))

    @pl.when(bound_ok)
    def _():
        over_key_chunks(bounded_body)

    @pl.when(jnp.logical_not(bound_ok))
    def _():
        m_ref[...] = jnp.full(m_ref.shape, MASKED, F32)
        over_key_chunks(running_max_body)

    outs = []
    for g in range(n_hg):
        a = acc_ref[g] * tile_l(1.0 / l_ref[g], gw)
        og = jnp.zeros((tq, gw), F32)
        for h in range(hg):
            og = og + jnp.where(head_of_lane == h, a[h * tq:(h + 1) * tq], 0.0)
        outs.append(og)
    o_ref[...] = jnp.concatenate(outs, axis=1)


def _dsa(qi_r, kit, wi, qb, kt, v, topk):
    s = qb.shape[0]
    tq = DSA_TQ
    assert s % DSA_GROUP == 0 and DSA_GROUP % DSA_CH == 0
    n_hg = DSA_HEADS // DSA_HG
    rows = DSA_HG * tq
    gw = DSA_HG * DSA_HD
    return pl.pallas_call(
        functools.partial(_dsa_kernel, seq=s, topk=topk),
        grid=(s // tq,),
        in_specs=[pl.BlockSpec((1, IDX_HEADS * tq, IDX_DIM), lambda i: (i, 0, 0)),
                  _resident(kit.shape),
                  pl.BlockSpec((tq, LANES), lambda i: (i, 0)),
                  pl.BlockSpec((tq, DSA_W), lambda i: (i, 0)),
                  pl.BlockSpec(memory_space=pl.ANY),
                  pl.BlockSpec(memory_space=pl.ANY)],
        out_specs=pl.BlockSpec((tq, DSA_W), lambda i: (i, 0)),
        out_shape=jax.ShapeDtypeStruct((s, DSA_W), F32),
        scratch_shapes=[pltpu.VMEM((s // DSA_CH, tq, DSA_CH), I32),
                        pltpu.VMEM((s // DSA_GROUP, tq, LANES), I32),
                        pltpu.VMEM((s // DSA_GROUP, tq, LANES), I32),
                        pltpu.VMEM((IDX_HEADS, tq, LANES), F32),
                        pltpu.VMEM((n_hg, rows, gw), BF16),
                        pltpu.VMEM((n_hg, rows, DSA_CH_ATT), F32),
                        pltpu.VMEM((n_hg, rows, DSA_CH_ATT), BF16),
                        pltpu.VMEM((tq, DSA_CH_ATT), F32),
                        pltpu.VMEM((n_hg, rows, LANES), F32),
                        pltpu.VMEM((n_hg, rows, LANES), F32),
                        pltpu.VMEM((n_hg, rows, LANES), F32),
                        pltpu.VMEM((n_hg, rows, gw), F32),
                        pltpu.VMEM((n_hg, rows, LANES), F32),
                        pltpu.SMEM((DSA_HEADS,), F32),
                        pltpu.VMEM((2, DSA_W, DSA_CH_ATT), BF16),
                        pltpu.VMEM((2, DSA_CH_ATT, DSA_W), BF16),
                        pltpu.SemaphoreType.DMA((2, 2))],
        compiler_params=_params(("arbitrary",)),
        name="dsa",
    )(qi_r, kit, wi, qb, kt, v)


def _merge_kernel(x_ref, mod_ref, gpre_ref, gpost_ref, oa_ref, ob_ref, wbg_ref, wpg_ref, wpd_ref,
                  wout_ref, o_ref):
    x = x_ref[...]
    h = _modulated(x, gpre_ref[...], mod_ref, 0, 1)
    gates = _sigmoid(_dot(h.astype(BF16), wbg_ref[...]))
    yg = _dot(oa_ref[...].astype(BF16), wpg_ref[...])
    yd = _dot(ob_ref[...].astype(BF16), wpd_ref[...])
    mix = gates[:, 0:D_MODEL] * yg + gates[:, D_MODEL:] * yd
    out = _dot(mix.astype(BF16), wout_ref[...])
    o_ref[...] = x + mod_ref[2:3, :] * _rms(out, gpost_ref[...])


def _merge(x, mod, gpre, gpost, oa, ob, wbg, wpg, wpd, wout):
    s = x.shape[0]
    tm = PROJ_ROWS
    row = lambda n: pl.BlockSpec((tm, n), lambda i: (i, 0))
    return pl.pallas_call(
        _merge_kernel,
        grid=(s // tm,),
        in_specs=[row(D_MODEL), _full(mod.shape), _full(gpre.shape), _full(gpost.shape),
                  row(GLA_V), row(DSA_W), _full(wbg.shape), _full(wpg.shape), _full(wpd.shape),
                  _full(wout.shape)],
        out_specs=row(D_MODEL),
        out_shape=jax.ShapeDtypeStruct((s, D_MODEL), F32),
        compiler_params=_params(("arbitrary",)),
        name="merge",
    )(x, mod, gpre, gpost, oa, ob, wbg, wpg, wpd, wout)


def _router(lg):
    t = lg.shape[0]
    lane = lax.broadcasted_iota(I32, (t, LANES), 1)
    lanef = lane.astype(F32)
    big = 1e9
    gm = lane < N_GROUPS
    gmax = jnp.max(jnp.where(gm, lg, -jnp.inf), axis=1, keepdims=True)
    gsum = jnp.sum(jnp.where(gm, jnp.exp(lg - gmax), 0.0), axis=1, keepdims=True)
    p_g = 1.0 / gsum
    g_sel = jnp.min(jnp.where(gm & (lg == gmax), lanef, big), axis=1, keepdims=True)
    lo = N_GROUPS + EXPERTS_PER_GROUP * g_sel
    em = (lanef >= lo) & (lanef < lo + EXPERTS_PER_GROUP)
    m1 = jnp.max(jnp.where(em, lg, -jnp.inf), axis=1, keepdims=True)
    i1 = jnp.min(jnp.where(em & (lg == m1), lanef, big), axis=1, keepdims=True)
    em2 = em & (lanef != i1)
    m2 = jnp.max(jnp.where(em2, lg, -jnp.inf), axis=1, keepdims=True)
    i2 = jnp.min(jnp.where(em2 & (lg == m2), lanef, big), axis=1, keepdims=True)
    e2 = jnp.exp(m2 - m1)
    inv = 1.0 / (1.0 + e2)
    return (jnp.where(lanef == i1, p_g * inv, 0.0) + jnp.where(lanef == i2, p_g * (e2 * inv), 0.0))


def _moe_kernel(x_ref, mod_ref, gpre_ref, gpost_ref, wr_ref, br_ref, wg_ref, wu_ref, wd_ref,
                o_ref, hb_ref, comb_ref, acc_ref):
    j = pl.program_id(1)
    tm = x_ref.shape[0]

    @pl.when(j == 0)
    def _():
        h = _modulated(x_ref[...], gpre_ref[...], mod_ref, 3, 4)
        hb = h.astype(BF16)
        hb_ref[...] = hb
        comb_ref[...] = _router(_dot(hb, wr_ref[...]) + br_ref[...])
        acc_ref[...] = jnp.zeros_like(acc_ref)

    hb = hb_ref[...]
    hgate = _dot(hb, wg_ref[...])
    hup = _dot(hb, wu_ref[...])
    act = hgate * _sigmoid(hgate) * hup
    comb = comb_ref[...]
    lane = lax.broadcasted_iota(I32, (tm, LANES), 1)
    parts = []
    for e in range(MOE_EB):
        sel = lane == (N_GROUPS + j * MOE_EB + e)
        cw = jnp.sum(jnp.where(sel, comb, 0.0), axis=1, keepdims=True)
        parts.append((act[:, e * D_EXPERT:(e + 1) * D_EXPERT] * cw).astype(BF16))
    acc_ref[...] += _dot(jnp.concatenate(parts, axis=1), wd_ref[...])

    @pl.when(j == pl.num_programs(1) - 1)
    def _():
        o_ref[...] = x_ref[...] + mod_ref[5:6, :] * _rms(acc_ref[...], gpost_ref[...])


def _moe(x, mod, gpre, gpost, wr, br, wg, wu, wd):
    s = x.shape[0]
    tm = MOE_ROWS
    bw = MOE_EB * D_EXPERT
    return pl.pallas_call(
        _moe_kernel,
        grid=(s // tm, N_EXPERTS // MOE_EB),
        in_specs=[pl.BlockSpec((tm, D_MODEL), lambda i, j: (i, 0)),
                  _full(mod.shape), _full(gpre.shape), _full(gpost.shape), _full(wr.shape),
                  _full(br.shape),
                  pl.BlockSpec((D_MODEL, bw), lambda i, j: (0, j)),
                  pl.BlockSpec((D_MODEL, bw), lambda i, j: (0, j)),
                  pl.BlockSpec((bw, D_MODEL), lambda i, j: (j, 0))],
        out_specs=pl.BlockSpec((tm, D_MODEL), lambda i, j: (i, 0)),
        out_shape=jax.ShapeDtypeStruct((s, D_MODEL), F32),
        scratch_shapes=[pltpu.VMEM((tm, D_MODEL), BF16), pltpu.VMEM((tm, LANES), F32),
                        pltpu.VMEM((tm, D_MODEL), F32)],
        compiler_params=_params(("arbitrary", "arbitrary")),
        name="moe",
    )(x, mod, gpre, gpost, wr, br, wg, wu, wd)


def _cast_columns_kernel(x_ref, o_ref):
    f = x_ref.shape[2]
    for j in range(x_ref.shape[0]):
        o_ref[:, j * f:(j + 1) * f] = x_ref[j].astype(BF16)


def _cast_rows_kernel(x_ref, o_ref):
    r = x_ref.shape[1]
    for j in range(x_ref.shape[0]):
        o_ref[j * r:(j + 1) * r, :] = x_ref[j].astype(BF16)


def _experts_to_columns(w):
    e, d, f = w.shape
    eb = MOE_CAST_EB
    return pl.pallas_call(
        _cast_columns_kernel,
        grid=(e // eb,),
        in_specs=[pl.BlockSpec((eb, d, f), lambda i: (i, 0, 0))],
        out_specs=pl.BlockSpec((d, eb * f), lambda i: (0, i)),
        out_shape=jax.ShapeDtypeStruct((d, e * f), BF16),
        compiler_params=_params(("arbitrary",)),
        name="expert_cast_cols",
    )(w)


def _experts_to_rows(w):
    e, f, d = w.shape
    eb = MOE_CAST_EB
    return pl.pallas_call(
        _cast_rows_kernel,
        grid=(e // eb,),
        in_specs=[pl.BlockSpec((eb, f, d), lambda i: (i, 0, 0))],
        out_specs=pl.BlockSpec((eb * f, d), lambda i: (i, 0)),
        out_shape=jax.ShapeDtypeStruct((e * f, d), BF16),
        compiler_params=_params(("arbitrary",)),
        name="expert_cast_rows",
    )(w)


def _rope_lane_constants():
    rot = DSA_HD // ROT_FRAC
    half = rot // 2
    freqs = np.power(np.float32(ROPE_THETA), -np.arange(half, dtype=np.float32) * np.float32(2.0) / rot)
    j = np.arange(LANES) % DSA_HD
    fv = np.where(j < rot, freqs[j % half], 0.0).astype(np.float32)
    sg = np.where(j < half, -1.0, np.where(j < rot, 1.0, 0.0)).astype(np.float32)
    return jnp.asarray(fv)[None, :], jnp.asarray(sg)[None, :]


def _pad_cols(w, n):
    return jnp.pad(w, ((0, 0), (0, n - w.shape[1])))


def _layer(x, c, pos, w_ada, b_ada, g_pre_mix, g_post_mix, g_pre_ffn, g_post_ffn, w_in, w_gla_a2,
           b_gla_a, g_gla_out, g_idx_k, w_proj_gla, w_proj_dsa, w_out, w_router_g, b_router_g,
           w_router_e, b_router_e, w_e_gate, w_e_up, w_e_down):
    s = x.shape[0]
    mod = _ada(jnp.broadcast_to(c, (8, D_MODEL)), w_ada, b_ada[None, :])[0].reshape(N_MOD, D_MODEL)

    o = np.cumsum((GLA_QK, GLA_QK, GLA_V, GLA_V, GLA_GATE_RANK, DSA_W, DSA_W, DSA_W, IDX_Q, IDX_DIM,
                   IDX_HEADS, 2 * D_MODEL))
    wb = w_in.astype(BF16)
    w_gla = wb[:, 0:o[3]]
    w_ga = _pad_cols(wb[:, o[3]:o[4]], LANES)
    w_dsa = wb[:, o[4]:o[8]]
    w_sm = _pad_cols(jnp.concatenate([wb[:, o[8]:o[9]], wb[:, o[9]:o[10]]], axis=1), LANES)
    w_bg = wb[:, o[10]:o[11]]
    w_a2 = jnp.pad(w_gla_a2.astype(BF16), ((0, LANES - GLA_GATE_RANK), (0, 0)))

    q_a, k_a, v_a, gg, la = _gla_proj(x, mod, g_pre_mix[None, :], w_gla, w_ga, w_a2, b_gla_a[None, :])
    o_a = _gla(q_a, k_a, v_a, gg, la, jnp.tile(g_gla_out, GLA_HEADS)[None, :])

    fv, sg = _rope_lane_constants()
    gik = jnp.pad(g_idx_k, (0, LANES - IDX_DIM))[None, :]
    q_b, k_b, v_b, qi, ki, wi = _dsa_proj(x, mod, g_pre_mix[None, :], pos.astype(F32)[:, None], fv, sg,
                                          w_dsa, w_sm, gik)
    nqb = s // DSA_TQ
    qi_r = qi.reshape(nqb, DSA_TQ, IDX_HEADS, IDX_DIM).transpose(0, 2, 1, 3).reshape(
        nqb, IDX_HEADS * DSA_TQ, IDX_DIM)
    o_b = _dsa(qi_r, ki.T, wi, q_b, k_b.T, v_b, min(DSA_TOPK_MAX, s // 4))

    x1 = _merge(x, mod, g_pre_mix[None, :], g_post_mix[None, :], o_a, o_b, w_bg,
                w_proj_gla.astype(BF16), w_proj_dsa.astype(BF16), w_out.astype(BF16))

    wr = _pad_cols(jnp.concatenate([w_router_g, w_router_e], axis=1).astype(BF16), LANES)
    br = jnp.pad(jnp.concatenate([b_router_g, b_router_e]), (0, LANES - N_GROUPS - N_EXPERTS))[None, :]
    wg = _experts_to_columns(w_e_gate)
    wu = _experts_to_columns(w_e_up)
    wd = _experts_to_rows(w_e_down)
    return _moe(x1, mod, g_pre_ffn[None, :], g_post_ffn[None, :], wr, br, wg, wu, wd)


def kernel(x, c, positions, w_ada, b_ada, g_pre_mix, g_post_mix, g_pre_ffn, g_post_ffn, w_in, w_gla_a2,
           b_gla_a, g_gla_out, g_idx_k, w_proj_gla, w_proj_dsa, w_out, w_router_g, b_router_g,
           w_router_e, b_router_e, w_e_gate, w_e_up, w_e_down):
    batch, depth = x.shape[0], w_ada.shape[0]
    outs = []
    for bi in range(batch):
        xb = x[bi]
        for l in range(depth):
            xb = _layer(xb, c[bi:bi + 1], positions[bi], w_ada[l], b_ada[l], g_pre_mix[l], g_post_mix[l],
                        g_pre_ffn[l], g_post_ffn[l], w_in[l], w_gla_a2[l], b_gla_a[l], g_gla_out[l],
                        g_idx_k[l], w_proj_gla[l], w_proj_dsa[l], w_out[l], w_router_g[l],
                        b_router_g[l], w_router_e[l], b_router_e[l], w_e_gate[l], w_e_up[l],
                        w_e_down[l])
        outs.append(xb)
    return jnp.stack(outs, axis=0)
```

```python
import functools

import numpy as np
import jax
import jax.numpy as jnp
from jax import lax
from jax.experimental import pallas as pl
from jax.experimental.pallas import tpu as pltpu

F32 = jnp.float32
BF16 = jnp.bfloat16
I32 = jnp.int32

D_MODEL = 1024
EPS = 1e-6
ROPE_THETA = 500000.0
ROT_FRAC = 4
GLA_HEADS = 4
GLA_DK = 64
GLA_DV = 128
GLA_GATE_RANK = 16
GLA_TAU = 16.0
GLA_CHUNK = 64
DSA_HEADS = 8
DSA_HD = 64
IDX_HEADS = 8
IDX_DIM = 64
DSA_TOPK_MAX = 256
N_GROUPS = 4
EXPERTS_PER_GROUP = 8
N_EXPERTS = N_GROUPS * EXPERTS_PER_GROUP
D_EXPERT = D_MODEL // 4
N_MOD = 6

GLA_QK = GLA_HEADS * GLA_DK
GLA_V = GLA_HEADS * GLA_DV
DSA_W = DSA_HEADS * DSA_HD
IDX_Q = IDX_HEADS * IDX_DIM

LANES = 128
VMEM_LIMIT_BYTES = 56 * 1024 * 1024

PROJ_ROWS = 1024
GLA_ROWS = 512
DSA_TQ = 256
DSA_CH = 1024
DSA_CH_ATT = 512
MOE_ROWS = 512
MOE_EB = 8
MOE_CAST_EB = 4
MASKED = -1e30
LOG2E = float(np.log2(np.e))
GLA_LEVELS = 7


def _dot(a, b):
    return jnp.dot(a, b, preferred_element_type=F32)


def _dot_nt(a, b):
    return lax.dot_general(a, b, (((1,), (1,)), ((), ())), preferred_element_type=F32)


def _dot_tn(a, b):
    return lax.dot_general(a, b, (((0,), (0,)), ((), ())), preferred_element_type=F32)


def _sigmoid(x):
    return 1.0 / (1.0 + jnp.exp(-x))


def _rms(x, g):
    ms = jnp.mean(x * x, axis=-1, keepdims=True)
    return x * lax.rsqrt(ms + EPS) * g


def _modulated(x, g, mod_ref, shift_row, scale_row):
    return (_rms(x, g) * (1.0 + mod_ref[scale_row:scale_row + 1, :])
            + mod_ref[shift_row:shift_row + 1, :])


def _params(sem):
    return pltpu.CompilerParams(dimension_semantics=sem, vmem_limit_bytes=VMEM_LIMIT_BYTES)


def _full(shape):
    return pl.BlockSpec(shape, lambda *_: (0,) * len(shape))


def _resident(shape):
    return pl.BlockSpec(shape, lambda *_: (0,) * len(shape), pipeline_mode=pl.Buffered(1))


def _ada_kernel(c_ref, w_ref, b_ref, o_ref):
    c = c_ref[...]
    a = c * _sigmoid(c)
    o_ref[...] = _dot(a.astype(BF16), w_ref[...].astype(BF16)) + b_ref[...]


def _ada(c8, w, b):
    n = w.shape[1]
    bn = 1536
    return pl.pallas_call(
        _ada_kernel,
        grid=(n // bn,),
        in_specs=[_full(c8.shape),
                  pl.BlockSpec((D_MODEL, bn), lambda j: (0, j)),
                  pl.BlockSpec((1, bn), lambda j: (0, j))],
        out_specs=pl.BlockSpec((8, bn), lambda j: (0, j)),
        out_shape=jax.ShapeDtypeStruct((8, n), F32),
        compiler_params=_params(("arbitrary",)),
        name="ada",
    )(c8, w, b)


def _gla_proj_kernel(x_ref, mod_ref, g_ref, w_ref, wga_ref, wa2_ref, ba_ref,
                     q_ref, k_ref, v_ref, gg_ref, la_ref):
    h = _modulated(x_ref[...], g_ref[...], mod_ref, 0, 1)
    hb = h.astype(BF16)
    p = _dot(hb, w_ref[...])
    q_ref[...] = p[:, 0:GLA_QK] * (GLA_DK ** -0.5)
    k_ref[...] = p[:, GLA_QK:2 * GLA_QK]
    v_ref[...] = p[:, 2 * GLA_QK:2 * GLA_QK + GLA_V]
    gg_ref[...] = p[:, 2 * GLA_QK + GLA_V:]
    ga = _dot(hb, wga_ref[...])
    z = _dot(ga.astype(BF16), wa2_ref[...]) + ba_ref[...]
    log_sig = jnp.minimum(z, 0.0) - jnp.log1p(jnp.exp(-jnp.abs(z)))
    la_ref[...] = log_sig * (1.0 / GLA_TAU)


def _gla_proj(x, mod, g, w, wga, wa2, ba):
    s = x.shape[0]
    tm = PROJ_ROWS
    row = lambda n: pl.BlockSpec((tm, n), lambda i: (i, 0))
    return pl.pallas_call(
        _gla_proj_kernel,
        grid=(s // tm,),
        in_specs=[row(D_MODEL), _full(mod.shape), _full(g.shape), _full(w.shape),
                  _full(wga.shape), _full(wa2.shape), _full(ba.shape)],
        out_specs=[row(GLA_QK), row(GLA_QK), row(GLA_V), row(GLA_V), row(GLA_QK)],
        out_shape=[jax.ShapeDtypeStruct((s, GLA_QK), F32), jax.ShapeDtypeStruct((s, GLA_QK), F32),
                   jax.ShapeDtypeStruct((s, GLA_V), F32), jax.ShapeDtypeStruct((s, GLA_V), F32),
                   jax.ShapeDtypeStruct((s, GLA_QK), F32)],
        compiler_params=_params(("arbitrary",)),
        name="gla_proj",
    )(x, mod, g, w, wga, wa2, ba)


def _dsa_proj_kernel(x_ref, mod_ref, g_ref, pos_ref, fv_ref, sg_ref, w_ref, wsm_ref, gik_ref,
                     qb_ref, kb_ref, vb_ref, qi_ref, ki_ref, wi_ref):
    tm = x_ref.shape[0]
    h = _modulated(x_ref[...], g_ref[...], mod_ref, 0, 1)
    hb = h.astype(BF16)
    ang = pos_ref[...] * fv_ref[...]
    cs = jnp.cos(ang)
    sn = jnp.sin(ang) * sg_ref[...]
    lane = lax.broadcasted_iota(I32, (tm, LANES), 1)
    first = (lane & (DSA_HD - 1)) < (DSA_HD // ROT_FRAC // 2)

    def rope(t):
        width = t.shape[1]
        rep = width // LANES
        tile = (lambda a: jnp.concatenate([a] * rep, axis=1)) if rep > 1 else (lambda a: a)
        half = DSA_HD // ROT_FRAC // 2
        fwd = pltpu.roll(t, half, 1)
        bwd = pltpu.roll(t, width - half, 1)
        partner = jnp.where(tile(first), bwd, fwd)
        return t * tile(cs) + partner * tile(sn)

    p = _dot(hb, w_ref[...])
    qb_ref[...] = (rope(p[:, 0:DSA_W]) * (DSA_HD ** -0.5 * LOG2E)).astype(BF16)
    kb_ref[...] = rope(p[:, DSA_W:2 * DSA_W]).astype(BF16)
    vb_ref[...] = p[:, 2 * DSA_W:3 * DSA_W].astype(BF16)
    qi_ref[...] = (rope(p[:, 3 * DSA_W:]) * (IDX_DIM ** -0.5)).astype(BF16)

    sm = _dot(hb, wsm_ref[...])
    is_ik = lane < IDX_DIM
    mu = jnp.sum(jnp.where(is_ik, sm, 0.0), axis=-1, keepdims=True) * (1.0 / IDX_DIM)
    xc = jnp.where(is_ik, sm - mu, 0.0)
    var = jnp.sum(xc * xc, axis=-1, keepdims=True) * (1.0 / IDX_DIM)
    y = xc * lax.rsqrt(var + EPS) * gik_ref[...]
    ki_ref[...] = rope(y)[:, 0:IDX_DIM].astype(BF16)
    wi_ref[...] = sm * (IDX_HEADS ** -0.5)


def _dsa_proj(x, mod, g, pos, fv, sg, w, wsm, gik):
    s = x.shape[0]
    tm = PROJ_ROWS
    row = lambda n: pl.BlockSpec((tm, n), lambda i: (i, 0))
    return pl.pallas_call(
        _dsa_proj_kernel,
        grid=(s // tm,),
        in_specs=[row(D_MODEL), _full(mod.shape), _full(g.shape), row(1), _full(fv.shape),
                  _full(sg.shape), _full(w.shape), _full(wsm.shape), _full(gik.shape)],
        out_specs=[row(DSA_W), row(DSA_W), row(DSA_W), row(IDX_Q), row(IDX_DIM), row(LANES)],
        out_shape=[jax.ShapeDtypeStruct((s, DSA_W), BF16), jax.ShapeDtypeStruct((s, DSA_W), BF16),
                   jax.ShapeDtypeStruct((s, DSA_W), BF16), jax.ShapeDtypeStruct((s, IDX_Q), BF16),
                   jax.ShapeDtypeStruct((s, IDX_DIM), BF16), jax.ShapeDtypeStruct((s, LANES), F32)],
        compiler_params=_params(("arbitrary",)),
        name="dsa_proj",
    )(x, mod, g, pos, fv, sg, w, wsm, gik)


def _gla_constants():
    c = GLA_CHUNK
    tril = np.tril(np.ones((c, c), np.float32))
    t = np.arange(c)
    mats = [tril]
    masks = [(t[:, None] == t[None, :])]
    for hs in (32, 16, 8, 4, 2, 1):
        blk = 2 * hs
        r = (t // blk) * blk + hs - 1
        mats.append(tril - tril[r, :])
        same = (t[:, None] // blk) == (t[None, :] // blk)
        masks.append(same & ((t[:, None] % blk) >= hs) & ((t[None, :] % blk) < hs))
    m_all = np.concatenate(mats, axis=0)
    lvl = np.stack([np.tile(m.astype(np.float32), (1, GLA_HEADS)) for m in masks])
    hrow = np.arange(GLA_HEADS * c) // c
    wmask = (hrow[:, None] == (np.arange(GLA_QK) // GLA_DK)[None, :]).astype(np.float32)
    vmask = (hrow[:, None] == (np.arange(GLA_V) // GLA_DV)[None, :]).astype(np.float32)
    smask = ((np.arange(GLA_V) // GLA_DV)[:, None] == (np.arange(GLA_QK) // GLA_DK)[None, :])
    return (jnp.asarray(m_all, BF16), jnp.asarray(lvl), jnp.asarray(wmask), jnp.asarray(vmask),
            jnp.asarray(smask.astype(np.float32)))


def _gla_kernel(q_ref, k_ref, v_ref, gg_ref, la_ref, gout_ref, mall_ref, lvl_ref, wmask_ref,
                vmask_ref, smask_ref, o_ref, st_ref):
    c = GLA_CHUNK

    @pl.when(pl.program_id(0) == 0)
    def _():
        st_ref[...] = jnp.zeros_like(st_ref)

    m_all = mall_ref[...]
    wmask = wmask_ref[...]
    vmask = vmask_ref[...]
    smask = smask_ref[...]
    gout = gout_ref[...]

    def chunk(ci, carry):
        r0 = pl.multiple_of(ci * c, c)
        rows = pl.ds(r0, c)
        q = q_ref[rows, :]
        k = k_ref[rows, :]
        v = v_ref[rows, :]
        la = la_ref[rows, :]
        hi = la.astype(BF16)
        r1 = la - hi.astype(F32)
        mid = r1.astype(BF16)
        lo = (r1 - mid.astype(F32)).astype(BF16)
        dall = _dot(m_all, hi) + _dot(m_all, mid) + _dot(m_all, lo)
        b = dall[0:c]
        b_last = b[c - 1:c, :]
        qhat = q * jnp.exp(b)
        khat = k * jnp.exp(b_last - b)

        a = jnp.zeros((c, GLA_HEADS * c), F32)
        for lv in range(GLA_LEVELS):
            if lv == 0:
                qt, kt = q, k
            else:
                d = dall[lv * c:(lv + 1) * c]
                qt = q * jnp.exp(jnp.minimum(d, 0.0))
                kt = k * jnp.exp(jnp.minimum(-d, 0.0))
            w = (jnp.concatenate([kt] * GLA_HEADS, axis=0) * wmask).astype(BF16)
            a = a + lvl_ref[lv] * _dot_nt(qt.astype(BF16), w)

        st = st_ref[...]
        vbd = (jnp.concatenate([v] * GLA_HEADS, axis=0) * vmask).astype(BF16)
        o = _dot(a.astype(BF16), vbd) + _dot_nt(qhat.astype(BF16), st.astype(BF16))
        st_ref[...] = st * jnp.exp(b_last) + smask * _dot_tn(v.astype(BF16), khat.astype(BF16))

        parts = []
        for hh in range(GLA_HEADS):
            oh = o[:, hh * GLA_DV:(hh + 1) * GLA_DV]
            ms = jnp.mean(oh * oh, axis=-1, keepdims=True)
            parts.append(oh * lax.rsqrt(ms + EPS))
        gg = gg_ref[rows, :]
        o_ref[rows, :] = jnp.concatenate(parts, axis=1) * gout * (gg * _sigmoid(gg))
        return carry

    lax.fori_loop(0, q_ref.shape[0] // c, chunk, 0, unroll=4)


def _gla(q, k, v, gg, la, gout):
    s = q.shape[0]
    tb = GLA_ROWS
    consts = _gla_constants()
    row = lambda n: pl.BlockSpec((tb, n), lambda i: (i, 0))
    return pl.pallas_call(
        _gla_kernel,
        grid=(s // tb,),
        in_specs=[row(GLA_QK), row(GLA_QK), row(GLA_V), row(GLA_V), row(GLA_QK), _full(gout.shape)]
                 + [_full(a.shape) for a in consts],
        out_specs=row(GLA_V),
        out_shape=jax.ShapeDtypeStruct((s, GLA_V), F32),
        scratch_shapes=[pltpu.VMEM((GLA_V, GLA_QK), F32)],
        compiler_params=_params(("arbitrary",)),
        name="gla",
    )(q, k, v, gg, la, gout, *consts)


DSA_SLAB = 32
DSA_HG = 2
DSA_MAX_SHIFT = 48.0
WORD = 32
DSA_GROUP = WORD * LANES
LANE_BITS = LANES.bit_length() - 1
HD_BITS = DSA_HD.bit_length() - 1


def _float_to_ordered_bits(x):
    bits = lax.bitcast_convert_type(x, I32)
    key = bits ^ ((bits >> 31) & jnp.int32(0x7FFFFFFF))
    return key ^ jnp.int32(-2147483648)


def _transpose_bits(a):
    a = list(a)
    j, msk = 16, 0x0000FFFF
    while j:
        m32 = jnp.int32(np.array(msk, np.uint32).view(np.int32))
        k = 0
        while k < WORD:
            t = (a[k] ^ (a[k + j] >> j)) & m32
            a[k] = a[k] ^ t
            a[k + j] = a[k + j] ^ (t << j)
            k = (k + j + 1) & ~j
        j >>= 1
        msk = (msk ^ (msk << j)) & 0xFFFFFFFF
    return a


def _dsa_kernel(qi_ref, kit_ref, wi_ref, qb_ref, kt_hbm, v_hbm, o_ref,
                key_ref, alive_ref, great_ref, wb_ref, qbd_ref, s_ref, p_ref, bias_ref, m_ref, l_ref,
                alpha_ref, acc_ref, bnd_ref, kmax_ref, kbuf_ref, vbuf_ref, sem_ref, *, seq, topk):
    tq = DSA_TQ
    ch = DSA_CH
    hg = DSA_HG
    n_hg = DSA_HEADS // hg
    gw = hg * DSA_HD
    bpc = ch // LANES
    ca = DSA_CH_ATT
    bpa = ca // LANES
    cga = DSA_GROUP // ca
    n_ca = ((pl.program_id(0) + 1) * tq + ca - 1) // ca
    cpg = DSA_GROUP // ch
    n_groups = seq // DSA_GROUP
    i = pl.program_id(0)
    n_ch = ((i + 1) * tq + ch - 1) // ch
    lane_i = lax.broadcasted_iota(I32, (tq, LANES), 1)
    row_pos = lax.broadcasted_iota(I32, (tq, 1), 0) + i * tq
    kf = float(topk)

    def tile_l(a, width):
        return jnp.concatenate([a] * (width // a.shape[1]), axis=1)

    def chunk(c):
        return pl.ds(pl.multiple_of(c * ch, ch), ch)

    def key_copy(c, slot):
        cols = pl.ds(pl.multiple_of(c * ca, ca), ca)
        return pltpu.make_async_copy(kt_hbm.at[:, cols], kbuf_ref.at[slot], sem_ref.at[0, slot])

    def value_copy(c, slot):
        cols = pl.ds(pl.multiple_of(c * ca, ca), ca)
        return pltpu.make_async_copy(v_hbm.at[cols, :], vbuf_ref.at[slot], sem_ref.at[1, slot])

    def kv_copies(c, slot):
        return key_copy(c, slot), value_copy(c, slot)

    def lane_sum(a):
        return jnp.sum(a.astype(F32), axis=1, keepdims=True)

    @pl.when(i == 0)
    def _():
        key_ref[...] = jnp.zeros(key_ref.shape, I32)

        def norm_body(c, best):
            copy = key_copy(c, 0)
            copy.start()
            copy.wait()
            out = []
            for h in range(DSA_HEADS):
                kf32 = kbuf_ref[0, h * DSA_HD:(h + 1) * DSA_HD, :].astype(F32)
                out.append(jnp.maximum(best[h], jnp.sum(kf32 * kf32, axis=0, keepdims=True)))
            return tuple(out)

        best = lax.fori_loop(0, seq // ca, norm_body,
                             tuple(jnp.zeros((1, ca), F32) for _ in range(DSA_HEADS)))
        for h in range(DSA_HEADS):
            kmax_ref[h] = jnp.sqrt(jnp.max(best[h]))

    wv = wi_ref[...]
    for h in range(IDX_HEADS):
        wb_ref[h] = jnp.broadcast_to(wv[:, IDX_DIM + h:IDX_DIM + h + 1], (tq, LANES))

    def score_body(c, carry):
        lg = _dot(qi_ref[0], kit_ref[:, chunk(c)])
        sc = jnp.maximum(lg[0:tq], 0.0) * tile_l(wb_ref[0], ch)
        for h in range(1, IDX_HEADS):
            sc = sc + jnp.maximum(lg[h * tq:(h + 1) * tq], 0.0) * tile_l(wb_ref[h], ch)
        sc = jnp.where(sc == 0.0, 0.0, sc)
        key_ref[c] = _float_to_ordered_bits(sc)
        return carry

    lax.fori_loop(0, n_ch, score_body, 0)

    n_pg = (n_ch + cpg - 1) // cpg

    def plane_body(t, carry):
        g = t // (tq // 8)
        rows = pl.ds(pl.multiple_of((t % (tq // 8)) * 8, 8), 8)

        def slot(b):
            return (g * cpg + b // bpc, rows, slice((b % bpc) * LANES, (b % bpc + 1) * LANES))

        planes = _transpose_bits([key_ref[slot(WORD - 1 - k)] for k in range(WORD)])
        for b in range(WORD):
            key_ref[slot(b)] = planes[b]
        return carry

    lax.fori_loop(0, n_pg * (tq // 8), plane_body, 0)

    def index_mask(g, bound):
        r = bound - g * DSA_GROUP - lane_i
        q = jnp.clip((r + (LANES - 1)) >> LANE_BITS, 0, WORD)
        return jnp.where(q >= WORD, jnp.int32(-1), (jnp.int32(1) << jnp.minimum(q, WORD - 1)) - 1)

    for g in range(n_groups):
        alive_ref[g] = index_mask(g, row_pos + 1)
        great_ref[g] = jnp.zeros((tq, LANES), I32)

    def select_bits(ng):
        def bit_body(p, cnt_great):
            pc = p // bpc
            pl0 = pl.multiple_of((p % bpc) * LANES, LANES)
            ones = []
            acc = jnp.zeros((tq, LANES), I32)
            for g in range(ng):
                x = alive_ref[g] & key_ref[g * cpg + pc, :, pl.ds(pl0, LANES)]
                acc = acc + lax.population_count(x)
                ones.append(x)
            cnt_one = lane_sum(acc)
            take = (cnt_great + cnt_one) >= kf
            take_b = jnp.broadcast_to(take, (tq, LANES))
            for g in range(ng):
                a = alive_ref[g]
                alive_ref[g] = jnp.where(take_b, ones[g], a ^ ones[g])
                great_ref[g] = jnp.where(take_b, great_ref[g], great_ref[g] | ones[g])
            return jnp.where(take, cnt_great, cnt_great + cnt_one)

        return lambda: lax.fori_loop(0, WORD, bit_body, jnp.zeros((tq, 1), F32))

    cnt_great = lax.switch(n_pg - 1, [select_bits(ng) for ng in range(1, n_groups + 1)])
    need = kf - cnt_great

    def count_alive(bound):
        acc = jnp.zeros((tq, LANES), I32)
        for g in range(n_groups):
            acc = acc + lax.population_count(alive_ref[g] & index_mask(g, bound))
        return lane_sum(acc)

    tie_row = count_alive(jnp.full((tq, 1), seq, I32)) > need

    def tie_break(_):
        nbits = max(1, int(np.ceil(np.log2(seq))))

        def jbit(p, m):
            cand = m | jnp.left_shift(jnp.int32(1), nbits - 1 - p)
            return jnp.where(count_alive(cand) < need, cand, m)

        m = lax.fori_loop(0, nbits, jbit, jnp.zeros((tq, 1), I32))
        return jnp.where(tie_row, m + 1, jnp.int32(seq))

    any_tie = jnp.max(jnp.where(tie_row, 1.0, 0.0)) > 0.0
    bound = lax.cond(any_tie, tie_break, lambda _: jnp.full((tq, 1), seq, I32), 0)
    for g in range(n_groups):
        great_ref[g] = great_ref[g] | (alive_ref[g] & index_mask(g, bound))

    qt = qb_ref[...]
    head_of_lane = lax.broadcasted_iota(I32, (tq, gw), 1) >> HD_BITS
    for g in range(n_hg):
        qg = qt[:, g * gw:(g + 1) * gw]
        for h in range(hg):
            qbd_ref[g, h * tq:(h + 1) * tq, :] = jnp.where(head_of_lane == h, qg, jnp.zeros_like(qg))
    l_ref[...] = jnp.zeros(l_ref.shape, F32)
    acc_ref[...] = jnp.zeros(acc_ref.shape, F32)

    for g in range(n_hg):
        qf = qbd_ref[g].astype(F32)
        qn = jnp.sqrt(jnp.sum(qf * qf, axis=1, keepdims=True))
        kn = jnp.concatenate([jnp.full((tq, 1), kmax_ref[g * hg + h], F32) for h in range(hg)], axis=0)
        bnd_ref[g] = jnp.broadcast_to(qn * kn, (hg * tq, LANES))
    bound_ok = jnp.max(bnd_ref[...]) < DSA_MAX_SHIFT

    def bias_of(c):
        sel = great_ref[c // cga]
        for jj in range(bpa):
            bit = (sel >> ((c % cga) * bpa + jj)) & 1
            bias_ref[:, jj * LANES:(jj + 1) * LANES] = jnp.where(bit != 0, 0.0, MASKED)

    def over_key_chunks(body):
        for cp in kv_copies(0, 0):
            cp.start()

        def trip(c, carry):
            slot = c & 1
            for cp in kv_copies(c, slot):
                cp.wait()

            @pl.when(c + 1 < n_ca)
            def _():
                for cp in kv_copies(c + 1, 1 - slot):
                    cp.start()

            body(c, slot)
            return carry

        lax.fori_loop(0, n_ca, trip, 0)

    def bounded_body(c, slot):
        bias_of(c)
        for g in range(n_hg):
            s_ref[g] = _dot(qbd_ref[g], kbuf_ref[slot, g * gw:(g + 1) * gw, :])
        for g in range(n_hg):
            for r in range(hg * tq // DSA_SLAB):
                rows = slice(r * DSA_SLAB, (r + 1) * DSA_SLAB)
                b0 = (r * DSA_SLAB) % tq
                s = s_ref[g, rows, :] + bias_ref[b0:b0 + DSA_SLAB, :]
                p = jnp.exp2(s - tile_l(bnd_ref[g, rows, :], ca))
                l_ref[g, rows, :] = l_ref[g, rows, :] + jnp.sum(p, axis=1, keepdims=True)
                p_ref[g, rows, :] = p.astype(BF16)
            acc_ref[g] = acc_ref[g] + _dot(p_ref[g], vbuf_ref[slot, :, g * gw:(g + 1) * gw])

    def running_max_body(c, slot):
        bias_of(c)
        for g in range(n_hg):
            s_ref[g] = _dot(qbd_ref[g], kbuf_ref[slot, g * gw:(g + 1) * gw, :])
        for g in range(n_hg):
            for r in range(hg * tq // DSA_SLAB):
                rows = slice(r * DSA_SLAB, (r + 1) * DSA_SLAB)
                b0 = (r * DSA_SLAB) % tq
                s = s_ref[g, rows, :] + bias_ref[b0:b0 + DSA_SLAB, :]
                m_prev = m_ref[g, rows, :]
                m_new = jnp.maximum(m_prev, jnp.max(s, axis=1, keepdims=True))
                alpha = jnp.exp2(m_prev - m_new)
                p = jnp.exp2(s - tile_l(m_new, ca))
                l_ref[g, rows, :] = alpha * l_ref[g, rows, :] + jnp.sum(p, axis=1, keepdims=True)
                m_ref[g, rows, :] = m_new
                alpha_ref[g, rows, :] = alpha
                p_ref[g, rows, :] = p.astype(BF16)
            acc_ref[g] = (acc_ref[g] * tile_l(alpha_ref[g], gw)
                          + _dot(p_ref[g], vbuf_ref[slot, :, g * gw:(g + 1) * gw]))

    @pl.when(bound_ok)
    def _():
        over_key_chunks(bounded_body)

    @pl.when(jnp.logical_not(bound_ok))
    def _():
        m_ref[...] = jnp.full(m_ref.shape, MASKED, F32)
        over_key_chunks(running_max_body)

    outs = []
    for g in range(n_hg):
        a = acc_ref[g] * tile_l(1.0 / l_ref[g], gw)
        og = jnp.zeros((tq, gw), F32)
        for h in range(hg):
            og = og + jnp.where(head_of_lane == h, a[h * tq:(h + 1) * tq], 0.0)
        outs.append(og)
    o_ref[...] = jnp.concatenate(outs, axis=1)


def _dsa(qi_r, kit, wi, qb, kt, v, topk):
    s = qb.shape[0]
    tq = DSA_TQ
    assert s % DSA_GROUP == 0 and DSA_GROUP % DSA_CH == 0
    n_hg = DSA_HEADS // DSA_HG
    rows = DSA_HG * tq
    gw = DSA_HG * DSA_HD
    return pl.pallas_call(
        functools.partial(_dsa_kernel, seq=s, topk=topk),
        grid=(s // tq,),
        in_specs=[pl.BlockSpec((1, IDX_HEADS * tq, IDX_DIM), lambda i: (i, 0, 0)),
                  _resident(kit.shape),
                  pl.BlockSpec((tq, LANES), lambda i: (i, 0)),
                  pl.BlockSpec((tq, DSA_W), lambda i: (i, 0)),
                  pl.BlockSpec(memory_space=pl.ANY),
                  pl.BlockSpec(memory_space=pl.ANY)],
        out_specs=pl.BlockSpec((tq, DSA_W), lambda i: (i, 0)),
        out_shape=jax.ShapeDtypeStruct((s, DSA_W), F32),
        scratch_shapes=[pltpu.VMEM((s // DSA_CH, tq, DSA_CH), I32),
                        pltpu.VMEM((s // DSA_GROUP, tq, LANES), I32),
                        pltpu.VMEM((s // DSA_GROUP, tq, LANES), I32),
                        pltpu.VMEM((IDX_HEADS, tq, LANES), F32),
                        pltpu.VMEM((n_hg, rows, gw), BF16),
                        pltpu.VMEM((n_hg, rows, DSA_CH_ATT), F32),
                        pltpu.VMEM((n_hg, rows, DSA_CH_ATT), BF16),
                        pltpu.VMEM((tq, DSA_CH_ATT), F32),
                        pltpu.VMEM((n_hg, rows, LANES), F32),
                        pltpu.VMEM((n_hg, rows, LANES), F32),
                        pltpu.VMEM((n_hg, rows, LANES), F32),
                        pltpu.VMEM((n_hg, rows, gw), F32),
                        pltpu.VMEM((n_hg, rows, LANES), F32),
                        pltpu.SMEM((DSA_HEADS,), F32),
                        pltpu.VMEM((2, DSA_W, DSA_CH_ATT), BF16),
                        pltpu.VMEM((2, DSA_CH_ATT, DSA_W), BF16),
                        pltpu.SemaphoreType.DMA((2, 2))],
        compiler_params=_params(("arbitrary",)),
        name="dsa",
    )(qi_r, kit, wi, qb, kt, v)


def _merge_kernel(x_ref, mod_ref, gpre_ref, gpost_ref, oa_ref, ob_ref, wbg_ref, wpg_ref, wpd_ref,
                  wout_ref, o_ref):
    x = x_ref[...]
    h = _modulated(x, gpre_ref[...], mod_ref, 0, 1)
    gates = _sigmoid(_dot(h.astype(BF16), wbg_ref[...]))
    yg = _dot(oa_ref[...].astype(BF16), wpg_ref[...])
    yd = _dot(ob_ref[...].astype(BF16), wpd_ref[...])
    mix = gates[:, 0:D_MODEL] * yg + gates[:, D_MODEL:] * yd
    out = _dot(mix.astype(BF16), wout_ref[...])
    o_ref[...] = x + mod_ref[2:3, :] * _rms(out, gpost_ref[...])


def _merge(x, mod, gpre, gpost, oa, ob, wbg, wpg, wpd, wout):
    s = x.shape[0]
    tm = PROJ_ROWS
    row = lambda n: pl.BlockSpec((tm, n), lambda i: (i, 0))
    return pl.pallas_call(
        _merge_kernel,
        grid=(s // tm,),
        in_specs=[row(D_MODEL), _full(mod.shape), _full(gpre.shape), _full(gpost.shape),
                  row(GLA_V), row(DSA_W), _full(wbg.shape), _full(wpg.shape), _full(wpd.shape),
                  _full(wout.shape)],
        out_specs=row(D_MODEL),
        out_shape=jax.ShapeDtypeStruct((s, D_MODEL), F32),
        compiler_params=_params(("arbitrary",)),
        name="merge",
    )(x, mod, gpre, gpost, oa, ob, wbg, wpg, wpd, wout)


def _router(lg):
    t = lg.shape[0]
    lane = lax.broadcasted_iota(I32, (t, LANES), 1)
    lanef = lane.astype(F32)
    big = 1e9
    gm = lane < N_GROUPS
    gmax = jnp.max(jnp.where(gm, lg, -jnp.inf), axis=1, keepdims=True)
    gsum = jnp.sum(jnp.where(gm, jnp.exp(lg - gmax), 0.0), axis=1, keepdims=True)
    p_g = 1.0 / gsum
    g_sel = jnp.min(jnp.where(gm & (lg == gmax), lanef, big), axis=1, keepdims=True)
    lo = N_GROUPS + EXPERTS_PER_GROUP * g_sel
    em = (lanef >= lo) & (lanef < lo + EXPERTS_PER_GROUP)
    m1 = jnp.max(jnp.where(em, lg, -jnp.inf), axis=1, keepdims=True)
    i1 = jnp.min(jnp.where(em & (lg == m1), lanef, big), axis=1, keepdims=True)
    em2 = em & (lanef != i1)
    m2 = jnp.max(jnp.where(em2, lg, -jnp.inf), axis=1, keepdims=True)
    i2 = jnp.min(jnp.where(em2 & (lg == m2), lanef, big), axis=1, keepdims=True)
    e2 = jnp.exp(m2 - m1)
    inv = 1.0 / (1.0 + e2)
    return (jnp.where(lanef == i1, p_g * inv, 0.0) + jnp.where(lanef == i2, p_g * (e2 * inv), 0.0))


def _moe_kernel(x_ref, mod_ref, gpre_ref, gpost_ref, wr_ref, br_ref, wg_ref, wu_ref, wd_ref,
                o_ref, hb_ref, comb_ref, acc_ref):
    j = pl.program_id(1)
    tm = x_ref.shape[0]

    @pl.when(j == 0)
    def _():
        h = _modulated(x_ref[...], gpre_ref[...], mod_ref, 3, 4)
        hb = h.astype(BF16)
        hb_ref[...] = hb
        comb_ref[...] = _router(_dot(hb, wr_ref[...]) + br_ref[...])
        acc_ref[...] = jnp.zeros_like(acc_ref)

    hb = hb_ref[...]
    hgate = _dot(hb, wg_ref[...])
    hup = _dot(hb, wu_ref[...])
    act = hgate * _sigmoid(hgate) * hup
    comb = comb_ref[...]
    lane = lax.broadcasted_iota(I32, (tm, LANES), 1)
    parts = []
    for e in range(MOE_EB):
        sel = lane == (N_GROUPS + j * MOE_EB + e)
        cw = jnp.sum(jnp.where(sel, comb, 0.0), axis=1, keepdims=True)
        parts.append((act[:, e * D_EXPERT:(e + 1) * D_EXPERT] * cw).astype(BF16))
    acc_ref[...] += _dot(jnp.concatenate(parts, axis=1), wd_ref[...])

    @pl.when(j == pl.num_programs(1) - 1)
    def _():
        o_ref[...] = x_ref[...] + mod_ref[5:6, :] * _rms(acc_ref[...], gpost_ref[...])


def _moe(x, mod, gpre, gpost, wr, br, wg, wu, wd):
    s = x.shape[0]
    tm = MOE_ROWS
    bw = MOE_EB * D_EXPERT
    return pl.pallas_call(
        _moe_kernel,
        grid=(s // tm, N_EXPERTS // MOE_EB),
        in_specs=[pl.BlockSpec((tm, D_MODEL), lambda i, j: (i, 0)),
                  _full(mod.shape), _full(gpre.shape), _full(gpost.shape), _full(wr.shape),
                  _full(br.shape),
                  pl.BlockSpec((D_MODEL, bw), lambda i, j: (0, j)),
                  pl.BlockSpec((D_MODEL, bw), lambda i, j: (0, j)),
                  pl.BlockSpec((bw, D_MODEL), lambda i, j: (j, 0))],
        out_specs=pl.BlockSpec((tm, D_MODEL), lambda i, j: (i, 0)),
        out_shape=jax.ShapeDtypeStruct((s, D_MODEL), F32),
        scratch_shapes=[pltpu.VMEM((tm, D_MODEL), BF16), pltpu.VMEM((tm, LANES), F32),
                        pltpu.VMEM((tm, D_MODEL), F32)],
        compiler_params=_params(("arbitrary", "arbitrary")),
        name="moe",
    )(x, mod, gpre, gpost, wr, br, wg, wu, wd)


def _cast_columns_kernel(x_ref, o_ref):
    f = x_ref.shape[2]
    for j in range(x_ref.shape[0]):
        o_ref[:, j * f:(j + 1) * f] = x_ref[j].astype(BF16)


def _cast_rows_kernel(x_ref, o_ref):
    r = x_ref.shape[1]
    for j in range(x_ref.shape[0]):
        o_ref[j * r:(j + 1) * r, :] = x_ref[j].astype(BF16)


def _experts_to_columns(w):
    e, d, f = w.shape
    eb = MOE_CAST_EB
    return pl.pallas_call(
        _cast_columns_kernel,
        grid=(e // eb,),
        in_specs=[pl.BlockSpec((eb, d, f), lambda i: (i, 0, 0))],
        out_specs=pl.BlockSpec((d, eb * f), lambda i: (0, i)),
        out_shape=jax.ShapeDtypeStruct((d, e * f), BF16),
        compiler_params=_params(("arbitrary",)),
        name="expert_cast_cols",
    )(w)


def _experts_to_rows(w):
    e, f, d = w.shape
    eb = MOE_CAST_EB
    return pl.pallas_call(
        _cast_rows_kernel,
        grid=(e // eb,),
        in_specs=[pl.BlockSpec((eb, f, d), lambda i: (i, 0, 0))],
        out_specs=pl.BlockSpec((eb * f, d), lambda i: (i, 0)),
        out_shape=jax.ShapeDtypeStruct((e * f, d), BF16),
        compiler_params=_params(("arbitrary",)),
        name="expert_cast_rows",
    )(w)


def _rope_lane_constants():
    rot = DSA_HD // ROT_FRAC
    half = rot // 2
    freqs = np.power(np.float32(ROPE_THETA), -np.arange(half, dtype=np.float32) * np.float32(2.0) / rot)
    j = np.arange(LANES) % DSA_HD
    fv = np.where(j < rot, freqs[j % half], 0.0).astype(np.float32)
    sg = np.where(j < half, -1.0, np.where(j < rot, 1.0, 0.0)).astype(np.float32)
    return jnp.asarray(fv)[None, :], jnp.asarray(sg)[None, :]


def _pad_cols(w, n):
    return jnp.pad(w, ((0, 0), (0, n - w.shape[1])))


def _layer(x, c, pos, w_ada, b_ada, g_pre_mix, g_post_mix, g_pre_ffn, g_post_ffn, w_in, w_gla_a2,
           b_gla_a, g_gla_out, g_idx_k, w_proj_gla, w_proj_dsa, w_out, w_router_g, b_router_g,
           w_router_e, b_router_e, w_e_gate, w_e_up, w_e_down):
    s = x.shape[0]
    mod = _ada(jnp.broadcast_to(c, (8, D_MODEL)), w_ada, b_ada[None, :])[0].reshape(N_MOD, D_MODEL)

    o = np.cumsum((GLA_QK, GLA_QK, GLA_V, GLA_V, GLA_GATE_RANK, DSA_W, DSA_W, DSA_W, IDX_Q, IDX_DIM,
                   IDX_HEADS, 2 * D_MODEL))
    wb = w_in.astype(BF16)
    w_gla = wb[:, 0:o[3]]
    w_ga = _pad_cols(wb[:, o[3]:o[4]], LANES)
    w_dsa = wb[:, o[4]:o[8]]
    w_sm = _pad_cols(jnp.concatenate([wb[:, o[8]:o[9]], wb[:, o[9]:o[10]]], axis=1), LANES)
    w_bg = wb[:, o[10]:o[11]]
    w_a2 = jnp.pad(w_gla_a2.astype(BF16), ((0, LANES - GLA_GATE_RANK), (0, 0)))

    q_a, k_a, v_a, gg, la = _gla_proj(x, mod, g_pre_mix[None, :], w_gla, w_ga, w_a2, b_gla_a[None, :])
    o_a = _gla(q_a, k_a, v_a, gg, la, jnp.tile(g_gla_out, GLA_HEADS)[None, :])

    fv, sg = _rope_lane_constants()
    gik = jnp.pad(g_idx_k, (0, LANES - IDX_DIM))[None, :]
    q_b, k_b, v_b, qi, ki, wi = _dsa_proj(x, mod, g_pre_mix[None, :], pos.astype(F32)[:, None], fv, sg,
                                          w_dsa, w_sm, gik)
    nqb = s // DSA_TQ
    qi_r = qi.reshape(nqb, DSA_TQ, IDX_HEADS, IDX_DIM).transpose(0, 2, 1, 3).reshape(
        nqb, IDX_HEADS * DSA_TQ, IDX_DIM)
    o_b = _dsa(qi_r, ki.T, wi, q_b, k_b.T, v_b, min(DSA_TOPK_MAX, s // 4))

    x1 = _merge(x, mod, g_pre_mix[None, :], g_post_mix[None, :], o_a, o_b, w_bg,
                w_proj_gla.astype(BF16), w_proj_dsa.astype(BF16), w_out.astype(BF16))

    wr = _pad_cols(jnp.concatenate([w_router_g, w_router_e], axis=1).astype(BF16), LANES)
    br = jnp.pad(jnp.concatenate([b_router_g, b_router_e]), (0, LANES - N_GROUPS - N_EXPERTS))[None, :]
    wg = _experts_to_columns(w_e_gate)
    wu = _experts_to_columns(w_e_up)
    wd = _experts_to_rows(w_e_down)
    return _moe(x1, mod, g_pre_ffn[None, :], g_post_ffn[None, :], wr, br, wg, wu, wd)


def kernel(x, c, positions, w_ada, b_ada, g_pre_mix, g_post_mix, g_pre_ffn, g_post_ffn, w_in, w_gla_a2,
           b_gla_a, g_gla_out, g_idx_k, w_proj_gla, w_proj_dsa, w_out, w_router_g, b_router_g,
           w_router_e, b_router_e, w_e_gate, w_e_up, w_e_down):
    batch, depth = x.shape[0], w_ada.shape[0]
    outs = []
    for bi in range(batch):
        xb = x[bi]
        for l in range(depth):
            xb = _layer(xb, c[bi:bi + 1], positions[bi], w_ada[l], b_ada[l], g_pre_mix[l], g_post_mix[l],
                        g_pre_ffn[l], g_post_ffn[l], w_in[l], w_gla_a2[l], b_gla_a[l], g_gla_out[l],
                        g_idx_k[l], w_proj_gla[l], w_proj_dsa[l], w_out[l], w_router_g[l],
                        b_router_g[l], w_router_e[l], b_router_e[l], w_e_gate[l], w_e_up[l],
                        w_e_down[l])
        outs.append(xb)
    return jnp.stack(outs, axis=0)
```

```python
import functools

import numpy as np
import jax
import jax.numpy as jnp
from jax import lax
from jax.experimental import pallas as pl
from jax.experimental.pallas import tpu as pltpu

F32 = jnp.float32
BF16 = jnp.bfloat16
I32 = jnp.int32

D_MODEL = 1024
EPS = 1e-6
ROPE_THETA = 500000.0
ROT_FRAC = 4
GLA_HEADS = 4
GLA_DK = 64
GLA_DV = 128
GLA_GATE_RANK = 16
GLA_TAU = 16.0
GLA_CHUNK = 64
DSA_HEADS = 8
DSA_HD = 64
IDX_HEADS = 8
IDX_DIM = 64
DSA_TOPK_MAX = 256
N_GROUPS = 4
EXPERTS_PER_GROUP = 8
N_EXPERTS = N_GROUPS * EXPERTS_PER_GROUP
D_EXPERT = D_MODEL // 4
N_MOD = 6

GLA_QK = GLA_HEADS * GLA_DK
GLA_V = GLA_HEADS * GLA_DV
DSA_W = DSA_HEADS * DSA_HD
IDX_Q = IDX_HEADS * IDX_DIM

LANES = 128
VMEM_LIMIT_BYTES = 56 * 1024 * 1024

PROJ_ROWS = 1024
MIX_ROWS = 512
GLA_ROWS = 512
DSA_TQ = 256
DSA_CH = 1024
DSA_CH_ATT = 512
MOE_ROWS = 512
MOE_EB = 8
MOE_CAST_EB = 4
MASKED = -1e30
LOG2E = float(np.log2(np.e))
GLA_LEVELS = 7


def _dot(a, b):
    return jnp.dot(a, b, preferred_element_type=F32)


def _dot_nt(a, b):
    return lax.dot_general(a, b, (((1,), (1,)), ((), ())), preferred_element_type=F32)


def _dot_tn(a, b):
    return lax.dot_general(a, b, (((0,), (0,)), ((), ())), preferred_element_type=F32)


def _sigmoid(x):
    return 1.0 / (1.0 + jnp.exp(-x))


def _rms(x, g):
    ms = jnp.mean(x * x, axis=-1, keepdims=True)
    return x * lax.rsqrt(ms + EPS) * g


def _modulated(x, g, mod_ref, shift_row, scale_row):
    return (_rms(x, g) * (1.0 + mod_ref[scale_row:scale_row + 1, :])
            + mod_ref[shift_row:shift_row + 1, :])


def _params(sem):
    return pltpu.CompilerParams(dimension_semantics=sem, vmem_limit_bytes=VMEM_LIMIT_BYTES)


def _full(shape):
    return pl.BlockSpec(shape, lambda *_: (0,) * len(shape))


def _resident(shape):
    return pl.BlockSpec(shape, lambda *_: (0,) * len(shape), pipeline_mode=pl.Buffered(1))


def _ada_kernel(c_ref, w_ref, b_ref, o_ref):
    c = c_ref[...]
    a = c * _sigmoid(c)
    o_ref[...] = _dot(a.astype(BF16), w_ref[...].astype(BF16)) + b_ref[...]


def _ada(c8, w, b):
    n = w.shape[1]
    bn = 1536
    return pl.pallas_call(
        _ada_kernel,
        grid=(n // bn,),
        in_specs=[_full(c8.shape),
                  pl.BlockSpec((D_MODEL, bn), lambda j: (0, j)),
                  pl.BlockSpec((1, bn), lambda j: (0, j))],
        out_specs=pl.BlockSpec((8, bn), lambda j: (0, j)),
        out_shape=jax.ShapeDtypeStruct((8, n), F32),
        compiler_params=_params(("arbitrary",)),
        name="ada",
    )(c8, w, b)


def _gla_proj_body(hb, w_ref, wga_ref, wa2_ref, ba_ref, q_ref, k_ref, v_ref, gg_ref, la_ref):
    p = _dot(hb, w_ref[...])
    q_ref[...] = p[:, 0:GLA_QK] * (GLA_DK ** -0.5)
    k_ref[...] = p[:, GLA_QK:2 * GLA_QK]
    v_ref[...] = p[:, 2 * GLA_QK:2 * GLA_QK + GLA_V]
    gg_ref[...] = p[:, 2 * GLA_QK + GLA_V:]
    ga = _dot(hb, wga_ref[...])
    z = _dot(ga.astype(BF16), wa2_ref[...]) + ba_ref[...]
    log_sig = jnp.minimum(z, 0.0) - jnp.log1p(jnp.exp(-jnp.abs(z)))
    la_ref[...] = log_sig * (1.0 / GLA_TAU)


def _dsa_proj_body(hb, pos_ref, fv_ref, sg_ref, w_ref, wsm_ref, gik_ref,
                   qb_ref, kb_ref, vb_ref, qi_ref, ki_ref, wi_ref):
    tm = hb.shape[0]
    ang = pos_ref[...] * fv_ref[...]
    cs = jnp.cos(ang)
    sn = jnp.sin(ang) * sg_ref[...]
    lane = lax.broadcasted_iota(I32, (tm, LANES), 1)
    first = (lane & (DSA_HD - 1)) < (DSA_HD // ROT_FRAC // 2)

    def rope(t):
        width = t.shape[1]
        rep = width // LANES
        tile = (lambda a: jnp.concatenate([a] * rep, axis=1)) if rep > 1 else (lambda a: a)
        half = DSA_HD // ROT_FRAC // 2
        fwd = pltpu.roll(t, half, 1)
        bwd = pltpu.roll(t, width - half, 1)
        partner = jnp.where(tile(first), bwd, fwd)
        return t * tile(cs) + partner * tile(sn)

    p = _dot(hb, w_ref[...])
    qb_ref[...] = (rope(p[:, 0:DSA_W]) * (DSA_HD ** -0.5 * LOG2E)).astype(BF16)
    kb_ref[...] = rope(p[:, DSA_W:2 * DSA_W]).astype(BF16)
    vb_ref[...] = p[:, 2 * DSA_W:3 * DSA_W].astype(BF16)
    qi_ref[...] = (rope(p[:, 3 * DSA_W:]) * (IDX_DIM ** -0.5)).astype(BF16)

    sm = _dot(hb, wsm_ref[...])
    is_ik = lane < IDX_DIM
    mu = jnp.sum(jnp.where(is_ik, sm, 0.0), axis=-1, keepdims=True) * (1.0 / IDX_DIM)
    xc = jnp.where(is_ik, sm - mu, 0.0)
    var = jnp.sum(xc * xc, axis=-1, keepdims=True) * (1.0 / IDX_DIM)
    y = xc * lax.rsqrt(var + EPS) * gik_ref[...]
    ki_ref[...] = rope(y)[:, 0:IDX_DIM].astype(BF16)
    wi_ref[...] = sm * (IDX_HEADS ** -0.5)


def _mix_proj_kernel(x_ref, mod_ref, g_ref, wg_ref, wga_ref, wa2_ref, ba_ref, pos_ref, fv_ref, sg_ref,
                     wd_ref, wsm_ref, gik_ref, *out_refs):
    hb = _modulated(x_ref[...], g_ref[...], mod_ref, 0, 1).astype(BF16)
    _gla_proj_body(hb, wg_ref, wga_ref, wa2_ref, ba_ref, *out_refs[:5])
    _dsa_proj_body(hb, pos_ref, fv_ref, sg_ref, wd_ref, wsm_ref, gik_ref, *out_refs[5:])


def _mix_proj(x, mod, g, wg, wga, wa2, ba, pos, fv, sg, wd, wsm, gik):
    s = x.shape[0]
    tm = MIX_ROWS
    row = lambda n: pl.BlockSpec((tm, n), lambda i: (i, 0))
    sds = lambda n, dt: jax.ShapeDtypeStruct((s, n), dt)
    consts = (mod, g, wg, wga, wa2, ba)
    return pl.pallas_call(
        _mix_proj_kernel,
        grid=(s // tm,),
        in_specs=[row(D_MODEL)] + [_full(a.shape) for a in consts] + [row(1)]
                 + [_full(a.shape) for a in (fv, sg, wd, wsm, gik)],
        out_specs=[row(GLA_QK), row(GLA_QK), row(GLA_V), row(GLA_V), row(GLA_QK),
                   row(DSA_W), row(DSA_W), row(DSA_W), row(IDX_Q), row(IDX_DIM), row(LANES)],
        out_shape=[sds(GLA_QK, F32), sds(GLA_QK, F32), sds(GLA_V, F32), sds(GLA_V, F32), sds(GLA_QK, F32),
                   sds(DSA_W, BF16), sds(DSA_W, BF16), sds(DSA_W, BF16), sds(IDX_Q, BF16),
                   sds(IDX_DIM, BF16), sds(LANES, F32)],
        compiler_params=_params(("arbitrary",)),
        name="mix_proj",
    )(x, *consts, pos, fv, sg, wd, wsm, gik)


def _gla_constants():
    c = GLA_CHUNK
    tril = np.tril(np.ones((c, c), np.float32))
    t = np.arange(c)
    mats = [tril]
    masks = [(t[:, None] == t[None, :])]
    for hs in (32, 16, 8, 4, 2, 1):
        blk = 2 * hs
        r = (t // blk) * blk + hs - 1
        mats.append(tril - tril[r, :])
        same = (t[:, None] // blk) == (t[None, :] // blk)
        masks.append(same & ((t[:, None] % blk) >= hs) & ((t[None, :] % blk) < hs))
    m_all = np.concatenate(mats, axis=0)
    lvl = np.stack([np.tile(m.astype(np.float32), (1, GLA_HEADS)) for m in masks])
    hrow = np.arange(GLA_HEADS * c) // c
    wmask = (hrow[:, None] == (np.arange(GLA_QK) // GLA_DK)[None, :]).astype(np.float32)
    vmask = (hrow[:, None] == (np.arange(GLA_V) // GLA_DV)[None, :]).astype(np.float32)
    smask = ((np.arange(GLA_V) // GLA_DV)[:, None] == (np.arange(GLA_QK) // GLA_DK)[None, :])
    return (jnp.asarray(m_all, BF16), jnp.asarray(lvl), jnp.asarray(wmask), jnp.asarray(vmask),
            jnp.asarray(smask.astype(np.float32)))


def _gla_kernel(q_ref, k_ref, v_ref, gg_ref, la_ref, gout_ref, mall_ref, lvl_ref, wmask_ref,
                vmask_ref, smask_ref, o_ref, st_ref):
    c = GLA_CHUNK

    @pl.when(pl.program_id(0) == 0)
    def _():
        st_ref[...] = jnp.zeros_like(st_ref)

    m_all = mall_ref[...]
    wmask = wmask_ref[...]
    vmask = vmask_ref[...]
    smask = smask_ref[...]
    gout = gout_ref[...]

    def chunk(ci, carry):
        r0 = pl.multiple_of(ci * c, c)
        rows = pl.ds(r0, c)
        q = q_ref[rows, :]
        k = k_ref[rows, :]
        v = v_ref[rows, :]
        la = la_ref[rows, :]
        hi = la.astype(BF16)
        r1 = la - hi.astype(F32)
        mid = r1.astype(BF16)
        lo = (r1 - mid.astype(F32)).astype(BF16)
        dall = _dot(m_all, hi) + _dot(m_all, mid) + _dot(m_all, lo)
        b = dall[0:c]
        b_last = b[c - 1:c, :]
        qhat = q * jnp.exp(b)
        khat = k * jnp.exp(b_last - b)

        a = jnp.zeros((c, GLA_HEADS * c), F32)
        for lv in range(GLA_LEVELS):
            if lv == 0:
                qt, kt = q, k
            else:
                d = dall[lv * c:(lv + 1) * c]
                qt = q * jnp.exp(jnp.minimum(d, 0.0))
                kt = k * jnp.exp(jnp.minimum(-d, 0.0))
            w = (jnp.concatenate([kt] * GLA_HEADS, axis=0) * wmask).astype(BF16)
            a = a + lvl_ref[lv] * _dot_nt(qt.astype(BF16), w)

        st = st_ref[...]
        vbd = (jnp.concatenate([v] * GLA_HEADS, axis=0) * vmask).astype(BF16)
        o = _dot(a.astype(BF16), vbd) + _dot_nt(qhat.astype(BF16), st.astype(BF16))
        st_ref[...] = st * jnp.exp(b_last) + smask * _dot_tn(v.astype(BF16), khat.astype(BF16))

        parts = []
        for hh in range(GLA_HEADS):
            oh = o[:, hh * GLA_DV:(hh + 1) * GLA_DV]
            ms = jnp.mean(oh * oh, axis=-1, keepdims=True)
            parts.append(oh * lax.rsqrt(ms + EPS))
        gg = gg_ref[rows, :]
        o_ref[rows, :] = jnp.concatenate(parts, axis=1) * gout * (gg * _sigmoid(gg))
        return carry

    lax.fori_loop(0, q_ref.shape[0] // c, chunk, 0, unroll=4)


def _gla(q, k, v, gg, la, gout):
    s = q.shape[0]
    tb = GLA_ROWS
    consts = _gla_constants()
    row = lambda n: pl.BlockSpec((tb, n), lambda i: (i, 0))
    return pl.pallas_call(
        _gla_kernel,
        grid=(s // tb,),
        in_specs=[row(GLA_QK), row(GLA_QK), row(GLA_V), row(GLA_V), row(GLA_QK), _full(gout.shape)]
                 + [_full(a.shape) for a in consts],
        out_specs=row(GLA_V),
        out_shape=jax.ShapeDtypeStruct((s, GLA_V), F32),
        scratch_shapes=[pltpu.VMEM((GLA_V, GLA_QK), F32)],
        compiler_params=_params(("arbitrary",)),
        name="gla",
    )(q, k, v, gg, la, gout, *consts)


DSA_SLAB = 32
DSA_HG = 2
DSA_MAX_SHIFT = 48.0
WORD = 32
DSA_GROUP = WORD * LANES
LANE_BITS = LANES.bit_length() - 1
HD_BITS = DSA_HD.bit_length() - 1


def _float_to_ordered_bits(x):
    bits = lax.bitcast_convert_type(x, I32)
    key = bits ^ ((bits >> 31) & jnp.int32(0x7FFFFFFF))
    return key ^ jnp.int32(-2147483648)


def _transpose_bits(a):
    a = list(a)
    j, msk = 16, 0x0000FFFF
    while j:
        m32 = jnp.int32(np.array(msk, np.uint32).view(np.int32))
        k = 0
        while k < WORD:
            t = (a[k] ^ (a[k + j] >> j)) & m32
            a[k] = a[k] ^ t
            a[k + j] = a[k + j] ^ (t << j)
            k = (k + j + 1) & ~j
        j >>= 1
        msk = (msk ^ (msk << j)) & 0xFFFFFFFF
    return a


def _dsa_kernel(qi_ref, kit_ref, wi_ref, qb_ref, kt_hbm, v_hbm, o_ref,
                key_ref, alive_ref, great_ref, wb_ref, qbd_ref, s_ref, p_ref, bias_ref, m_ref, l_ref,
                alpha_ref, acc_ref, bnd_ref, kmax_ref, kbuf_ref, vbuf_ref, sem_ref, *, seq, topk):
    tq = DSA_TQ
    ch = DSA_CH
    hg = DSA_HG
    n_hg = DSA_HEADS // hg
    gw = hg * DSA_HD
    bpc = ch // LANES
    ca = DSA_CH_ATT
    bpa = ca // LANES
    cga = DSA_GROUP // ca
    n_ca = ((pl.program_id(0) + 1) * tq + ca - 1) // ca
    cpg = DSA_GROUP // ch
    n_groups = seq // DSA_GROUP
    i = pl.program_id(0)
    n_ch = ((i + 1) * tq + ch - 1) // ch
    lane_i = lax.broadcasted_iota(I32, (tq, LANES), 1)
    row_pos = lax.broadcasted_iota(I32, (tq, 1), 0) + i * tq
    kf = float(topk)

    def tile_l(a, width):
        return jnp.concatenate([a] * (width // a.shape[1]), axis=1)

    def chunk(c):
        return pl.ds(pl.multiple_of(c * ch, ch), ch)

    def key_copy(c, slot):
        cols = pl.ds(pl.multiple_of(c * ca, ca), ca)
        return pltpu.make_async_copy(kt_hbm.at[:, cols], kbuf_ref.at[slot], sem_ref.at[0, slot])

    def value_copy(c, slot):
        cols = pl.ds(pl.multiple_of(c * ca, ca), ca)
        return pltpu.make_async_copy(v_hbm.at[cols, :], vbuf_ref.at[slot], sem_ref.at[1, slot])

    def kv_copies(c, slot):
        return key_copy(c, slot), value_copy(c, slot)

    def lane_sum(a):
        return jnp.sum(a.astype(F32), axis=1, keepdims=True)

    @pl.when(i == 0)
    def _():
        key_ref[...] = jnp.zeros(key_ref.shape, I32)

        def norm_body(c, best):
            copy = key_copy(c, 0)
            copy.start()
            copy.wait()
            out = []
            for h in range(DSA_HEADS):
                kf32 = kbuf_ref[0, h * DSA_HD:(h + 1) * DSA_HD, :].astype(F32)
                out.append(jnp.maximum(best[h], jnp.sum(kf32 * kf32, axis=0, keepdims=True)))
            return tuple(out)

        best = lax.fori_loop(0, seq // ca, norm_body,
                             tuple(jnp.zeros((1, ca), F32) for _ in range(DSA_HEADS)))
        for h in range(DSA_HEADS):
            kmax_ref[h] = jnp.sqrt(jnp.max(best[h]))

    wv = wi_ref[...]
    for h in range(IDX_HEADS):
        wb_ref[h] = jnp.broadcast_to(wv[:, IDX_DIM + h:IDX_DIM + h + 1], (tq, LANES))

    def score_body(c, carry):
        lg = _dot(qi_ref[0], kit_ref[:, chunk(c)])
        sc = jnp.maximum(lg[0:tq], 0.0) * tile_l(wb_ref[0], ch)
        for h in range(1, IDX_HEADS):
            sc = sc + jnp.maximum(lg[h * tq:(h + 1) * tq], 0.0) * tile_l(wb_ref[h], ch)
        sc = jnp.where(sc == 0.0, 0.0, sc)
        key_ref[c] = _float_to_ordered_bits(sc)
        return carry

    lax.fori_loop(0, n_ch, score_body, 0)

    n_pg = (n_ch + cpg - 1) // cpg

    def plane_body(t, carry):
        g = t // (tq // 8)
        rows = pl.ds(pl.multiple_of((t % (tq // 8)) * 8, 8), 8)

        def slot(b):
            return (g * cpg + b // bpc, rows, slice((b % bpc) * LANES, (b % bpc + 1) * LANES))

        planes = _transpose_bits([key_ref[slot(WORD - 1 - k)] for k in range(WORD)])
        for b in range(WORD):
            key_ref[slot(b)] = planes[b]
        return carry

    lax.fori_loop(0, n_pg * (tq // 8), plane_body, 0)

    def index_mask(g, bound):
        r = bound - g * DSA_GROUP - lane_i
        q = jnp.clip((r + (LANES - 1)) >> LANE_BITS, 0, WORD)
        return jnp.where(q >= WORD, jnp.int32(-1), (jnp.int32(1) << jnp.minimum(q, WORD - 1)) - 1)

    for g in range(n_groups):
        alive_ref[g] = index_mask(g, row_pos + 1)
        great_ref[g] = jnp.zeros((tq, LANES), I32)

    def select_bits(ng):
        def bit_body(p, cnt_great):
            pc = p // bpc
            pl0 = pl.multiple_of((p % bpc) * LANES, LANES)
            ones = []
            acc = jnp.zeros((tq, LANES), I32)
            for g in range(ng):
                x = alive_ref[g] & key_ref[g * cpg + pc, :, pl.ds(pl0, LANES)]
                acc = acc + lax.population_count(x)
                ones.append(x)
            cnt_one = lane_sum(acc)
            take = (cnt_great + cnt_one) >= kf
            take_b = jnp.broadcast_to(take, (tq, LANES))
            for g in range(ng):
                a = alive_ref[g]
                alive_ref[g] = jnp.where(take_b, ones[g], a ^ ones[g])
                great_ref[g] = jnp.where(take_b, great_ref[g], great_ref[g] | ones[g])
            return jnp.where(take, cnt_great, cnt_great + cnt_one)

        return lambda: lax.fori_loop(0, WORD, bit_body, jnp.zeros((tq, 1), F32))

    cnt_great = lax.switch(n_pg - 1, [select_bits(ng) for ng in range(1, n_groups + 1)])
    need = kf - cnt_great

    def count_alive(bound):
        acc = jnp.zeros((tq, LANES), I32)
        for g in range(n_groups):
            acc = acc + lax.population_count(alive_ref[g] & index_mask(g, bound))
        return lane_sum(acc)

    tie_row = count_alive(jnp.full((tq, 1), seq, I32)) > need

    def tie_break(_):
        nbits = max(1, int(np.ceil(np.log2(seq))))

        def jbit(p, m):
            cand = m | jnp.left_shift(jnp.int32(1), nbits - 1 - p)
            return jnp.where(count_alive(cand) < need, cand, m)

        m = lax.fori_loop(0, nbits, jbit, jnp.zeros((tq, 1), I32))
        return jnp.where(tie_row, m + 1, jnp.int32(seq))

    any_tie = jnp.max(jnp.where(tie_row, 1.0, 0.0)) > 0.0
    bound = lax.cond(any_tie, tie_break, lambda _: jnp.full((tq, 1), seq, I32), 0)
    for g in range(n_groups):
        great_ref[g] = great_ref[g] | (alive_ref[g] & index_mask(g, bound))

    qt = qb_ref[...]
    head_of_lane = lax.broadcasted_iota(I32, (tq, gw), 1) >> HD_BITS
    for g in range(n_hg):
        qg = qt[:, g * gw:(g + 1) * gw]
        for h in range(hg):
            qbd_ref[g, h * tq:(h + 1) * tq, :] = jnp.where(head_of_lane == h, qg, jnp.zeros_like(qg))
    l_ref[...] = jnp.zeros(l_ref.shape, F32)
    acc_ref[...] = jnp.zeros(acc_ref.shape, F32)

    for g in range(n_hg):
        qf = qbd_ref[g].astype(F32)
        qn = jnp.sqrt(jnp.sum(qf * qf, axis=1, keepdims=True))
        kn = jnp.concatenate([jnp.full((tq, 1), kmax_ref[g * hg + h], F32) for h in range(hg)], axis=0)
        bnd_ref[g] = jnp.broadcast_to(qn * kn, (hg * tq, LANES))
    bound_ok = jnp.max(bnd_ref[...]) < DSA_MAX_SHIFT

    def bias_of(c):
        sel = great_ref[c // cga]
        for jj in range(bpa):
            bit = (sel >> ((c % cga) * bpa + jj)) & 1
            bias_ref[:, jj * LANES:(jj + 1) * LANES] = jnp.where(bit != 0, 0.0, MASKED)

    def over_key_chunks(body):
        for cp in kv_copies(0, 0):
            cp.start()

        def trip(c, carry):
            slot = c & 1
            for cp in kv_copies(c, slot):
                cp.wait()

            @pl.when(c + 1 < n_ca)
            def _():
                for cp in kv_copies(c + 1, 1 - slot):
                    cp.start()

            body(c, slot)
            return carry

        lax.fori_loop(0, n_ca, trip, 0)

    def bounded_body(c, slot):
        bias_of(c)
        for g in range(n_hg):
            s_ref[g] = _dot(qbd_ref[g], kbuf_ref[slot, g * gw:(g + 1) * gw, :])
        for g in range(n_hg):
            for r in range(hg * tq // DSA_SLAB):
                rows = slice(r * DSA_SLAB, (r + 1) * DSA_SLAB)
                b0 = (r * DSA_SLAB) % tq
                s = s_ref[g, rows, :] + bias_ref[b0:b0 + DSA_SLAB, :]
                p = jnp.exp2(s - tile_l(bnd_ref[g, rows, :], ca))
                l_ref[g, rows, :] = l_ref[g, rows, :] + jnp.sum(p, axis=1, keepdims=True)
                p_ref[g, rows, :] = p.astype(BF16)
            acc_ref[g] = acc_ref[g] + _dot(p_ref[g], vbuf_ref[slot, :, g * gw:(g + 1) * gw])

    def running_max_body(c, slot):
        bias_of(c)
        for g in range(n_hg):
            s_ref[g] = _dot(qbd_ref[g], kbuf_ref[slot, g * gw:(g + 1) * gw, :])
        for g in range(n_hg):
            for r in range(hg * tq // DSA_SLAB):
                rows = slice(r * DSA_SLAB, (r + 1) * DSA_SLAB)
                b0 = (r * DSA_SLAB) % tq
                s = s_ref[g, rows, :] + bias_ref[b0:b0 + DSA_SLAB, :]
                m_prev = m_ref[g, rows, :]
                m_new = jnp.maximum(m_prev, jnp.max(s, axis=1, keepdims=True))
                alpha = jnp.exp2(m_prev - m_new)
                p = jnp.exp2(s - tile_l(m_new, ca))
                l_ref[g, rows, :] = alpha * l_ref[g, rows, :] + jnp.sum(p, axis=1, keepdims=True)
                m_ref[g, rows, :] = m_new
                alpha_ref[g, rows, :] = alpha
                p_ref[g, rows, :] = p.astype(BF16)
            acc_ref[g] = (acc_ref[g] * tile_l(alpha_ref[g], gw)
                          + _dot(p_ref[g], vbuf_ref[slot, :, g * gw:(g + 1) * gw]))

    @pl.when(bound_ok)
    def _():
        over_key_chunks(bounded_body)

    @pl.when(jnp.logical_not(bound_ok))
    def _():
        m_ref[...] = jnp.full(m_ref.shape, MASKED, F32)
        over_key_chunks(running_max_body)

    outs = []
    for g in range(n_hg):
        a = acc_ref[g] * tile_l(1.0 / l_ref[g], gw)
        og = jnp.zeros((tq, gw), F32)
        for h in range(hg):
            og = og + jnp.where(head_of_lane == h, a[h * tq:(h + 1) * tq], 0.0)
        outs.append(og)
    o_ref[...] = jnp.concatenate(outs, axis=1)


def _dsa(qi_r, kit, wi, qb, kt, v, topk):
    s = qb.shape[0]
    tq = DSA_TQ
    assert s % DSA_GROUP == 0 and DSA_GROUP % DSA_CH == 0
    n_hg = DSA_HEADS // DSA_HG
    rows = DSA_HG * tq
    gw = DSA_HG * DSA_HD
    return pl.pallas_call(
        functools.partial(_dsa_kernel, seq=s, topk=topk),
        grid=(s // tq,),
        in_specs=[pl.BlockSpec((1, IDX_HEADS * tq, IDX_DIM), lambda i: (i, 0, 0)),
                  _resident(kit.shape),
                  pl.BlockSpec((tq, LANES), lambda i: (i, 0)),
                  pl.BlockSpec((tq, DSA_W), lambda i: (i, 0)),
                  pl.BlockSpec(memory_space=pl.ANY),
                  pl.BlockSpec(memory_space=pl.ANY)],
        out_specs=pl.BlockSpec((tq, DSA_W), lambda i: (i, 0)),
        out_shape=jax.ShapeDtypeStruct((s, DSA_W), F32),
        scratch_shapes=[pltpu.VMEM((s // DSA_CH, tq, DSA_CH), I32),
                        pltpu.VMEM((s // DSA_GROUP, tq, LANES), I32),
                        pltpu.VMEM((s // DSA_GROUP, tq, LANES), I32),
                        pltpu.VMEM((IDX_HEADS, tq, LANES), F32),
                        pltpu.VMEM((n_hg, rows, gw), BF16),
                        pltpu.VMEM((n_hg, rows, DSA_CH_ATT), F32),
                        pltpu.VMEM((n_hg, rows, DSA_CH_ATT), BF16),
                        pltpu.VMEM((tq, DSA_CH_ATT), F32),
                        pltpu.VMEM((n_hg, rows, LANES), F32),
                        pltpu.VMEM((n_hg, rows, LANES), F32),
                        pltpu.VMEM((n_hg, rows, LANES), F32),
                        pltpu.VMEM((n_hg, rows, gw), F32),
                        pltpu.VMEM((n_hg, rows, LANES), F32),
                        pltpu.SMEM((DSA_HEADS,), F32),
                        pltpu.VMEM((2, DSA_W, DSA_CH_ATT), BF16),
                        pltpu.VMEM((2, DSA_CH_ATT, DSA_W), BF16),
                        pltpu.SemaphoreType.DMA((2, 2))],
        compiler_params=_params(("arbitrary",)),
        name="dsa",
    )(qi_r, kit, wi, qb, kt, v)


def _merge_kernel(x_ref, mod_ref, gpre_ref, gpost_ref, oa_ref, ob_ref, wbg_ref, wpg_ref, wpd_ref,
                  wout_ref, o_ref):
    x = x_ref[...]
    h = _modulated(x, gpre_ref[...], mod_ref, 0, 1)
    gates = _sigmoid(_dot(h.astype(BF16), wbg_ref[...]))
    yg = _dot(oa_ref[...].astype(BF16), wpg_ref[...])
    yd = _dot(ob_ref[...].astype(BF16), wpd_ref[...])
    mix = gates[:, 0:D_MODEL] * yg + gates[:, D_MODEL:] * yd
    out = _dot(mix.astype(BF16), wout_ref[...])
    o_ref[...] = x + mod_ref[2:3, :] * _rms(out, gpost_ref[...])


def _merge(x, mod, gpre, gpost, oa, ob, wbg, wpg, wpd, wout):
    s = x.shape[0]
    tm = PROJ_ROWS
    row = lambda n: pl.BlockSpec((tm, n), lambda i: (i, 0))
    return pl.pallas_call(
        _merge_kernel,
        grid=(s // tm,),
        in_specs=[row(D_MODEL), _full(mod.shape), _full(gpre.shape), _full(gpost.shape),
                  row(GLA_V), row(DSA_W), _full(wbg.shape), _full(wpg.shape), _full(wpd.shape),
                  _full(wout.shape)],
        out_specs=row(D_MODEL),
        out_shape=jax.ShapeDtypeStruct((s, D_MODEL), F32),
        compiler_params=_params(("arbitrary",)),
        name="merge",
    )(x, mod, gpre, gpost, oa, ob, wbg, wpg, wpd, wout)


def _router(lg):
    t = lg.shape[0]
    lane = lax.broadcasted_iota(I32, (t, LANES), 1)
    lanef = lane.astype(F32)
    big = 1e9
    gm = lane < N_GROUPS
    gmax = jnp.max(jnp.where(gm, lg, -jnp.inf), axis=1, keepdims=True)
    gsum = jnp.sum(jnp.where(gm, jnp.exp(lg - gmax), 0.0), axis=1, keepdims=True)
    p_g = 1.0 / gsum
    g_sel = jnp.min(jnp.where(gm & (lg == gmax), lanef, big), axis=1, keepdims=True)
    lo = N_GROUPS + EXPERTS_PER_GROUP * g_sel
    em = (lanef >= lo) & (lanef < lo + EXPERTS_PER_GROUP)
    m1 = jnp.max(jnp.where(em, lg, -jnp.inf), axis=1, keepdims=True)
    i1 = jnp.min(jnp.where(em & (lg == m1), lanef, big), axis=1, keepdims=True)
    em2 = em & (lanef != i1)
    m2 = jnp.max(jnp.where(em2, lg, -jnp.inf), axis=1, keepdims=True)
    i2 = jnp.min(jnp.where(em2 & (lg == m2), lanef, big), axis=1, keepdims=True)
    e2 = jnp.exp(m2 - m1)
    inv = 1.0 / (1.0 + e2)
    return (jnp.where(lanef == i1, p_g * inv, 0.0) + jnp.where(lanef == i2, p_g * (e2 * inv), 0.0))


def _moe_kernel(x_ref, mod_ref, gpre_ref, gpost_ref, wr_ref, br_ref, wg_ref, wu_ref, wd_ref,
                o_ref, hb_ref, comb_ref, acc_ref):
    j = pl.program_id(1)
    tm = x_ref.shape[0]

    @pl.when(j == 0)
    def _():
        h = _modulated(x_ref[...], gpre_ref[...], mod_ref, 3, 4)
        hb = h.astype(BF16)
        hb_ref[...] = hb
        comb_ref[...] = _router(_dot(hb, wr_ref[...]) + br_ref[...])
        acc_ref[...] = jnp.zeros_like(acc_ref)

    hb = hb_ref[...]
    hgate = _dot(hb, wg_ref[...])
    hup = _dot(hb, wu_ref[...])
    act = hgate * _sigmoid(hgate) * hup
    comb = comb_ref[...]
    lane = lax.broadcasted_iota(I32, (tm, LANES), 1)
    parts = []
    for e in range(MOE_EB):
        sel = lane == (N_GROUPS + j * MOE_EB + e)
        cw = jnp.sum(jnp.where(sel, comb, 0.0), axis=1, keepdims=True)
        parts.append((act[:, e * D_EXPERT:(e + 1) * D_EXPERT] * cw).astype(BF16))
    acc_ref[...] += _dot(jnp.concatenate(parts, axis=1), wd_ref[...])

    @pl.when(j == pl.num_programs(1) - 1)
    def _():
        o_ref[...] = x_ref[...] + mod_ref[5:6, :] * _rms(acc_ref[...], gpost_ref[...])


def _moe(x, mod, gpre, gpost, wr, br, wg, wu, wd):
    s = x.shape[0]
    tm = MOE_ROWS
    bw = MOE_EB * D_EXPERT
    return pl.pallas_call(
        _moe_kernel,
        grid=(s // tm, N_EXPERTS // MOE_EB),
        in_specs=[pl.BlockSpec((tm, D_MODEL), lambda i, j: (i, 0)),
                  _full(mod.shape), _full(gpre.shape), _full(gpost.shape), _full(wr.shape),
                  _full(br.shape),
                  pl.BlockSpec((D_MODEL, bw), lambda i, j: (0, j)),
                  pl.BlockSpec((D_MODEL, bw), lambda i, j: (0, j)),
                  pl.BlockSpec((bw, D_MODEL), lambda i, j: (j, 0))],
        out_specs=pl.BlockSpec((tm, D_MODEL), lambda i, j: (i, 0)),
        out_shape=jax.ShapeDtypeStruct((s, D_MODEL), F32),
        scratch_shapes=[pltpu.VMEM((tm, D_MODEL), BF16), pltpu.VMEM((tm, LANES), F32),
                        pltpu.VMEM((tm, D_MODEL), F32)],
        compiler_params=_params(("arbitrary", "arbitrary")),
        name="moe",
    )(x, mod, gpre, gpost, wr, br, wg, wu, wd)


def _cast_columns_kernel(x_ref, o_ref):
    f = x_ref.shape[2]
    for j in range(x_ref.shape[0]):
        o_ref[:, j * f:(j + 1) * f] = x_ref[j].astype(BF16)


def _cast_rows_kernel(x_ref, o_ref):
    r = x_ref.shape[1]
    for j in range(x_ref.shape[0]):
        o_ref[j * r:(j + 1) * r, :] = x_ref[j].astype(BF16)


def _experts_to_columns(w):
    e, d, f = w.shape
    eb = MOE_CAST_EB
    return pl.pallas_call(
        _cast_columns_kernel,
        grid=(e // eb,),
        in_specs=[pl.BlockSpec((eb, d, f), lambda i: (i, 0, 0))],
        out_specs=pl.BlockSpec((d, eb * f), lambda i: (0, i)),
        out_shape=jax.ShapeDtypeStruct((d, e * f), BF16),
        compiler_params=_params(("arbitrary",)),
        name="expert_cast_cols",
    )(w)


def _experts_to_rows(w):
    e, f, d = w.shape
    eb = MOE_CAST_EB
    return pl.pallas_call(
        _cast_rows_kernel,
        grid=(e // eb,),
        in_specs=[pl.BlockSpec((eb, f, d), lambda i: (i, 0, 0))],
        out_specs=pl.BlockSpec((eb * f, d), lambda i: (i, 0)),
        out_shape=jax.ShapeDtypeStruct((e * f, d), BF16),
        compiler_params=_params(("arbitrary",)),
        name="expert_cast_rows",
    )(w)


def _rope_lane_constants():
    rot = DSA_HD // ROT_FRAC
    half = rot // 2
    freqs = np.power(np.float32(ROPE_THETA), -np.arange(half, dtype=np.float32) * np.float32(2.0) / rot)
    j = np.arange(LANES) % DSA_HD
    fv = np.where(j < rot, freqs[j % half], 0.0).astype(np.float32)
    sg = np.where(j < half, -1.0, np.where(j < rot, 1.0, 0.0)).astype(np.float32)
    return jnp.asarray(fv)[None, :], jnp.asarray(sg)[None, :]


def _pad_cols(w, n):
    return jnp.pad(w, ((0, 0), (0, n - w.shape[1])))


def _layer(x, c, pos, w_ada, b_ada, g_pre_mix, g_post_mix, g_pre_ffn, g_post_ffn, w_in, w_gla_a2,
           b_gla_a, g_gla_out, g_idx_k, w_proj_gla, w_proj_dsa, w_out, w_router_g, b_router_g,
           w_router_e, b_router_e, w_e_gate, w_e_up, w_e_down):
    s = x.shape[0]
    mod = _ada(jnp.broadcast_to(c, (8, D_MODEL)), w_ada, b_ada[None, :])[0].reshape(N_MOD, D_MODEL)

    o = np.cumsum((GLA_QK, GLA_QK, GLA_V, GLA_V, GLA_GATE_RANK, DSA_W, DSA_W, DSA_W, IDX_Q, IDX_DIM,
                   IDX_HEADS, 2 * D_MODEL))
    wb = w_in.astype(BF16)
    w_gla = wb[:, 0:o[3]]
    w_ga = _pad_cols(wb[:, o[3]:o[4]], LANES)
    w_dsa = wb[:, o[4]:o[8]]
    w_sm = _pad_cols(jnp.concatenate([wb[:, o[8]:o[9]], wb[:, o[9]:o[10]]], axis=1), LANES)
    w_bg = wb[:, o[10]:o[11]]
    w_a2 = jnp.pad(w_gla_a2.astype(BF16), ((0, LANES - GLA_GATE_RANK), (0, 0)))

    fv, sg = _rope_lane_constants()
    gik = jnp.pad(g_idx_k, (0, LANES - IDX_DIM))[None, :]
    q_a, k_a, v_a, gg, la, q_b, k_b, v_b, qi, ki, wi = _mix_proj(
        x, mod, g_pre_mix[None, :], w_gla, w_ga, w_a2, b_gla_a[None, :], pos.astype(F32)[:, None], fv, sg,
        w_dsa, w_sm, gik)
    o_a = _gla(q_a, k_a, v_a, gg, la, jnp.tile(g_gla_out, GLA_HEADS)[None, :])
    nqb = s // DSA_TQ
    qi_r = qi.reshape(nqb, DSA_TQ, IDX_HEADS, IDX_DIM).transpose(0, 2, 1, 3).reshape(
        nqb, IDX_HEADS * DSA_TQ, IDX_DIM)
    o_b = _dsa(qi_r, ki.T, wi, q_b, k_b.T, v_b, min(DSA_TOPK_MAX, s // 4))

    x1 = _merge(x, mod, g_pre_mix[None, :], g_post_mix[None, :], o_a, o_b, w_bg,
                w_proj_gla.astype(BF16), w_proj_dsa.astype(BF16), w_out.astype(BF16))

    wr = _pad_cols(jnp.concatenate([w_router_g, w_router_e], axis=1).astype(BF16), LANES)
    br = jnp.pad(jnp.concatenate([b_router_g, b_router_e]), (0, LANES - N_GROUPS - N_EXPERTS))[None, :]
    wg = _experts_to_columns(w_e_gate)
    wu = _experts_to_columns(w_e_up)
    wd = _experts_to_rows(w_e_down)
    return _moe(x1, mod, g_pre_ffn[None, :], g_post_ffn[None, :], wr, br, wg, wu, wd)


def kernel(x, c, positions, w_ada, b_ada, g_pre_mix, g_post_mix, g_pre_ffn, g_post_ffn, w_in, w_gla_a2,
           b_gla_a, g_gla_out, g_idx_k, w_proj_gla, w_proj_dsa, w_out, w_router_g, b_router_g,
           w_router_e, b_router_e, w_e_gate, w_e_up, w_e_down):
    batch, depth = x.shape[0], w_ada.shape[0]
    outs = []
    for bi in range(batch):
        xb = x[bi]
        for l in range(depth):
            xb = _layer(xb, c[bi:bi + 1], positions[bi], w_ada[l], b_ada[l], g_pre_mix[l], g_post_mix[l],
                        g_pre_ffn[l], g_post_ffn[l], w_in[l], w_gla_a2[l], b_gla_a[l], g_gla_out[l],
                        g_idx_k[l], w_proj_gla[l], w_proj_dsa[l], w_out[l], w_router_g[l],
                        b_router_g[l], w_router_e[l], b_router_e[l], w_e_gate[l], w_e_up[l],
                        w_e_down[l])
        outs.append(xb)
    return jnp.stack(outs, axis=0)
```
